```python
import math
import jax, jax.numpy as jnp
from jax import lax
import numpy as np

D_MODEL = 1024
BATCH = 8
SEQ = 8192
DEPTH = 2

CHUNK = 64
PLE_DIM = 256
N_A = max(1, DEPTH // 2)
N_B = DEPTH - N_A
SSM_WIDTH = D_MODEL
GROUP_SIZE = 16
N_GROUPS = SSM_WIDTH // GROUP_SIZE
STATE = 64
DT_MIN = 0.001
DT_MAX = 0.1
HEAD_DIM = 64
SB_WIDTH = D_MODEL
N_HEADS = SB_WIDTH // HEAD_DIM
Q_BLOCK = 128
EPS = 1e-6

kernel_name = "hybrid_s5_stickbreaking_yoco"


def rms_norm(x, g):
    xf = x.astype(jnp.float32)
    y = xf * lax.rsqrt(jnp.mean(xf * xf, axis=-1, keepdims=True) + EPS)
    return (y * g.astype(jnp.float32)).astype(x.dtype)


def _linear_recurrence_combine(left, right):
    ar1, ai1, br1, bi1 = left
    ar2, ai2, br2, bi2 = right
    return (ar2 * ar1 - ai2 * ai1,
            ar2 * ai1 + ai2 * ar1,
            ar2 * br1 - ai2 * bi1 + br2,
            ar2 * bi1 + ai2 * br1 + bi2)


def s5_scan(u, lam_re, lam_im, log_dt, b_re, b_im, c_re, c_im):
    bsz, seq, _ = u.shape
    f32 = jnp.float32
    lr = jnp.minimum(lam_re.astype(f32), -1e-4)
    li = lam_im.astype(f32)
    dt = jnp.exp(log_dt.astype(f32))[:, None]
    mag = jnp.exp(lr * dt)
    a_re = mag * jnp.cos(li * dt)
    a_im = mag * jnp.sin(li * dt)
    den = lr * lr + li * li
    nr = a_re - 1.0
    f_re = (nr * lr + a_im * li) / den
    f_im = (a_im * lr - nr * li) / den
    br = b_re.astype(f32)
    bi = b_im.astype(f32)
    bb_re = f_re[..., None] * br - f_im[..., None] * bi
    bb_im = f_re[..., None] * bi + f_im[..., None] * br
    cr = c_re.astype(f32)
    ci = c_im.astype(f32)
    n_chunks = seq // CHUNK
    uc = u.astype(f32).reshape(bsz, n_chunks, CHUNK, N_GROUPS, GROUP_SIZE).transpose(1, 0, 2, 3, 4)

    def step(carry, u_chunk):
        h_re, h_im = carry
        x_re = jnp.einsum('bcgh,gph->bcgp', u_chunk, bb_re)
        x_im = jnp.einsum('bcgh,gph->bcgp', u_chunk, bb_im)
        x_re = x_re.at[:, 0].add(a_re * h_re - a_im * h_im)
        x_im = x_im.at[:, 0].add(a_re * h_im + a_im * h_re)
        ar = jnp.broadcast_to(a_re, x_re.shape)
        ai = jnp.broadcast_to(a_im, x_im.shape)
        _, _, s_re, s_im = lax.associative_scan(_linear_recurrence_combine, (ar, ai, x_re, x_im), axis=1)
        y = jnp.einsum('bcgp,ghp->bcgh', s_re, cr) - jnp.einsum('bcgp,ghp->bcgh', s_im, ci)
        return (s_re[:, -1], s_im[:, -1]), y

    h0 = jnp.zeros((bsz, N_GROUPS, STATE), f32)
    _, ys = lax.scan(step, (h0, h0), uc)
    return ys.transpose(1, 0, 2, 3, 4).reshape(bsz, seq, N_GROUPS * GROUP_SIZE)


def s5_mixer(h, w_in, lam_re, lam_im, log_dt, b_re, b_im, c_re, c_im, d_skip, w_glu, b_glu, w_out):
    f32 = jnp.float32
    proj = h @ w_in
    u, gate = jnp.split(proj, 2, axis=-1)
    y = s5_scan(u, lam_re, lam_im, log_dt, b_re, b_im, c_re, c_im) + d_skip.astype(f32) * u.astype(f32)
    g = jax.nn.gelu(y)
    y = g * jax.nn.sigmoid(g @ w_glu.astype(f32) + b_glu.astype(f32))
    y = y.astype(h.dtype) * jax.nn.silu(gate)
    return y @ w_out


def stick_breaking_attention(q, k, v):
    f32 = jnp.float32
    seq = q.shape[2]
    scale = HEAD_DIM ** -0.5
    outs = []
    for start in range(0, seq, Q_BLOCK):
        end = start + Q_BLOCK
        qb = q[:, :, start:end].astype(f32)
        kp = k[:, :, :end].astype(f32)
        vp = v[:, :, :end].astype(f32)
        z = jnp.einsum('bhqd,bhkd->bhqk', qb, kp) * scale
        mask = jnp.arange(end)[None, :] < jnp.arange(start, end)[:, None]
        log_keep = jnp.where(mask, jax.nn.log_sigmoid(-z), 0.0)
        tail = lax.cumsum(log_keep, axis=3, reverse=True) - log_keep
        w = jnp.where(mask, jnp.exp(jax.nn.log_sigmoid(z) + tail), 0.0)
        outs.append(jnp.einsum('bhqk,bhkd->bhqd', w, vp))
    return jnp.concatenate(outs, axis=2)


def split_heads(t):
    bsz, seq, _ = t.shape
    return t.reshape(bsz, seq, N_HEADS, HEAD_DIM).transpose(0, 2, 1, 3)


def shared_kv(x, kv_norm, w_kv):
    kv = rms_norm(x, kv_norm) @ w_kv
    k, v = jnp.split(kv, 2, axis=-1)
    return split_heads(k), split_heads(v)


def stick_breaking_mixer(h, k, v, w_in, w_out):
    bsz, seq, _ = h.shape
    proj = h @ w_in
    q, gate = jnp.split(proj, 2, axis=-1)
    o = stick_breaking_attention(split_heads(q), k, v)
    o = o.transpose(0, 2, 1, 3).reshape(bsz, seq, SB_WIDTH).astype(h.dtype)
    return (o * jax.nn.silu(gate)) @ w_out


def _fwd_setup_inputs(seed: int = 0) -> dict:
    key = jax.random.key(seed)
    ks = jax.random.split(key, 24)
    f32 = jnp.float32
    nrm = lambda k, shape, s: jax.random.normal(k, shape, f32) * s
    gain = lambda k, shape: 1.0 + 0.05 * jax.random.normal(k, shape, f32)
    lam_im_base = jnp.pi * jnp.arange(STATE, dtype=f32)
    return {
        'x': jax.random.normal(ks[0], (BATCH, SEQ, D_MODEL), f32),
        'p': jax.random.normal(ks[1], (DEPTH, BATCH, SEQ, PLE_DIM), f32),
        'a_norm_pre': gain(ks[2], (N_A, D_MODEL)),
        'a_norm_post': gain(ks[3], (N_A, D_MODEL)),
        'a_w_in': nrm(ks[4], (N_A, D_MODEL, 2 * SSM_WIDTH), D_MODEL ** -0.5),
        'a_lam_re': -0.5 + nrm(ks[5], (N_A, N_GROUPS, STATE), 0.01),
        'a_lam_im': lam_im_base + nrm(ks[6], (N_A, N_GROUPS, STATE), 0.01),
        'a_log_dt': jax.random.uniform(ks[7], (N_A, N_GROUPS), f32, math.log(DT_MIN), math.log(DT_MAX)),
        'a_b_re': nrm(ks[8], (N_A, N_GROUPS, STATE, GROUP_SIZE), (2.0 * GROUP_SIZE) ** -0.5),
        'a_b_im': nrm(ks[9], (N_A, N_GROUPS, STATE, GROUP_SIZE), (2.0 * GROUP_SIZE) ** -0.5),
        'a_c_re': nrm(ks[10], (N_A, N_GROUPS, GROUP_SIZE, STATE), (2.0 * STATE) ** -0.5),
        'a_c_im': nrm(ks[11], (N_A, N_GROUPS, GROUP_SIZE, STATE), (2.0 * STATE) ** -0.5),
        'a_d_skip': nrm(ks[12], (N_A, SSM_WIDTH), 1.0),
        'a_w_glu': nrm(ks[13], (N_A, SSM_WIDTH, SSM_WIDTH), SSM_WIDTH ** -0.5),
        'a_b_glu': nrm(ks[14], (N_A, SSM_WIDTH), 0.01),
        'a_w_out': nrm(ks[15], (N_A, SSM_WIDTH, D_MODEL), SSM_WIDTH ** -0.5),
        'kv_norm': gain(ks[16], (D_MODEL,)),
        'w_kv': nrm(ks[17], (D_MODEL, 2 * SB_WIDTH), D_MODEL ** -0.5),
        'b_norm_pre': gain(ks[18], (N_B, D_MODEL)),
        'b_norm_post': gain(ks[19], (N_B, D_MODEL)),
        'b_w_in': nrm(ks[20], (N_B, D_MODEL, 2 * SB_WIDTH), D_MODEL ** -0.5),
        'b_w_out': nrm(ks[21], (N_B, SB_WIDTH, D_MODEL), SB_WIDTH ** -0.5),
        'ple_w_proj': nrm(ks[22], (DEPTH, PLE_DIM, D_MODEL), PLE_DIM ** -0.5),
        'ple_w_gate': nrm(ks[23], (DEPTH, D_MODEL, D_MODEL), D_MODEL ** -0.5),
    }


def _fwd_reference(x, p, a_norm_pre, a_norm_post, a_w_in, a_lam_re, a_lam_im, a_log_dt, a_b_re, a_b_im,
              a_c_re, a_c_im, a_d_skip, a_w_glu, a_b_glu, a_w_out, kv_norm, w_kv,
              b_norm_pre, b_norm_post, b_w_in, b_w_out, ple_w_proj, ple_w_gate):
    k = None
    v = None
    for i in range(DEPTH):
        if i < N_A:
            j = i
            h = rms_norm(x, a_norm_pre[j])
            y = s5_mixer(h, a_w_in[j], a_lam_re[j], a_lam_im[j], a_log_dt[j], a_b_re[j], a_b_im[j],
                         a_c_re[j], a_c_im[j], a_d_skip[j], a_w_glu[j], a_b_glu[j], a_w_out[j])
            x = x + rms_norm(y, a_norm_post[j])
        else:
            j = i - N_A
            h = rms_norm(x, b_norm_pre[j])
            y = stick_breaking_mixer(h, k, v, b_w_in[j], b_w_out[j])
            x = x + rms_norm(y, b_norm_post[j])
        x = x + jax.nn.sigmoid(x @ ple_w_gate[i]) * (p[i] @ ple_w_proj[i])
        if i == N_A - 1:
            k, v = shared_kv(x, kv_norm, w_kv)
    return x


import jax as _jax
import jax.numpy as _jnp

TWIN_FORMAT = 'train_step'
FWD_PARAMS = ['x', 'p', 'a_norm_pre', 'a_norm_post', 'a_w_in', 'a_lam_re', 'a_lam_im', 'a_log_dt', 'a_b_re', 'a_b_im', 'a_c_re', 'a_c_im', 'a_d_skip', 'a_w_glu', 'a_b_glu', 'a_w_out', 'kv_norm', 'w_kv', 'b_norm_pre', 'b_norm_post', 'b_w_in', 'b_w_out', 'ple_w_proj', 'ple_w_gate']
TWIN_WEIGHTS = ['a_norm_pre', 'a_norm_post', 'a_w_in', 'a_lam_re', 'a_lam_im', 'a_log_dt', 'a_b_re', 'a_b_im', 'a_c_re', 'a_c_im', 'a_d_skip', 'a_w_glu', 'a_b_glu', 'a_w_out', 'kv_norm', 'w_kv', 'b_norm_pre', 'b_norm_post', 'b_w_in', 'b_w_out', 'ple_w_proj', 'ple_w_gate']
TWIN_DIFF_INPUT = 'x'
TWIN_INPUTS = ['x', 'p', 'a_norm_pre', 'a_norm_post', 'a_w_in', 'a_lam_re', 'a_lam_im', 'a_log_dt', 'a_b_re', 'a_b_im', 'a_c_re', 'a_c_im', 'a_d_skip', 'a_w_glu', 'a_b_glu', 'a_w_out', 'kv_norm', 'w_kv', 'b_norm_pre', 'b_norm_post', 'b_w_in', 'b_w_out', 'ple_w_proj', 'ple_w_gate', 'loss_target', 'm_a_norm_pre', 'm_a_norm_post', 'm_a_w_in', 'm_a_lam_re', 'm_a_lam_im', 'm_a_log_dt', 'm_a_b_re', 'm_a_b_im', 'm_a_c_re', 'm_a_c_im', 'm_a_d_skip', 'm_a_w_glu', 'm_a_b_glu', 'm_a_w_out', 'm_kv_norm', 'm_w_kv', 'm_b_norm_pre', 'm_b_norm_post', 'm_b_w_in', 'm_b_w_out', 'm_ple_w_proj', 'm_ple_w_gate', 'v_a_norm_pre', 'v_a_norm_post', 'v_a_w_in', 'v_a_lam_re', 'v_a_lam_im', 'v_a_log_dt', 'v_a_b_re', 'v_a_b_im', 'v_a_c_re', 'v_a_c_im', 'v_a_d_skip', 'v_a_w_glu', 'v_a_b_glu', 'v_a_w_out', 'v_kv_norm', 'v_w_kv', 'v_b_norm_pre', 'v_b_norm_post', 'v_b_w_in', 'v_b_w_out', 'v_ple_w_proj', 'v_ple_w_gate']
TWIN_OUTPUTS = ['loss', 'grad_x', 'grad_a_norm_pre', 'grad_a_norm_post', 'grad_a_w_in', 'grad_a_lam_re', 'grad_a_lam_im', 'grad_a_log_dt', 'grad_a_b_re', 'grad_a_b_im', 'grad_a_c_re', 'grad_a_c_im', 'grad_a_d_skip', 'grad_a_w_glu', 'grad_a_b_glu', 'grad_a_w_out', 'grad_kv_norm', 'grad_w_kv', 'grad_b_norm_pre', 'grad_b_norm_post', 'grad_b_w_in', 'grad_b_w_out', 'grad_ple_w_proj', 'grad_ple_w_gate', 'delta_a_norm_pre', 'delta_a_norm_post', 'delta_a_w_in', 'delta_a_lam_re', 'delta_a_lam_im', 'delta_a_log_dt', 'delta_a_b_re', 'delta_a_b_im', 'delta_a_c_re', 'delta_a_c_im', 'delta_a_d_skip', 'delta_a_w_glu', 'delta_a_b_glu', 'delta_a_w_out', 'delta_kv_norm', 'delta_w_kv', 'delta_b_norm_pre', 'delta_b_norm_post', 'delta_b_w_in', 'delta_b_w_out', 'delta_ple_w_proj', 'delta_ple_w_gate', 'new_m_a_norm_pre', 'new_m_a_norm_post', 'new_m_a_w_in', 'new_m_a_lam_re', 'new_m_a_lam_im', 'new_m_a_log_dt', 'new_m_a_b_re', 'new_m_a_b_im', 'new_m_a_c_re', 'new_m_a_c_im', 'new_m_a_d_skip', 'new_m_a_w_glu', 'new_m_a_b_glu', 'new_m_a_w_out', 'new_m_kv_norm', 'new_m_w_kv', 'new_m_b_norm_pre', 'new_m_b_norm_post', 'new_m_b_w_in', 'new_m_b_w_out', 'new_m_ple_w_proj', 'new_m_ple_w_gate', 'new_v_a_norm_pre', 'new_v_a_norm_post', 'new_v_a_w_in', 'new_v_a_lam_re', 'new_v_a_lam_im', 'new_v_a_log_dt', 'new_v_a_b_re', 'new_v_a_b_im', 'new_v_a_c_re', 'new_v_a_c_im', 'new_v_a_d_skip', 'new_v_a_w_glu', 'new_v_a_b_glu', 'new_v_a_w_out', 'new_v_kv_norm', 'new_v_w_kv', 'new_v_b_norm_pre', 'new_v_b_norm_post', 'new_v_b_w_in', 'new_v_b_w_out', 'new_v_ple_w_proj', 'new_v_ple_w_gate']
TWIN_LEAF_KINDS = {'loss': 'loss', 'grad_x': 'grad_x', 'grad_a_norm_pre': 'grad_w', 'grad_a_norm_post': 'grad_w', 'grad_a_w_in': 'grad_w', 'grad_a_lam_re': 'grad_w', 'grad_a_lam_im': 'grad_w', 'grad_a_log_dt': 'grad_w', 'grad_a_b_re': 'grad_w', 'grad_a_b_im': 'grad_w', 'grad_a_c_re': 'grad_w', 'grad_a_c_im': 'grad_w', 'grad_a_d_skip': 'grad_w', 'grad_a_w_glu': 'grad_w', 'grad_a_b_glu': 'grad_w', 'grad_a_w_out': 'grad_w', 'grad_kv_norm': 'grad_w', 'grad_w_kv': 'grad_w', 'grad_b_norm_pre': 'grad_w', 'grad_b_norm_post': 'grad_w', 'grad_b_w_in': 'grad_w', 'grad_b_w_out': 'grad_w', 'grad_ple_w_proj': 'grad_w', 'grad_ple_w_gate': 'grad_w', 'delta_a_norm_pre': 'delta_w', 'delta_a_norm_post': 'delta_w', 'delta_a_w_in': 'delta_w', 'delta_a_lam_re': 'delta_w', 'delta_a_lam_im': 'delta_w', 'delta_a_log_dt': 'delta_w', 'delta_a_b_re': 'delta_w', 'delta_a_b_im': 'delta_w', 'delta_a_c_re': 'delta_w', 'delta_a_c_im': 'delta_w', 'delta_a_d_skip': 'delta_w', 'delta_a_w_glu': 'delta_w', 'delta_a_b_glu': 'delta_w', 'delta_a_w_out': 'delta_w', 'delta_kv_norm': 'delta_w', 'delta_w_kv': 'delta_w', 'delta_b_norm_pre': 'delta_w', 'delta_b_norm_post': 'delta_w', 'delta_b_w_in': 'delta_w', 'delta_b_w_out': 'delta_w', 'delta_ple_w_proj': 'delta_w', 'delta_ple_w_gate': 'delta_w', 'new_m_a_norm_pre': 'new_m', 'new_m_a_norm_post': 'new_m', 'new_m_a_w_in': 'new_m', 'new_m_a_lam_re': 'new_m', 'new_m_a_lam_im': 'new_m', 'new_m_a_log_dt': 'new_m', 'new_m_a_b_re': 'new_m', 'new_m_a_b_im': 'new_m', 'new_m_a_c_re': 'new_m', 'new_m_a_c_im': 'new_m', 'new_m_a_d_skip': 'new_m', 'new_m_a_w_glu': 'new_m', 'new_m_a_b_glu': 'new_m', 'new_m_a_w_out': 'new_m', 'new_m_kv_norm': 'new_m', 'new_m_w_kv': 'new_m', 'new_m_b_norm_pre': 'new_m', 'new_m_b_norm_post': 'new_m', 'new_m_b_w_in': 'new_m', 'new_m_b_w_out': 'new_m', 'new_m_ple_w_proj': 'new_m', 'new_m_ple_w_gate': 'new_m', 'new_v_a_norm_pre': 'new_v', 'new_v_a_norm_post': 'new_v', 'new_v_a_w_in': 'new_v', 'new_v_a_lam_re': 'new_v', 'new_v_a_lam_im': 'new_v', 'new_v_a_log_dt': 'new_v', 'new_v_a_b_re': 'new_v', 'new_v_a_b_im': 'new_v', 'new_v_a_c_re': 'new_v', 'new_v_a_c_im': 'new_v', 'new_v_a_d_skip': 'new_v', 'new_v_a_w_glu': 'new_v', 'new_v_a_b_glu': 'new_v', 'new_v_a_w_out': 'new_v', 'new_v_kv_norm': 'new_v', 'new_v_w_kv': 'new_v', 'new_v_b_norm_pre': 'new_v', 'new_v_b_norm_post': 'new_v', 'new_v_b_w_in': 'new_v', 'new_v_b_w_out': 'new_v', 'new_v_ple_w_proj': 'new_v', 'new_v_ple_w_gate': 'new_v'}


def _forward(args):
    return _fwd_reference(*[args[k] for k in FWD_PARAMS])


def _output_shape():
    def fwd():
        inp = _fwd_setup_inputs(0)
        return _fwd_reference(*[inp[k] for k in FWD_PARAMS])
    out = _jax.eval_shape(fwd)
    return out.shape, out.dtype

N_MICROBATCH = 1
ADAM_LR = 0.001
ADAM_B1 = 0.9
ADAM_B2 = 0.999
ADAM_EPS = 1e-08
ADAM_WD = 0.01
ADAM_STEP = 10
PER_EXAMPLE_BATCH_AXIS = {'x': 0, 'p': 1, 'loss_target': 0}
SHARED_INPUTS = []
_WEIGHT_DTYPES = {'a_norm_pre': _jnp.float32, 'a_norm_post': _jnp.float32, 'a_w_in': _jnp.float32, 'a_lam_re': _jnp.float32, 'a_lam_im': _jnp.float32, 'a_log_dt': _jnp.float32, 'a_b_re': _jnp.float32, 'a_b_im': _jnp.float32, 'a_c_re': _jnp.float32, 'a_c_im': _jnp.float32, 'a_d_skip': _jnp.float32, 'a_w_glu': _jnp.float32, 'a_b_glu': _jnp.float32, 'a_w_out': _jnp.float32, 'kv_norm': _jnp.float32, 'w_kv': _jnp.float32, 'b_norm_pre': _jnp.float32, 'b_norm_post': _jnp.float32, 'b_w_in': _jnp.float32, 'b_w_out': _jnp.float32, 'ple_w_proj': _jnp.float32, 'ple_w_gate': _jnp.float32}
MOMENT_SCALE = {'a_norm_pre': 1.203851e+00, 'a_norm_post': 6.642413e+01, 'a_w_in': 8.824119e-01, 'a_lam_re': 4.675946e-02, 'a_lam_im': 4.106118e-02, 'a_log_dt': 4.525262e+01, 'a_b_re': 2.752586e-02, 'a_b_im': 2.854204e-02, 'a_c_re': 5.579921e-02, 'a_c_im': 5.605527e-02, 'a_d_skip': 1.736778e+00, 'a_w_glu': 2.539522e-01, 'a_b_glu': 5.701042e-01, 'a_w_out': 1.454046e+00, 'kv_norm': 7.476637e-01, 'w_kv': 5.332202e-01, 'b_norm_pre': 6.205547e-01, 'b_norm_post': 6.490922e+01, 'b_w_in': 4.558437e-01, 'b_w_out': 7.283736e-01, 'ple_w_proj': 9.577702e-01, 'ple_w_gate': 7.212724e-01}


def _to_microbatches(a, axis):
    t = _jnp.moveaxis(a, axis, 0)
    t = t.reshape((N_MICROBATCH, t.shape[0] // N_MICROBATCH) + t.shape[1:])
    return _jnp.moveaxis(t, 1, axis + 1)


def setup_inputs(seed: int = 0) -> dict:
    inp = _fwd_setup_inputs(seed)
    key = _jax.random.fold_in(_jax.random.key(seed), 7919)
    shape, _ = _output_shape()
    out = dict(inp)
    out["loss_target"] = _jax.random.normal(_jax.random.fold_in(key, 0), shape, _jnp.float32)
    for i, name in enumerate(TWIN_WEIGHTS):
        w = inp[name].astype(_jnp.float32)
        if MOMENT_SCALE is None:
            s = _jnp.sqrt(_jnp.mean(_jnp.square(w)) + 1e-30)
        else:
            s = MOMENT_SCALE[name]
        km, kv = _jax.random.split(_jax.random.fold_in(key, i + 1))
        out[name] = w
        out["m_" + name] = s * _jax.random.normal(km, w.shape, _jnp.float32)
        out["v_" + name] = (s * s) * _jax.random.uniform(kv, w.shape, _jnp.float32, 0.5, 1.5)
    if N_MICROBATCH > 1:
        for name, axis in PER_EXAMPLE_BATCH_AXIS.items():
            out[name] = _to_microbatches(out[name], axis)
    return {'x': out['x'], 'p': out['p'], 'a_norm_pre': out['a_norm_pre'], 'a_norm_post': out['a_norm_post'], 'a_w_in': out['a_w_in'], 'a_lam_re': out['a_lam_re'], 'a_lam_im': out['a_lam_im'], 'a_log_dt': out['a_log_dt'], 'a_b_re': out['a_b_re'], 'a_b_im': out['a_b_im'], 'a_c_re': out['a_c_re'], 'a_c_im': out['a_c_im'], 'a_d_skip': out['a_d_skip'], 'a_w_glu': out['a_w_glu'], 'a_b_glu': out['a_b_glu'], 'a_w_out': out['a_w_out'], 'kv_norm': out['kv_norm'], 'w_kv': out['w_kv'], 'b_norm_pre': out['b_norm_pre'], 'b_norm_post': out['b_norm_post'], 'b_w_in': out['b_w_in'], 'b_w_out': out['b_w_out'], 'ple_w_proj': out['ple_w_proj'], 'ple_w_gate': out['ple_w_gate'], 'loss_target': out['loss_target'], 'm_a_norm_pre': out['m_a_norm_pre'], 'm_a_norm_post': out['m_a_norm_post'], 'm_a_w_in': out['m_a_w_in'], 'm_a_lam_re': out['m_a_lam_re'], 'm_a_lam_im': out['m_a_lam_im'], 'm_a_log_dt': out['m_a_log_dt'], 'm_a_b_re': out['m_a_b_re'], 'm_a_b_im': out['m_a_b_im'], 'm_a_c_re': out['m_a_c_re'], 'm_a_c_im': out['m_a_c_im'], 'm_a_d_skip': out['m_a_d_skip'], 'm_a_w_glu': out['m_a_w_glu'], 'm_a_b_glu': out['m_a_b_glu'], 'm_a_w_out': out['m_a_w_out'], 'm_kv_norm': out['m_kv_norm'], 'm_w_kv': out['m_w_kv'], 'm_b_norm_pre': out['m_b_norm_pre'], 'm_b_norm_post': out['m_b_norm_post'], 'm_b_w_in': out['m_b_w_in'], 'm_b_w_out': out['m_b_w_out'], 'm_ple_w_proj': out['m_ple_w_proj'], 'm_ple_w_gate': out['m_ple_w_gate'], 'v_a_norm_pre': out['v_a_norm_pre'], 'v_a_norm_post': out['v_a_norm_post'], 'v_a_w_in': out['v_a_w_in'], 'v_a_lam_re': out['v_a_lam_re'], 'v_a_lam_im': out['v_a_lam_im'], 'v_a_log_dt': out['v_a_log_dt'], 'v_a_b_re': out['v_a_b_re'], 'v_a_b_im': out['v_a_b_im'], 'v_a_c_re': out['v_a_c_re'], 'v_a_c_im': out['v_a_c_im'], 'v_a_d_skip': out['v_a_d_skip'], 'v_a_w_glu': out['v_a_w_glu'], 'v_a_b_glu': out['v_a_b_glu'], 'v_a_w_out': out['v_a_w_out'], 'v_kv_norm': out['v_kv_norm'], 'v_w_kv': out['v_w_kv'], 'v_b_norm_pre': out['v_b_norm_pre'], 'v_b_norm_post': out['v_b_norm_post'], 'v_b_w_in': out['v_b_w_in'], 'v_b_w_out': out['v_b_w_out'], 'v_ple_w_proj': out['v_ple_w_proj'], 'v_ple_w_gate': out['v_ple_w_gate']}


def _loss(weights, diff, rest, loss_target):
    with _jax.named_scope("forward"):
        args = {**rest, TWIN_DIFF_INPUT: diff, **{k: w.astype(_WEIGHT_DTYPES[k]) for k, w in weights.items()}}
        y = _forward(args)
    with _jax.named_scope("loss_head"):
        err = _jnp.square(y.astype(_jnp.float32) - loss_target)
        return 0.5 * _jnp.sum(_jnp.mean(err, axis=-1)) if err.ndim else 0.5 * err


def _adamw(w, g, m, v):
    m = ADAM_B1 * m + (1.0 - ADAM_B1) * g
    v = ADAM_B2 * v + (1.0 - ADAM_B2) * _jnp.square(g)
    m_hat = m / (1.0 - ADAM_B1 ** ADAM_STEP)
    v_hat = v / (1.0 - ADAM_B2 ** ADAM_STEP)
    delta = -ADAM_LR * (m_hat / (_jnp.sqrt(v_hat) + ADAM_EPS) + ADAM_WD * w)
    return delta, m, v


def reference(x, p, a_norm_pre, a_norm_post, a_w_in, a_lam_re, a_lam_im, a_log_dt, a_b_re, a_b_im, a_c_re, a_c_im, a_d_skip, a_w_glu, a_b_glu, a_w_out, kv_norm, w_kv, b_norm_pre, b_norm_post, b_w_in, b_w_out, ple_w_proj, ple_w_gate, loss_target, m_a_norm_pre, m_a_norm_post, m_a_w_in, m_a_lam_re, m_a_lam_im, m_a_log_dt, m_a_b_re, m_a_b_im, m_a_c_re, m_a_c_im, m_a_d_skip, m_a_w_glu, m_a_b_glu, m_a_w_out, m_kv_norm, m_w_kv, m_b_norm_pre, m_b_norm_post, m_b_w_in, m_b_w_out, m_ple_w_proj, m_ple_w_gate, v_a_norm_pre, v_a_norm_post, v_a_w_in, v_a_lam_re, v_a_lam_im, v_a_log_dt, v_a_b_re, v_a_b_im, v_a_c_re, v_a_c_im, v_a_d_skip, v_a_w_glu, v_a_b_glu, v_a_w_out, v_kv_norm, v_w_kv, v_b_norm_pre, v_b_norm_post, v_b_w_in, v_b_w_out, v_ple_w_proj, v_ple_w_gate):
    given = dict(x=x, p=p, a_norm_pre=a_norm_pre, a_norm_post=a_norm_post, a_w_in=a_w_in, a_lam_re=a_lam_re, a_lam_im=a_lam_im, a_log_dt=a_log_dt, a_b_re=a_b_re, a_b_im=a_b_im, a_c_re=a_c_re, a_c_im=a_c_im, a_d_skip=a_d_skip, a_w_glu=a_w_glu, a_b_glu=a_b_glu, a_w_out=a_w_out, kv_norm=kv_norm, w_kv=w_kv, b_norm_pre=b_norm_pre, b_norm_post=b_norm_post, b_w_in=b_w_in, b_w_out=b_w_out, ple_w_proj=ple_w_proj, ple_w_gate=ple_w_gate, loss_target=loss_target, m_a_norm_pre=m_a_norm_pre, m_a_norm_post=m_a_norm_post, m_a_w_in=m_a_w_in, m_a_lam_re=m_a_lam_re, m_a_lam_im=m_a_lam_im, m_a_log_dt=m_a_log_dt, m_a_b_re=m_a_b_re, m_a_b_im=m_a_b_im, m_a_c_re=m_a_c_re, m_a_c_im=m_a_c_im, m_a_d_skip=m_a_d_skip, m_a_w_glu=m_a_w_glu, m_a_b_glu=m_a_b_glu, m_a_w_out=m_a_w_out, m_kv_norm=m_kv_norm, m_w_kv=m_w_kv, m_b_norm_pre=m_b_norm_pre, m_b_norm_post=m_b_norm_post, m_b_w_in=m_b_w_in, m_b_w_out=m_b_w_out, m_ple_w_proj=m_ple_w_proj, m_ple_w_gate=m_ple_w_gate, v_a_norm_pre=v_a_norm_pre, v_a_norm_post=v_a_norm_post, v_a_w_in=v_a_w_in, v_a_lam_re=v_a_lam_re, v_a_lam_im=v_a_lam_im, v_a_log_dt=v_a_log_dt, v_a_b_re=v_a_b_re, v_a_b_im=v_a_b_im, v_a_c_re=v_a_c_re, v_a_c_im=v_a_c_im, v_a_d_skip=v_a_d_skip, v_a_w_glu=v_a_w_glu, v_a_b_glu=v_a_b_glu, v_a_w_out=v_a_w_out, v_kv_norm=v_kv_norm, v_w_kv=v_w_kv, v_b_norm_pre=v_b_norm_pre, v_b_norm_post=v_b_norm_post, v_b_w_in=v_b_w_in, v_b_w_out=v_b_w_out, v_ple_w_proj=v_ple_w_proj, v_ple_w_gate=v_ple_w_gate)
    weights = {n: given[n] for n in TWIN_WEIGHTS}
    shared = {n: given[n] for n in SHARED_INPUTS}
    per_example = {n: given[n] for n in ['x', 'p']}
    grad_fn = _jax.value_and_grad(_loss, argnums=(0, 1))

    def one_microbatch(ex, loss_target):
        ex = dict(ex)
        diff = ex.pop(TWIN_DIFF_INPUT)
        return grad_fn(weights, diff, {**shared, **ex}, loss_target)

    if N_MICROBATCH == 1:
        loss, (grad_w, grad_x) = one_microbatch(per_example, given["loss_target"])
    else:
        def body(carry, xs):
            loss_sum, grad_sum = carry
            l_k, (gw_k, gx_k) = one_microbatch(xs[0], xs[1])
            with _jax.named_scope("update"):
                return (loss_sum + l_k, _jax.tree.map(_jnp.add, grad_sum, gw_k)), gx_k

        init = (_jnp.zeros((), _jnp.float32), _jax.tree.map(_jnp.zeros_like, weights))
        (loss, grad_w), grad_x = _jax.lax.scan(body, init, (per_example, given["loss_target"]))
    with _jax.named_scope("update"):
        delta_w, new_m, new_v = {}, {}, {}
        for n in TWIN_WEIGHTS:
            delta_w[n], new_m[n], new_v[n] = _adamw(weights[n], grad_w[n], given["m_" + n], given["v_" + n])
    return (loss, grad_x, *[grad_w[n] for n in TWIN_WEIGHTS], *[delta_w[n] for n in TWIN_WEIGHTS],
            *[new_m[n] for n in TWIN_WEIGHTS], *[new_v[n] for n in TWIN_WEIGHTS])
```

```python
import functools
import math

import jax
import jax.numpy as jnp
from jax import lax
from jax.experimental import pallas as pl
from jax.experimental.pallas import tpu as pltpu

F32 = jnp.float32
BF16 = jnp.bfloat16

N_DEV = 8
D_MODEL = 1024
GROUP_SIZE = 16
N_GROUPS = D_MODEL // GROUP_SIZE
STATE = 64
HEAD_DIM = 64
EPS = 1e-6
LANES = 128
GROUPS_PER_BLOCK = LANES // GROUP_SIZE
N_GBLK = N_GROUPS // GROUPS_PER_BLOCK
STATE_COLS = 2 * GROUPS_PER_BLOCK * STATE
N_SLAB = STATE_COLS // LANES
VMEM_LIMIT = 56 * 1024 * 1024
EXP_UNDERFLOW = -104.0

ADAM_LR = 0.001
ADAM_B1 = 0.9
ADAM_B2 = 0.999
ADAM_EPS = 1e-08
ADAM_WD = 0.01
ADAM_STEP = 10


def _pallas(body, **kw):
    return pl.pallas_call(body, **kw)


def _cparams(*sem):
    return pltpu.CompilerParams(dimension_semantics=sem, vmem_limit_bytes=VMEM_LIMIT)


def _mm(a, b, *, ta=False, tb=False, add=None, out_dtype=F32, name):
    m, k = (a.shape[1], a.shape[0]) if ta else a.shape
    n = b.shape[0] if tb else b.shape[1]
    tm = min(m, 512)
    tn = min(n, 1024)
    tk = min(k, 1024 if not ta else 512)
    nk = k // tk
    grid = (m // tm, n // tn, nk)
    a_spec = pl.BlockSpec((tk, tm), lambda i, j, l: (l, i)) if ta else pl.BlockSpec((tm, tk), lambda i, j, l: (i, l))
    b_spec = pl.BlockSpec((tn, tk), lambda i, j, l: (j, l)) if tb else pl.BlockSpec((tk, tn), lambda i, j, l: (l, j))
    o_spec = pl.BlockSpec((tm, tn), lambda i, j, l: (i, j))
    dims = (((0 if ta else 1,), (1 if tb else 0,)), ((), ()))
    has_add = add is not None

    def body(*refs):
        a_ref, b_ref = refs[0], refs[1]
        add_ref = refs[2] if has_add else None
        o_ref, acc_ref = refs[-2], refs[-1]
        l = pl.program_id(2)
        part = lax.dot_general(a_ref[...].astype(BF16), b_ref[...].astype(BF16), dims, preferred_element_type=F32)

        @pl.when(l == 0)
        def _():
            acc_ref[...] = part

        @pl.when(l > 0)
        def _():
            acc_ref[...] += part

        @pl.when(l == nk - 1)
        def _():
            r = acc_ref[...]
            if has_add:
                r = r + add_ref[...].astype(F32)
            o_ref[...] = r.astype(out_dtype)

    ins = [a, b] + ([add] if has_add else [])
    specs = [a_spec, b_spec] + ([o_spec] if has_add else [])
    return _pallas(
        body, name=name, grid=grid, in_specs=specs, out_specs=o_spec,
        out_shape=jax.ShapeDtypeStruct((m, n), out_dtype),
        scratch_shapes=[pltpu.VMEM((tm, tn), F32)],
        compiler_params=_cparams("parallel", "parallel", "arbitrary"),
    )(*ins)


def _row_spec(arr, tm):
    return pl.BlockSpec((tm, arr.shape[1]), lambda i: (i, 0))


def _full_spec(arr):
    return pl.BlockSpec(arr.shape, lambda i: (0,) * arr.ndim)


def _ew(fn, rows, vecs, out_dtypes, *, n_red=0, tm=256, name):
    t = rows[0].shape[0]
    nr, nv = len(rows), len(vecs)
    shapes = jax.eval_shape(fn, *[jax.ShapeDtypeStruct((tm, r.shape[1]), F32) for r in rows],
                            *[jax.ShapeDtypeStruct(v.shape, F32) for v in vecs])
    n_out = len(shapes) - n_red

    def body(*refs):
        ins = [r[...].astype(F32) for r in refs[:nr + nv]]
        outs = fn(*ins)
        o_refs = refs[nr + nv:]
        for o_ref, o in zip(o_refs[:n_out], outs[:n_out]):
            o_ref[...] = o.astype(o_ref.dtype)
        i = pl.program_id(0)
        for o_ref, o in zip(o_refs[n_out:], outs[n_out:]):
            @pl.when(i == 0)
            def _(o_ref=o_ref, o=o):
                o_ref[...] = o

            @pl.when(i > 0)
            def _(o_ref=o_ref, o=o):
                o_ref[...] += o

    out_shape = [jax.ShapeDtypeStruct((t, s.shape[1]), dt) for s, dt in zip(shapes[:n_out], out_dtypes)]
    out_shape += [jax.ShapeDtypeStruct(s.shape, F32) for s in shapes[n_out:]]
    out_specs = [pl.BlockSpec((tm, s.shape[1]), lambda i: (i, 0)) for s in shapes[:n_out]]
    out_specs += [pl.BlockSpec(s.shape, lambda i: (0, 0)) for s in shapes[n_out:]]
    return _pallas(
        body, name=name, grid=(t // tm,),
        in_specs=[_row_spec(r, tm) for r in rows] + [_full_spec(v) for v in vecs],
        out_specs=out_specs, out_shape=out_shape,
        compiler_params=_cparams("arbitrary" if n_red else "parallel"),
    )(*rows, *vecs)


def _ew_bwd(fn, rows, vecs, cts, row_wrt, vec_wrt, out_dtypes, *, tm=256, name):
    nr, nv, nc = len(rows), len(vecs), len(cts)

    def grad_fn(*tiles):
        ins = list(tiles[:nr + nv])
        ct = tiles[nr + nv:]
        wrt = list(row_wrt) + [nr + j for j in vec_wrt]

        def f(*sel):
            full = list(ins)
            for idx, s in zip(wrt, sel):
                full[idx] = s
            return tuple(fn(*full))

        _, vjp = jax.vjp(f, *[ins[idx] for idx in wrt])
        return vjp(tuple(ct))

    return _ew_reordered(grad_fn, rows, vecs, cts, out_dtypes, len(vec_wrt), tm, name)


def _ew_reordered(grad_fn, rows, vecs, cts, out_dtypes, n_red, tm, name):
    nr, nv = len(rows), len(vecs)

    def fn(*tiles):
        r, c, v = tiles[:nr], tiles[nr:nr + len(cts)], tiles[nr + len(cts):]
        return grad_fn(*r, *v, *c)

    return _ew(fn, list(rows) + list(cts), vecs, out_dtypes, n_red=n_red, tm=tm, name=name)


def _rms(x, g):
    return x * lax.rsqrt(jnp.mean(x * x, axis=-1, keepdims=True) + EPS) * g


def _sigmoid(x):
    return 1.0 / (1.0 + jnp.exp(-x))


def _silu(x):
    return x * _sigmoid(x)


def _gelu(x):
    return 0.5 * x * (1.0 + jnp.tanh(math.sqrt(2.0 / math.pi) * (x + 0.044715 * x * x * x)))


def _s5_expand(u_bf, bbd_ref, xs_ref, tt):
    for gb in range(N_GBLK):
        xg = jnp.dot(u_bf[:, gb * LANES:(gb + 1) * LANES], bbd_ref[gb], preferred_element_type=F32)
        for l in range(N_SLAB):
            xs_ref[l, gb * tt:(gb + 1) * tt, :] = xg[:, l * LANES:(l + 1) * LANES]


def _s5_forward_scan(xs_ref, ar, ai, state, tt):
    half = N_SLAB // 2

    def step(t, st):
        new = [None] * N_SLAB
        for l in range(half):
            sr, si = st[l], st[half + l]
            rows = pl.ds(t, N_GBLK, stride=tt)
            nr_ = ar[l] * sr - ai[l] * si + xs_ref[l, rows, :]
            ni_ = ar[l] * si + ai[l] * sr + xs_ref[half + l, rows, :]
            xs_ref[l, rows, :] = nr_
            xs_ref[half + l, rows, :] = ni_
            new[l], new[half + l] = nr_, ni_
        return tuple(new)

    return lax.fori_loop(0, tt, step, tuple(state), unroll=4)


def _slab_rows(ref, gb, tt):
    return jnp.concatenate([ref[l, gb * tt:(gb + 1) * tt, :] for l in range(N_SLAB)], axis=1)


def _s5_fwd(u, bbd, ccd, a_re, a_im, *, tt=256):
    t = u.shape[0]
    nt = t // tt
    half = N_SLAB // 2

    def body(u_ref, bbd_ref, ccd_ref, ar_ref, ai_ref, y_ref, hb_ref, xs_ref, st_ref):
        i = pl.program_id(0)

        @pl.when(i == 0)
        def _():
            st_ref[...] = jnp.zeros_like(st_ref)

        hb_ref[0] = st_ref[...]
        _s5_expand(u_ref[...].astype(BF16), bbd_ref, xs_ref, tt)
        ar = [ar_ref[l] for l in range(half)]
        ai = [ai_ref[l] for l in range(half)]
        last = _s5_forward_scan(xs_ref, ar, ai, [st_ref[l] for l in range(N_SLAB)], tt)
        for l in range(N_SLAB):
            st_ref[l] = last[l]
        for gb in range(N_GBLK):
            sg = _slab_rows(xs_ref, gb, tt).astype(BF16)
            y_ref[:, gb * LANES:(gb + 1) * LANES] = jnp.dot(sg, ccd_ref[gb], preferred_element_type=F32)

    return _pallas(
        body, name="s5_fwd", grid=(nt,),
        in_specs=[pl.BlockSpec((tt, D_MODEL), lambda i: (i, 0)), _full_spec(bbd), _full_spec(ccd),
                  _full_spec(a_re), _full_spec(a_im)],
        out_specs=[pl.BlockSpec((tt, D_MODEL), lambda i: (i, 0)),
                   pl.BlockSpec((1, N_SLAB, N_GBLK, LANES), lambda i: (i, 0, 0, 0))],
        out_shape=[jax.ShapeDtypeStruct((t, D_MODEL), F32), jax.ShapeDtypeStruct((nt, N_SLAB, N_GBLK, LANES), F32)],
        scratch_shapes=[pltpu.VMEM((N_SLAB, N_GBLK * tt, LANES), F32), pltpu.VMEM((N_SLAB, N_GBLK, LANES), F32)],
        compiler_params=_cparams("arbitrary"),
    )(u, bbd, ccd, a_re, a_im)


def _s5_bwd(u, dy, du_add, hb, bbd, ccd, a_re, a_im, *, tt=256):
    t = u.shape[0]
    nt = t // tt
    half = N_SLAB // 2
    rev = lambda i: (nt - 1 - i, 0)

    def body(u_ref, dy_ref, dua_ref, hb_ref, bbd_ref, ccd_ref, ar_ref, ai_ref,
             du_ref, dbbd_ref, dccd_ref, dar_ref, dai_ref, xs_ref, es_ref, lam_ref):
        i = pl.program_id(0)

        @pl.when(i == 0)
        def _():
            lam_ref[...] = jnp.zeros_like(lam_ref)
            dbbd_ref[...] = jnp.zeros_like(dbbd_ref)
            dccd_ref[...] = jnp.zeros_like(dccd_ref)
            dar_ref[...] = jnp.zeros_like(dar_ref)
            dai_ref[...] = jnp.zeros_like(dai_ref)

        u_bf = u_ref[...].astype(BF16)
        dy_bf = dy_ref[...].astype(BF16)
        _s5_expand(u_bf, bbd_ref, xs_ref, tt)
        ar = [ar_ref[l] for l in range(half)]
        ai = [ai_ref[l] for l in range(half)]
        start = [hb_ref[0, l] for l in range(N_SLAB)]
        _s5_forward_scan(xs_ref, ar, ai, start, tt)
        for gb in range(N_GBLK):
            eg = lax.dot_general(dy_bf[:, gb * LANES:(gb + 1) * LANES], ccd_ref[gb], (((1,), (1,)), ((), ())),
                                 preferred_element_type=F32)
            for l in range(N_SLAB):
                es_ref[l, gb * tt:(gb + 1) * tt, :] = eg[:, l * LANES:(l + 1) * LANES]

        def lam_step(t_, lam, prev, acc):
            new_lam, new_acc = [None] * N_SLAB, [None] * N_SLAB
            rows = pl.ds(t_, N_GBLK, stride=tt)
            for l in range(half):
                lr = es_ref[l, rows, :] + ar[l] * lam[l] + ai[l] * lam[half + l]
                li = es_ref[half + l, rows, :] - ai[l] * lam[l] + ar[l] * lam[half + l]
                es_ref[l, rows, :] = lr
                es_ref[half + l, rows, :] = li
                pr, pi = prev(l), prev(half + l)
                new_acc[l] = acc[l] + lr * pr + li * pi
                new_acc[half + l] = acc[half + l] - lr * pi + li * pr
                new_lam[l], new_lam[half + l] = lr, li
            return tuple(new_lam), tuple(new_acc)

        def loop_body(k, carry):
            lam, acc = carry
            t_ = tt - 1 - k
            return lam_step(t_, lam, lambda l: xs_ref[l, pl.ds(t_ - 1, N_GBLK, stride=tt), :], acc)

        zero = tuple(jnp.zeros((N_GBLK, LANES), F32) for _ in range(N_SLAB))
        lam0 = tuple(lam_ref[l] for l in range(N_SLAB))
        lam, acc = lax.fori_loop(0, tt - 1, loop_body, (lam0, zero), unroll=2)
        lam, acc = lam_step(0, lam, lambda l: start[l], acc)
        for l in range(N_SLAB):
            lam_ref[l] = lam[l]
        for l in range(half):
            dar_ref[l] += acc[l]
            dai_ref[l] += acc[half + l]

        for gb in range(N_GBLK):
            cols = slice(gb * LANES, (gb + 1) * LANES)
            lg = _slab_rows(es_ref, gb, tt).astype(BF16)
            sg = _slab_rows(xs_ref, gb, tt).astype(BF16)
            du_ref[:, cols] = dua_ref[:, cols] + lax.dot_general(
                lg, bbd_ref[gb], (((1,), (1,)), ((), ())), preferred_element_type=F32)
            dbbd_ref[gb] += lax.dot_general(u_bf[:, cols], lg, (((0,), (0,)), ((), ())), preferred_element_type=F32)
            dccd_ref[gb] += lax.dot_general(sg, dy_bf[:, cols], (((0,), (0,)), ((), ())), preferred_element_type=F32)

    row = pl.BlockSpec((tt, D_MODEL), rev)
    small = pl.BlockSpec((half, N_GBLK, LANES), lambda i: (0, 0, 0))
    return _pallas(
        body, name="s5_bwd", grid=(nt,),
        in_specs=[row, row, row, pl.BlockSpec((1, N_SLAB, N_GBLK, LANES), lambda i: (nt - 1 - i, 0, 0, 0)),
                  _full_spec(bbd), _full_spec(ccd), _full_spec(a_re), _full_spec(a_im)],
        out_specs=[row, _full_spec(bbd), _full_spec(ccd), small, small],
        out_shape=[jax.ShapeDtypeStruct((t, D_MODEL), F32), jax.ShapeDtypeStruct(bbd.shape, F32),
                   jax.ShapeDtypeStruct(ccd.shape, F32), jax.ShapeDtypeStruct(a_re.shape, F32),
                   jax.ShapeDtypeStruct(a_im.shape, F32)],
        scratch_shapes=[pltpu.VMEM((N_SLAB, N_GBLK * tt, LANES), F32), pltpu.VMEM((N_SLAB, N_GBLK * tt, LANES), F32),
                        pltpu.VMEM((N_SLAB, N_GBLK, LANES), F32)],
        compiler_params=_cparams("arbitrary"),
    )(u, dy, du_add, hb, bbd, ccd, a_re, a_im)


def _s5_discretize(lam_re, lam_im, log_dt, b_re, b_im, c_re, c_im):
    lr = jnp.minimum(lam_re, -1e-4)
    li = lam_im
    dt = jnp.exp(log_dt)[:, None]
    mag = jnp.exp(lr * dt)
    a_re = mag * jnp.cos(li * dt)
    a_im = mag * jnp.sin(li * dt)
    den = lr * lr + li * li
    nr = a_re - 1.0
    f_re = (nr * lr + a_im * li) / den
    f_im = (a_im * lr - nr * li) / den
    bb_re = f_re[..., None] * b_re - f_im[..., None] * b_im
    bb_im = f_re[..., None] * b_im + f_im[..., None] * b_re
    eye = jnp.eye(GROUPS_PER_BLOCK, dtype=F32)
    bb = jnp.stack([bb_re, bb_im]).reshape(2, N_GBLK, GROUPS_PER_BLOCK, STATE, GROUP_SIZE)
    bbd = jnp.einsum('rgjph,jk->gjhrkp', bb, eye).reshape(N_GBLK, LANES, STATE_COLS)
    cc = jnp.stack([c_re, -c_im]).reshape(2, N_GBLK, GROUPS_PER_BLOCK, GROUP_SIZE, STATE)
    ccd = jnp.einsum('rgjhp,jk->grjpkh', cc, eye).reshape(N_GBLK, STATE_COLS, LANES)

    def dense(a):
        return a.reshape(N_GBLK, N_SLAB // 2, LANES).transpose(1, 0, 2)

    return dense(a_re), dense(a_im), bbd, ccd


def _attn_tiles(tq):
    row = lax.broadcasted_iota(jnp.int32, (tq, tq), 0)
    col = lax.broadcasted_iota(jnp.int32, (tq, tq), 1)
    return col < row, (row > col).astype(BF16), (row >= col).astype(BF16)


def _attn_scores(qa, kj, c, diag, later, masked):
    z = lax.dot_general(qa, kj, (((1,), (1,)), ((), ())), preferred_element_type=F32) * (HEAD_DIM ** -0.5)
    lk = -(jnp.maximum(z, 0.0) + jnp.log(1.0 + jnp.exp(-jnp.abs(z))))
    if masked:
        lk = jnp.where(diag, lk, 0.0)
    hi = lk.astype(BF16)
    lo = (lk - hi.astype(F32)).astype(BF16)
    tail = jnp.dot(hi, later, preferred_element_type=F32) + jnp.dot(lo, later, preferred_element_type=F32)
    w = jnp.exp(z + lk + tail + c)
    if masked:
        w = jnp.where(diag, w, 0.0)
    return z, lk, w


def _attn_fwd(q, k, v, *, tq=128):
    t = q.shape[0]
    n_pair = D_MODEL // LANES

    def body(q_ref, k_ref, v_ref, o_ref, ol_ref):
        i = pl.program_id(1)
        diag, later, _ = _attn_tiles(tq)
        lane = lax.broadcasted_iota(jnp.int32, (1, LANES), 1)
        q2 = q_ref[...]
        out = jnp.zeros((tq, LANES), F32)
        out_low = jnp.zeros((tq, LANES), F32)
        for a in range(2):
            hm = (lane >= a * HEAD_DIM) & (lane < (a + 1) * HEAD_DIM)
            qa = jnp.where(hm, q2, jnp.zeros_like(q2))

            def block(j, c, acc, low, masked):
                rows = pl.ds(pl.multiple_of(j * tq, tq), tq)
                _, lk, w = _attn_scores(qa, k_ref[rows, :], c, diag, later, masked)
                w_hi = w.astype(BF16)
                w_lo = (w - w_hi.astype(F32)).astype(BF16)
                vj = v_ref[rows, :]
                acc = acc + jnp.dot(w_hi, vj, preferred_element_type=F32)
                low = low + jnp.dot(w_lo, vj, preferred_element_type=F32)
                return c + jnp.sum(lk, axis=1, keepdims=True), acc, low

            zero = jnp.zeros((tq, LANES), F32)
            c, acc, low = block(i, jnp.zeros((tq, 1), F32), zero, zero, True)

            def cond(st):
                return (st[0] >= 0) & (jnp.max(st[1]) > EXP_UNDERFLOW)

            def step(st):
                c_, acc_, low_ = block(st[0], st[1], st[2], st[3], False)
                return st[0] - 1, c_, acc_, low_

            _, c, acc, low = lax.while_loop(cond, step, (i - 1, c, acc, low))
            out = out + jnp.where(hm, acc, 0.0)
            out_low = out_low + jnp.where(hm, low, 0.0)
        o_ref[...] = out
        ol_ref[...] = out_low

    qspec = pl.BlockSpec((tq, LANES), lambda h, i: (i, h))
    kspec = pl.BlockSpec((t, LANES), lambda h, i: (0, h))
    full = jax.ShapeDtypeStruct((t, D_MODEL), F32)
    return _pallas(
        body, name="attn_fwd", grid=(n_pair, t // tq), in_specs=[qspec, kspec, kspec], out_specs=[qspec, qspec],
        out_shape=[full, full], compiler_params=_cparams("parallel", "arbitrary"),
    )(q, k, v)


def _attn_bwd(q, k, v, o, o_low, do, *, tq=128):
    t = q.shape[0]
    n_pair = D_MODEL // LANES

    def body(q_ref, k_ref, v_ref, o_ref, ol_ref, do_ref, dq_ref, dk_ref, dv_ref):
        i = pl.program_id(1)

        @pl.when(i == 0)
        def _():
            dk_ref[...] = jnp.zeros_like(dk_ref)
            dv_ref[...] = jnp.zeros_like(dv_ref)

        diag, later, later_eq = _attn_tiles(tq)
        lane = lax.broadcasted_iota(jnp.int32, (1, LANES), 1)
        q2 = q_ref[...]
        do2 = do_ref[...]
        dd = do2.astype(BF16).astype(F32) * (o_ref[...] + ol_ref[...])
        dq = jnp.zeros((tq, LANES), F32)
        for a in range(2):
            hm = (lane >= a * HEAD_DIM) & (lane < (a + 1) * HEAD_DIM)
            qa = jnp.where(hm, q2, jnp.zeros_like(q2))
            doa = jnp.where(hm, do2, 0.0).astype(BF16)
            total = jnp.sum(jnp.where(hm, dd, 0.0), axis=1, keepdims=True)

            def block(j, c, r, acc, masked):
                rows = pl.ds(pl.multiple_of(j * tq, tq), tq)
                kj, vj = k_ref[rows, :], v_ref[rows, :]
                z, lk, w = _attn_scores(qa, kj, c, diag, later, masked)
                dw = lax.dot_general(doa, vj, (((1,), (1,)), ((), ())), preferred_element_type=F32)
                g = dw * w
                g_hi = g.astype(BF16)
                g_lo = (g - g_hi.astype(F32)).astype(BF16)
                from_here = (jnp.dot(g_hi, later_eq, preferred_element_type=F32)
                             + jnp.dot(g_lo, later_eq, preferred_element_type=F32))
                before = total - r - from_here
                beta = jnp.exp(z + lk)
                dz = g * (1.0 - beta) - beta * before
                if masked:
                    dz = jnp.where(diag, dz, 0.0)
                dz_bf = (dz * (HEAD_DIM ** -0.5)).astype(BF16)
                acc = acc + jnp.dot(dz_bf, kj, preferred_element_type=F32)
                dk_ref[rows, :] += lax.dot_general(dz_bf, qa, (((0,), (0,)), ((), ())), preferred_element_type=F32)
                dv_ref[rows, :] += lax.dot_general(w.astype(BF16), doa, (((0,), (0,)), ((), ())),
                                                   preferred_element_type=F32)
                return c + jnp.sum(lk, axis=1, keepdims=True), r + jnp.sum(g, axis=1, keepdims=True), acc

            zc = jnp.zeros((tq, 1), F32)
            c, r, acc = block(i, zc, zc, jnp.zeros((tq, LANES), F32), True)

            def cond(st):
                return (st[0] >= 0) & (jnp.max(st[1]) > EXP_UNDERFLOW)

            def step(st):
                c_, r_, acc_ = block(st[0], st[1], st[2], st[3], False)
                return st[0] - 1, c_, r_, acc_

            _, c, r, acc = lax.while_loop(cond, step, (i - 1, c, r, acc))
            dq = dq + jnp.where(hm, acc, 0.0)
        dq_ref[...] = dq

    qspec = pl.BlockSpec((tq, LANES), lambda h, i: (i, h))
    kspec = pl.BlockSpec((t, LANES), lambda h, i: (0, h))
    full = jax.ShapeDtypeStruct((t, D_MODEL), F32)
    return _pallas(
        body, name="attn_bwd", grid=(n_pair, t // tq), in_specs=[qspec, kspec, kspec, qspec, qspec, qspec],
        out_specs=[qspec, kspec, kspec], out_shape=[full, full, full],
        compiler_params=_cparams("parallel", "arbitrary"),
    )(q, k, v, o, o_low, do)


def _exchange(arrs, *, scatter, name):
    n = len(arrs)
    out_shape = [jax.ShapeDtypeStruct(a.shape if scatter else (N_DEV,) + a.shape, a.dtype) for a in arrs]

    def body(*refs):
        ins, outs = refs[:n], refs[n:2 * n]
        send_sems, recv_sems, local_sems = refs[2 * n:]
        x, y, c = lax.axis_index("x"), lax.axis_index("y"), lax.axis_index("c")
        me = 4 * x + 2 * y + c

        def remote(a, d):
            px, py, pc = x ^ ((d >> 2) & 1), y ^ ((d >> 1) & 1), c ^ (d & 1)
            peer = 4 * px + 2 * py + pc
            src = ins[a].at[peer] if scatter else ins[a]
            send = pltpu.make_async_remote_copy(
                src_ref=src, dst_ref=outs[a].at[me], send_sem=send_sems.at[a, d - 1], recv_sem=recv_sems.at[a, d - 1],
                device_id=(px, py, pc), device_id_type=pl.DeviceIdType.MESH)
            recv = pltpu.make_async_remote_copy(
                src_ref=src, dst_ref=outs[a].at[peer], send_sem=send_sems.at[a, d - 1], recv_sem=recv_sems.at[a, d - 1],
                device_id=(px, py, pc), device_id_type=pl.DeviceIdType.MESH)
            return send, recv

        local = [pltpu.make_async_copy(ins[a].at[me] if scatter else ins[a], outs[a].at[me], local_sems.at[a])
                 for a in range(n)]
        for a in range(n):
            local[a].start()
            for d in range(1, N_DEV):
                remote(a, d)[0].start()
        for a in range(n):
            for d in range(1, N_DEV):
                send, recv = remote(a, d)
                send.wait_send()
                recv.wait_recv()
            local[a].wait()

    any_spec = pl.BlockSpec(memory_space=pl.ANY)
    return _pallas(
        body, name=name, in_specs=[any_spec] * n, out_specs=[any_spec] * n, out_shape=out_shape,
        scratch_shapes=[pltpu.SemaphoreType.DMA((n, N_DEV - 1)), pltpu.SemaphoreType.DMA((n, N_DEV - 1)),
                        pltpu.SemaphoreType.DMA((n,))],
    )(*arrs)


def _adamw(parts, w, m, v, *, name):
    s, r, c = parts.shape
    tr = r if r <= 256 else 256

    def body(p_ref, w_ref, m_ref, v_ref, g_ref, d_ref, nm_ref, nv_ref):
        g = p_ref[0]
        for j in range(1, s):
            g = g + p_ref[j]
        nm = ADAM_B1 * m_ref[...] + (1.0 - ADAM_B1) * g
        nv = ADAM_B2 * v_ref[...] + (1.0 - ADAM_B2) * (g * g)
        m_hat = nm / (1.0 - ADAM_B1 ** ADAM_STEP)
        v_hat = nv / (1.0 - ADAM_B2 ** ADAM_STEP)
        g_ref[...] = g
        d_ref[...] = -ADAM_LR * (m_hat / (jnp.sqrt(v_hat) + ADAM_EPS) + ADAM_WD * w_ref[...])
        nm_ref[...] = nm
        nv_ref[...] = nv

    spec = pl.BlockSpec((tr, c), lambda i: (i, 0))
    one = jax.ShapeDtypeStruct((r, c), F32)
    return _pallas(
        body, name=name, grid=(r // tr,), in_specs=[pl.BlockSpec((s, tr, c), lambda i: (0, i, 0)), spec, spec, spec],
        out_specs=[spec] * 4, out_shape=[one] * 4, compiler_params=_cparams("parallel"),
    )(parts, w, m, v)


def _cols_to_blocks(g, n_blk):
    r = g.shape[0]
    return g.reshape(r, n_blk, g.shape[1] // n_blk).transpose(1, 0, 2)


def _blocks_to_cols(gathered):
    n_blk, r, c = gathered.shape
    return gathered.transpose(1, 0, 2).reshape(r, n_blk * c)


def kernel(x, p, a_norm_pre, a_norm_post, a_w_in, a_lam_re, a_lam_im, a_log_dt, a_b_re, a_b_im, a_c_re, a_c_im, a_d_skip, a_w_glu, a_b_glu, a_w_out, kv_norm, w_kv, b_norm_pre, b_norm_post, b_w_in, b_w_out, ple_w_proj, ple_w_gate, loss_target, m_a_norm_pre, m_a_norm_post, m_a_w_in, m_a_lam_re, m_a_lam_im, m_a_log_dt, m_a_b_re, m_a_b_im, m_a_c_re, m_a_c_im, m_a_d_skip, m_a_w_glu, m_a_b_glu, m_a_w_out, m_kv_norm, m_w_kv, m_b_norm_pre, m_b_norm_post, m_b_w_in, m_b_w_out, m_ple_w_proj, m_ple_w_gate, v_a_norm_pre, v_a_norm_post, v_a_w_in, v_a_lam_re, v_a_lam_im, v_a_log_dt, v_a_b_re, v_a_b_im, v_a_c_re, v_a_c_im, v_a_d_skip, v_a_w_glu, v_a_b_glu, v_a_w_out, v_kv_norm, v_w_kv, v_b_norm_pre, v_b_norm_post, v_b_w_in, v_b_w_out, v_ple_w_proj, v_ple_w_gate):
    weights = dict(a_norm_pre=a_norm_pre, a_norm_post=a_norm_post, a_w_in=a_w_in, a_lam_re=a_lam_re, a_lam_im=a_lam_im, a_log_dt=a_log_dt, a_b_re=a_b_re, a_b_im=a_b_im, a_c_re=a_c_re, a_c_im=a_c_im, a_d_skip=a_d_skip, a_w_glu=a_w_glu, a_b_glu=a_b_glu, a_w_out=a_w_out, kv_norm=kv_norm, w_kv=w_kv, b_norm_pre=b_norm_pre, b_norm_post=b_norm_post, b_w_in=b_w_in, b_w_out=b_w_out, ple_w_proj=ple_w_proj, ple_w_gate=ple_w_gate)
    mom_m = dict(a_norm_pre=m_a_norm_pre, a_norm_post=m_a_norm_post, a_w_in=m_a_w_in, a_lam_re=m_a_lam_re, a_lam_im=m_a_lam_im, a_log_dt=m_a_log_dt, a_b_re=m_a_b_re, a_b_im=m_a_b_im, a_c_re=m_a_c_re, a_c_im=m_a_c_im, a_d_skip=m_a_d_skip, a_w_glu=m_a_w_glu, a_b_glu=m_a_b_glu, a_w_out=m_a_w_out, kv_norm=m_kv_norm, w_kv=m_w_kv, b_norm_pre=m_b_norm_pre, b_norm_post=m_b_norm_post, b_w_in=m_b_w_in, b_w_out=m_b_w_out, ple_w_proj=m_ple_w_proj, ple_w_gate=m_ple_w_gate)
    mom_v = dict(a_norm_pre=v_a_norm_pre, a_norm_post=v_a_norm_post, a_w_in=v_a_w_in, a_lam_re=v_a_lam_re, a_lam_im=v_a_lam_im, a_log_dt=v_a_log_dt, a_b_re=v_a_b_re, a_b_im=v_a_b_im, a_c_re=v_a_c_re, a_c_im=v_a_c_im, a_d_skip=v_a_d_skip, a_w_glu=v_a_w_glu, a_b_glu=v_a_b_glu, a_w_out=v_a_w_out, kv_norm=v_kv_norm, w_kv=v_w_kv, b_norm_pre=v_b_norm_pre, b_norm_post=v_b_norm_post, b_w_in=v_b_w_in, b_w_out=v_b_w_out, ple_w_proj=v_ple_w_proj, ple_w_gate=v_ple_w_gate)
    names = list(weights)

    t = x.shape[1]
    x0 = x[0]
    p0, p1 = p[0, 0], p[1, 0]
    tgt = loss_target[0]

    mats = dict(a_w_in=a_w_in[0], a_w_glu=a_w_glu[0], a_w_out=a_w_out[0], w_kv=w_kv, b_w_in=b_w_in[0],
                b_w_out=b_w_out[0], ple_w_proj=ple_w_proj.reshape(2 * 256, LANES),
                ple_w_gate=ple_w_gate.reshape(2 * LANES, D_MODEL))
    vec_names = ["a_norm_pre", "a_norm_post", "a_d_skip", "a_b_glu"]
    vec_pack = jnp.concatenate([weights[n] for n in vec_names], axis=0)
    gathered = _exchange([m_.astype(BF16) for m_ in mats.values()] + [vec_pack], scatter=False, name="gather_weights")
    gw = dict(zip(mats, gathered[:-1]))
    vec_full = gathered[-1].transpose(1, 0, 2).reshape(len(vec_names), D_MODEL)
    g_apre, g_apost, d_skip, b_glu = (vec_full[i:i + 1] for i in range(4))
    w_u, w_ga = _blocks_to_cols(gw["a_w_in"][:4]), _blocks_to_cols(gw["a_w_in"][4:])
    w_glu = gw["a_w_glu"].reshape(D_MODEL, D_MODEL)
    w_out = gw["a_w_out"].reshape(D_MODEL, D_MODEL)
    w_k, w_v = _blocks_to_cols(gw["w_kv"][:4]), _blocks_to_cols(gw["w_kv"][4:])
    w_q, w_gb = _blocks_to_cols(gw["b_w_in"][:4]), _blocks_to_cols(gw["b_w_in"][4:])
    w_bout = gw["b_w_out"].reshape(D_MODEL, D_MODEL)
    wp = gw["ple_w_proj"].reshape(N_DEV, 2, 256, LANES)
    w_p0, w_p1 = _blocks_to_cols(wp[:, 0]), _blocks_to_cols(wp[:, 1])
    wg = gw["ple_w_gate"].reshape(N_DEV, 2, LANES, D_MODEL)
    w_g0, w_g1 = wg[:, 0].reshape(D_MODEL, D_MODEL), wg[:, 1].reshape(D_MODEL, D_MODEL)
    g_kv, g_bpre, g_bpost = kv_norm.reshape(1, D_MODEL), b_norm_pre, b_norm_post

    s5_params = (a_lam_re[0], a_lam_im[0], a_log_dt[0], a_b_re[0], a_b_im[0], a_c_re[0], a_c_im[0])
    (a_re, a_im, bbd, ccd), s5_vjp = jax.vjp(_s5_discretize, *s5_params)
    bbd_bf, ccd_bf = bbd.astype(BF16), ccd.astype(BF16)

    f_a2 = lambda ys, u, d: (_gelu(ys + d * u),)
    f_a3 = lambda g, s, ga, b: (g * _sigmoid(s + b) * _silu(ga),)
    f_ple = lambda xin, pg, pp: (xin + _sigmoid(pg) * pp,)
    f_res_rms = lambda xin, y, g: (xin + _rms(y, g),)
    f_rms = lambda y, g: (_rms(y, g),)
    f_k = lambda x2, gk, gp: (_rms(x2, gk), _rms(x2, gp))
    f_b3 = lambda o, gb: (o * _silu(gb),)

    (h0,) = _ew(f_rms, [x0], [g_apre], [BF16], name="a_norm_pre")
    u = _mm(h0, w_u, name="a_proj_u")
    ga = _mm(h0, w_ga, name="a_proj_gate")
    ys, hb = _s5_fwd(u, bbd_bf, ccd_bf, a_re, a_im)
    (g_act,) = _ew(f_a2, [ys, u], [d_skip], [BF16], name="a_gelu")
    s_glu = _mm(g_act, w_glu, name="a_glu")
    (y3,) = _ew(lambda ys_, u_, s_, ga_, d_, b_: f_a3(_gelu(ys_ + d_ * u_), s_, ga_, b_),
                [ys, u, s_glu, ga], [d_skip, b_glu], [BF16], name="a_gate")
    y4 = _mm(y3, w_out, name="a_out")
    (x1,) = _ew(f_res_rms, [x0, y4], [g_apost], [F32], name="a_norm_post")
    pg0 = _mm(x1, w_g0, name="ple0_gate")
    pp0 = _mm(p0, w_p0, name="ple0_proj")
    (x2,) = _ew(f_ple, [x1, pg0, pp0], [], [F32], name="ple0")
    kvn, h1 = _ew(f_k, [x2], [g_kv, g_bpre], [BF16, BF16], name="b_norms")
    k_ = _mm(kvn, w_k, out_dtype=BF16, name="kv_k")
    v_ = _mm(kvn, w_v, out_dtype=BF16, name="kv_v")
    q_ = _mm(h1, w_q, out_dtype=BF16, name="b_proj_q")
    gb = _mm(h1, w_gb, name="b_proj_gate")
    o, o_low = _attn_fwd(q_, k_, v_)
    (y5in,) = _ew(f_b3, [o, gb], [], [BF16], name="b_gate")
    y5 = _mm(y5in, w_bout, name="b_out")
    (x3,) = _ew(f_res_rms, [x2, y5], [g_bpost], [F32], name="b_norm_post")
    pg1 = _mm(x3, w_g1, name="ple1_gate")
    pp1 = _mm(p1, w_p1, name="ple1_proj")

    def f_loss(xin, pg, pp, tg):
        err = xin + _sigmoid(pg) * pp - tg
        return err * (1.0 / D_MODEL), (0.5 / D_MODEL) * jnp.sum(err * err).reshape(1, 1)

    dx4, loss_part = _ew(f_loss, [x3, pg1, pp1, tgt], [], [F32], n_red=1, name="ple1_loss")
    loss = lax.psum(loss_part[0, 0], ("x", "y", "c"))

    grads = {}
    f_ple_gate = lambda pg, pp: (_sigmoid(pg) * pp,)
    dpg1, dpp1 = _ew_bwd(f_ple_gate, [pg1, pp1], [], [dx4], [0, 1], [], [BF16, BF16], name="ple1_bwd")
    dx3 = _mm(dpg1, w_g1, tb=True, add=dx4, name="ple1_gate_dx")
    d_wg1 = _mm(x3, dpg1, ta=True, name="ple1_gate_dw")
    d_wp1 = _mm(p1, dpp1, ta=True, name="ple1_proj_dw")
    dy5, d_gbpost = _ew_bwd(f_rms, [y5], [g_bpost], [dx3], [0], [0], [BF16], name="b_norm_post_bwd")
    dy5in = _mm(dy5, w_bout, tb=True, name="b_out_dx")
    grads["b_w_out"] = _mm(y5in, dy5, ta=True, name="b_out_dw").reshape(N_DEV, LANES, D_MODEL)
    do, dgb = _ew_bwd(f_b3, [o, gb], [], [dy5in], [0, 1], [], [F32, F32], name="b_gate_bwd")
    dq, dk, dv = _attn_bwd(q_, k_, v_, o, o_low, do)
    dh1 = _mm(dgb, w_gb, tb=True, add=_mm(dq, w_q, tb=True, name="b_proj_q_dx"), name="b_proj_gate_dx")
    grads["b_w_in"] = jnp.concatenate([_cols_to_blocks(_mm(h1, dq, ta=True, name="b_proj_q_dw"), 4),
                                       _cols_to_blocks(_mm(h1, dgb, ta=True, name="b_proj_gate_dw"), 4)])
    dkvn = _mm(dv, w_v, tb=True, add=_mm(dk, w_k, tb=True, name="kv_k_dx"), name="kv_v_dx")
    grads["w_kv"] = jnp.concatenate([_cols_to_blocks(_mm(kvn, dk, ta=True, name="kv_k_dw"), 4),
                                     _cols_to_blocks(_mm(kvn, dv, ta=True, name="kv_v_dw"), 4)])
    f_k3 = lambda x2_, gk, gp: f_k(x2_, gk, gp) + (x2_,)
    dx2, d_gkv, d_gbpre = _ew_bwd(f_k3, [x2], [g_kv, g_bpre], [dkvn, dh1, dx3], [0], [0, 1], [F32], name="b_norms_bwd")
    dpg0, dpp0 = _ew_bwd(f_ple_gate, [pg0, pp0], [], [dx2], [0, 1], [], [BF16, BF16], name="ple0_bwd")
    dx1 = _mm(dpg0, w_g0, tb=True, add=dx2, name="ple0_gate_dx")
    d_wg0 = _mm(x1, dpg0, ta=True, name="ple0_gate_dw")
    d_wp0 = _mm(p0, dpp0, ta=True, name="ple0_proj_dw")
    grads["ple_w_gate"] = jnp.stack([d_wg0.reshape(N_DEV, LANES, D_MODEL), d_wg1.reshape(N_DEV, LANES, D_MODEL)],
                                    axis=1).reshape(N_DEV, 2 * LANES, D_MODEL)
    grads["ple_w_proj"] = jnp.stack([_cols_to_blocks(d_wp0, N_DEV), _cols_to_blocks(d_wp1, N_DEV)],
                                    axis=1).reshape(N_DEV, 2 * 256, LANES)
    dy4, d_gapost = _ew_bwd(f_rms, [y4], [g_apost], [dx1], [0], [0], [BF16], name="a_norm_post_bwd")
    dy3 = _mm(dy4, w_out, tb=True, name="a_out_dx")
    grads["a_w_out"] = _mm(y3, dy4, ta=True, name="a_out_dw").reshape(N_DEV, LANES, D_MODEL)
    dg_direct, ds, dga, d_bglu = _a_gate_bwd(ys, u, s_glu, ga, d_skip, b_glu, dy3, f_a3)
    dg = _mm(ds, w_glu, tb=True, add=dg_direct, name="a_glu_dx")
    grads["a_w_glu"] = _mm(g_act, ds, ta=True, name="a_glu_dw").reshape(N_DEV, LANES, D_MODEL)
    dys, du_elem, d_dskip = _ew_bwd(f_a2, [ys, u], [d_skip], [dg], [0, 1], [0], [F32, F32], name="a_gelu_bwd")
    du, d_bbd, d_ccd, d_are, d_aim = _s5_bwd(u, dys, du_elem, hb, bbd_bf, ccd_bf, a_re, a_im)
    dh0 = _mm(dga, w_ga, tb=True, add=_mm(du, w_u, tb=True, name="a_proj_u_dx"), name="a_proj_gate_dx")
    grads["a_w_in"] = jnp.concatenate([_cols_to_blocks(_mm(h0, du, ta=True, name="a_proj_u_dw"), 4),
                                       _cols_to_blocks(_mm(h0, dga, ta=True, name="a_proj_gate_dw"), 4)])
    f_rms_res = lambda x0_, g: (_rms(x0_, g), x0_)
    grad_x, d_gapre = _ew_bwd(f_rms_res, [x0], [g_apre], [dh0, dx1], [0], [0], [F32], name="a_norm_pre_bwd")

    d_s5 = s5_vjp((d_are, d_aim, d_bbd, d_ccd))
    small = dict(zip(["a_lam_re", "a_lam_im", "a_log_dt", "a_b_re", "a_b_im", "a_c_re", "a_c_im"], d_s5))
    small.update(kv_norm=d_gkv, b_norm_pre=d_gbpre, b_norm_post=d_gbpost)
    small_names = list(small)
    small_flat = jnp.concatenate([small[n].reshape(-1) for n in small_names])
    small_pack = jnp.pad(small_flat, (0, -small_flat.size % (8 * LANES))).reshape(-1, LANES)
    vec_grads = jnp.concatenate([d_gapre, d_gapost, d_dskip, d_bglu], axis=0)
    vec_scatter = vec_grads.reshape(len(vec_names), N_DEV, LANES).transpose(1, 0, 2)
    mat_names = list(mats)
    scattered = _exchange([grads[n] for n in mat_names] + [vec_scatter], scatter=True, name="scatter_grads")
    (small_all,) = _exchange([small_pack], scatter=False, name="gather_small_grads")

    out_g, out_d, out_m, out_v = {}, {}, {}, {}

    def update(n, parts):
        shape = weights[n].shape
        rc = parts.shape[1:]
        res = _adamw(parts, weights[n].reshape(rc), mom_m[n].reshape(rc), mom_v[n].reshape(rc), name="adamw_" + n)
        out_g[n], out_d[n], out_m[n], out_v[n] = (r.reshape(shape) for r in res)

    for n, parts in zip(mat_names, scattered[:-1]):
        update(n, parts)
    for i, n in enumerate(vec_names):
        update(n, scattered[-1][:, i:i + 1, :])
    off = 0
    for n in small_names:
        size = small[n].size
        rows = size // LANES if size % LANES == 0 and size >= LANES else 1
        seg = small_all.reshape(N_DEV, -1)[:, off:off + size].reshape(N_DEV, rows, size // rows)
        update(n, seg)
        off += size

    return (loss, grad_x[None], *[out_g[n] for n in names], *[out_d[n] for n in names],
            *[out_m[n] for n in names], *[out_v[n] for n in names])


def _a_gate_bwd(ys, u, s_glu, ga, d_skip, b_glu, dy3, f_a3):
    def fn(ys_, u_, s_, ga_, ct, d_, b_):
        g = _gelu(ys_ + d_ * u_)
        _, vjp = jax.vjp(lambda g_, s__, ga__, b__: f_a3(g_, s__, ga__, b__)[0], g, s_, ga_, b_)
        dg, ds, dga, db = vjp(ct)
        return dg, ds, dga, db

    return _ew(fn, [ys, u, s_glu, ga, dy3], [d_skip, b_glu], [F32, BF16, F32], n_red=1, name="a_gate_bwd")
```

```python
import functools
import math

import jax
import jax.numpy as jnp
from jax import lax
from jax.experimental import pallas as pl
from jax.experimental.pallas import tpu as pltpu

F32 = jnp.float32
BF16 = jnp.bfloat16

N_DEV = 8
D_MODEL = 1024
GROUP_SIZE = 16
N_GROUPS = D_MODEL // GROUP_SIZE
STATE = 64
HEAD_DIM = 64
EPS = 1e-6
LANES = 128
GROUPS_PER_BLOCK = LANES // GROUP_SIZE
N_GBLK = N_GROUPS // GROUPS_PER_BLOCK
STATE_COLS = 2 * GROUPS_PER_BLOCK * STATE
N_SLAB = STATE_COLS // LANES
VMEM_LIMIT = 56 * 1024 * 1024
EXP_UNDERFLOW = -104.0

ADAM_LR = 0.001
ADAM_B1 = 0.9
ADAM_B2 = 0.999
ADAM_EPS = 1e-08
ADAM_WD = 0.01
ADAM_STEP = 10


def _pallas(body, **kw):
    return pl.pallas_call(body, **kw)


def _cparams(*sem):
    return pltpu.CompilerParams(dimension_semantics=sem, vmem_limit_bytes=VMEM_LIMIT)


def _mm(a, b, *, ta=False, tb=False, add=None, out_dtype=F32, name):
    m, k = (a.shape[1], a.shape[0]) if ta else a.shape
    n = b.shape[0] if tb else b.shape[1]
    tm = min(m, 1024 if ta else 512)
    tn = min(n, 1024)
    tk = min(k, 1024 if not ta else 512)
    nk = k // tk
    grid = (m // tm, n // tn, nk)
    a_spec = pl.BlockSpec((tk, tm), lambda i, j, l: (l, i)) if ta else pl.BlockSpec((tm, tk), lambda i, j, l: (i, l))
    b_spec = pl.BlockSpec((tn, tk), lambda i, j, l: (j, l)) if tb else pl.BlockSpec((tk, tn), lambda i, j, l: (l, j))
    o_spec = pl.BlockSpec((tm, tn), lambda i, j, l: (i, j))
    dims = (((0 if ta else 1,), (1 if tb else 0,)), ((), ()))
    has_add = add is not None

    def body(*refs):
        a_ref, b_ref = refs[0], refs[1]
        add_ref = refs[2] if has_add else None
        o_ref, acc_ref = refs[-2], refs[-1]
        l = pl.program_id(2)
        part = lax.dot_general(a_ref[...].astype(BF16), b_ref[...].astype(BF16), dims, preferred_element_type=F32)

        @pl.when(l == 0)
        def _():
            acc_ref[...] = part

        @pl.when(l > 0)
        def _():
            acc_ref[...] += part

        @pl.when(l == nk - 1)
        def _():
            r = acc_ref[...]
            if has_add:
                r = r + add_ref[...].astype(F32)
            o_ref[...] = r.astype(out_dtype)

    ins = [a, b] + ([add] if has_add else [])
    specs = [a_spec, b_spec] + ([o_spec] if has_add else [])
    return _pallas(
        body, name=name, grid=grid, in_specs=specs, out_specs=o_spec,
        out_shape=jax.ShapeDtypeStruct((m, n), out_dtype),
        scratch_shapes=[pltpu.VMEM((tm, tn), F32)],
        compiler_params=_cparams("parallel", "parallel", "arbitrary"),
    )(*ins)


def _row_spec(arr, tm):
    return pl.BlockSpec((tm, arr.shape[1]), lambda i: (i, 0))


def _full_spec(arr):
    return pl.BlockSpec(arr.shape, lambda i: (0,) * arr.ndim)


def _ew(fn, rows, vecs, out_dtypes, *, n_red=0, tm=256, name):
    t = rows[0].shape[0]
    nr, nv = len(rows), len(vecs)
    shapes = jax.eval_shape(fn, *[jax.ShapeDtypeStruct((tm, r.shape[1]), F32) for r in rows],
                            *[jax.ShapeDtypeStruct(v.shape, F32) for v in vecs])
    n_out = len(shapes) - n_red

    def body(*refs):
        ins = [r[...].astype(F32) for r in refs[:nr + nv]]
        outs = fn(*ins)
        o_refs = refs[nr + nv:]
        for o_ref, o in zip(o_refs[:n_out], outs[:n_out]):
            o_ref[...] = o.astype(o_ref.dtype)
        i = pl.program_id(0)
        for o_ref, o in zip(o_refs[n_out:], outs[n_out:]):
            @pl.when(i == 0)
            def _(o_ref=o_ref, o=o):
                o_ref[...] = o

            @pl.when(i > 0)
            def _(o_ref=o_ref, o=o):
                o_ref[...] += o

    out_shape = [jax.ShapeDtypeStruct((t, s.shape[1]), dt) for s, dt in zip(shapes[:n_out], out_dtypes)]
    out_shape += [jax.ShapeDtypeStruct(s.shape, F32) for s in shapes[n_out:]]
    out_specs = [pl.BlockSpec((tm, s.shape[1]), lambda i: (i, 0)) for s in shapes[:n_out]]
    out_specs += [pl.BlockSpec(s.shape, lambda i: (0, 0)) for s in shapes[n_out:]]
    return _pallas(
        body, name=name, grid=(t // tm,),
        in_specs=[_row_spec(r, tm) for r in rows] + [_full_spec(v) for v in vecs],
        out_specs=out_specs, out_shape=out_shape,
        compiler_params=_cparams("arbitrary" if n_red else "parallel"),
    )(*rows, *vecs)


def _ew_bwd(fn, rows, vecs, cts, row_wrt, vec_wrt, out_dtypes, *, tm=256, name):
    nr, nv, nc = len(rows), len(vecs), len(cts)

    def grad_fn(*tiles):
        ins = list(tiles[:nr + nv])
        ct = tiles[nr + nv:]
        wrt = list(row_wrt) + [nr + j for j in vec_wrt]

        def f(*sel):
            full = list(ins)
            for idx, s in zip(wrt, sel):
                full[idx] = s
            return tuple(fn(*full))

        _, vjp = jax.vjp(f, *[ins[idx] for idx in wrt])
        return vjp(tuple(ct))

    return _ew_reordered(grad_fn, rows, vecs, cts, out_dtypes, len(vec_wrt), tm, name)


def _ew_reordered(grad_fn, rows, vecs, cts, out_dtypes, n_red, tm, name):
    nr, nv = len(rows), len(vecs)

    def fn(*tiles):
        r, c, v = tiles[:nr], tiles[nr:nr + len(cts)], tiles[nr + len(cts):]
        return grad_fn(*r, *v, *c)

    return _ew(fn, list(rows) + list(cts), vecs, out_dtypes, n_red=n_red, tm=tm, name=name)


def _rms(x, g):
    return x * lax.rsqrt(jnp.mean(x * x, axis=-1, keepdims=True) + EPS) * g


def _sigmoid(x):
    return 1.0 / (1.0 + jnp.exp(-x))


def _silu(x):
    return x * _sigmoid(x)


def _gelu(x):
    return 0.5 * x * (1.0 + jnp.tanh(math.sqrt(2.0 / math.pi) * (x + 0.044715 * x * x * x)))


def _token_rows(t):
    return pl.ds(pl.multiple_of(t * N_GBLK, N_GBLK), N_GBLK)


def _block_rows(gb, tt):
    return pl.ds(gb, tt, stride=N_GBLK)


def _s5_expand(u_bf, bbd_ref, xs_ref, tt):
    for gb in range(N_GBLK):
        xg = jnp.dot(u_bf[:, gb * LANES:(gb + 1) * LANES], bbd_ref[gb], preferred_element_type=F32)
        for l in range(N_SLAB):
            xs_ref[l, _block_rows(gb, tt), :] = xg[:, l * LANES:(l + 1) * LANES]


def _s5_forward_scan(xs_ref, ar, ai, state, tt):
    half = N_SLAB // 2

    def step(t, st):
        new = [None] * N_SLAB
        for l in range(half):
            sr, si = st[l], st[half + l]
            rows = _token_rows(t)
            nr_ = ar[l] * sr - ai[l] * si + xs_ref[l, rows, :]
            ni_ = ar[l] * si + ai[l] * sr + xs_ref[half + l, rows, :]
            xs_ref[l, rows, :] = nr_
            xs_ref[half + l, rows, :] = ni_
            new[l], new[half + l] = nr_, ni_
        return tuple(new)

    return lax.fori_loop(0, tt, step, tuple(state), unroll=4)


def _slab_rows(ref, gb, tt):
    return jnp.concatenate([ref[l, _block_rows(gb, tt), :] for l in range(N_SLAB)], axis=1)


def _s5_fwd(u, bbd, ccd, a_re, a_im, *, tt=256):
    t = u.shape[0]
    nt = t // tt
    half = N_SLAB // 2

    def body(u_ref, bbd_ref, ccd_ref, ar_ref, ai_ref, y_ref, hb_ref, xs_ref, st_ref):
        i = pl.program_id(0)

        @pl.when(i == 0)
        def _():
            st_ref[...] = jnp.zeros_like(st_ref)

        hb_ref[0] = st_ref[...]
        _s5_expand(u_ref[...].astype(BF16), bbd_ref, xs_ref, tt)
        ar = [ar_ref[l] for l in range(half)]
        ai = [ai_ref[l] for l in range(half)]
        last = _s5_forward_scan(xs_ref, ar, ai, [st_ref[l] for l in range(N_SLAB)], tt)
        for l in range(N_SLAB):
            st_ref[l] = last[l]
        for gb in range(N_GBLK):
            sg = _slab_rows(xs_ref, gb, tt).astype(BF16)
            y_ref[:, gb * LANES:(gb + 1) * LANES] = jnp.dot(sg, ccd_ref[gb], preferred_element_type=F32)

    return _pallas(
        body, name="s5_fwd", grid=(nt,),
        in_specs=[pl.BlockSpec((tt, D_MODEL), lambda i: (i, 0)), _full_spec(bbd), _full_spec(ccd),
                  _full_spec(a_re), _full_spec(a_im)],
        out_specs=[pl.BlockSpec((tt, D_MODEL), lambda i: (i, 0)),
                   pl.BlockSpec((1, N_SLAB, N_GBLK, LANES), lambda i: (i, 0, 0, 0))],
        out_shape=[jax.ShapeDtypeStruct((t, D_MODEL), F32), jax.ShapeDtypeStruct((nt, N_SLAB, N_GBLK, LANES), F32)],
        scratch_shapes=[pltpu.VMEM((N_SLAB, N_GBLK * tt, LANES), F32), pltpu.VMEM((N_SLAB, N_GBLK, LANES), F32)],
        compiler_params=_cparams("arbitrary"),
    )(u, bbd, ccd, a_re, a_im)


def _s5_bwd(u, dy, du_add, hb, bbd, ccd, a_re, a_im, *, tt=256):
    t = u.shape[0]
    nt = t // tt
    half = N_SLAB // 2
    rev = lambda i: (nt - 1 - i, 0)

    def body(u_ref, dy_ref, dua_ref, hb_ref, bbd_ref, ccd_ref, ar_ref, ai_ref,
             du_ref, dbbd_ref, dccd_ref, dar_ref, dai_ref, xs_ref, es_ref, lam_ref):
        i = pl.program_id(0)

        @pl.when(i == 0)
        def _():
            lam_ref[...] = jnp.zeros_like(lam_ref)
            dbbd_ref[...] = jnp.zeros_like(dbbd_ref)
            dccd_ref[...] = jnp.zeros_like(dccd_ref)
            dar_ref[...] = jnp.zeros_like(dar_ref)
            dai_ref[...] = jnp.zeros_like(dai_ref)

        u_bf = u_ref[...].astype(BF16)
        dy_bf = dy_ref[...].astype(BF16)
        _s5_expand(u_bf, bbd_ref, xs_ref, tt)
        ar = [ar_ref[l] for l in range(half)]
        ai = [ai_ref[l] for l in range(half)]
        start = [hb_ref[0, l] for l in range(N_SLAB)]
        _s5_forward_scan(xs_ref, ar, ai, start, tt)
        for gb in range(N_GBLK):
            eg = lax.dot_general(dy_bf[:, gb * LANES:(gb + 1) * LANES], ccd_ref[gb], (((1,), (1,)), ((), ())),
                                 preferred_element_type=F32)
            for l in range(N_SLAB):
                es_ref[l, _block_rows(gb, tt), :] = eg[:, l * LANES:(l + 1) * LANES]

        def lam_step(t_, lam, prev, acc):
            new_lam, new_acc = [None] * N_SLAB, [None] * N_SLAB
            rows = _token_rows(t_)
            for l in range(half):
                lr = es_ref[l, rows, :] + ar[l] * lam[l] + ai[l] * lam[half + l]
                li = es_ref[half + l, rows, :] - ai[l] * lam[l] + ar[l] * lam[half + l]
                es_ref[l, rows, :] = lr
                es_ref[half + l, rows, :] = li
                pr, pi = prev(l), prev(half + l)
                new_acc[l] = acc[l] + lr * pr + li * pi
                new_acc[half + l] = acc[half + l] - lr * pi + li * pr
                new_lam[l], new_lam[half + l] = lr, li
            return tuple(new_lam), tuple(new_acc)

        def loop_body(k, carry):
            lam, acc = carry
            t_ = tt - 1 - k
            return lam_step(t_, lam, lambda l: xs_ref[l, _token_rows(t_ - 1), :], acc)

        zero = tuple(jnp.zeros((N_GBLK, LANES), F32) for _ in range(N_SLAB))
        lam0 = tuple(lam_ref[l] for l in range(N_SLAB))
        lam, acc = lax.fori_loop(0, tt - 1, loop_body, (lam0, zero), unroll=2)
        lam, acc = lam_step(0, lam, lambda l: start[l], acc)
        for l in range(N_SLAB):
            lam_ref[l] = lam[l]
        for l in range(half):
            dar_ref[l] += acc[l]
            dai_ref[l] += acc[half + l]

        for gb in range(N_GBLK):
            cols = slice(gb * LANES, (gb + 1) * LANES)
            lg = _slab_rows(es_ref, gb, tt).astype(BF16)
            sg = _slab_rows(xs_ref, gb, tt).astype(BF16)
            du_ref[:, cols] = (dua_ref[:, cols] + lax.dot_general(
                lg, bbd_ref[gb], (((1,), (1,)), ((), ())), preferred_element_type=F32)).astype(du_ref.dtype)
            dbbd_ref[gb] += lax.dot_general(u_bf[:, cols], lg, (((0,), (0,)), ((), ())), preferred_element_type=F32)
            dccd_ref[gb] += lax.dot_general(sg, dy_bf[:, cols], (((0,), (0,)), ((), ())), preferred_element_type=F32)

    row = pl.BlockSpec((tt, D_MODEL), rev)
    small = pl.BlockSpec((half, N_GBLK, LANES), lambda i: (0, 0, 0))
    return _pallas(
        body, name="s5_bwd", grid=(nt,),
        in_specs=[row, row, row, pl.BlockSpec((1, N_SLAB, N_GBLK, LANES), lambda i: (nt - 1 - i, 0, 0, 0)),
                  _full_spec(bbd), _full_spec(ccd), _full_spec(a_re), _full_spec(a_im)],
        out_specs=[row, _full_spec(bbd), _full_spec(ccd), small, small],
        out_shape=[jax.ShapeDtypeStruct((t, D_MODEL), BF16), jax.ShapeDtypeStruct(bbd.shape, F32),
                   jax.ShapeDtypeStruct(ccd.shape, F32), jax.ShapeDtypeStruct(a_re.shape, F32),
                   jax.ShapeDtypeStruct(a_im.shape, F32)],
        scratch_shapes=[pltpu.VMEM((N_SLAB, N_GBLK * tt, LANES), F32), pltpu.VMEM((N_SLAB, N_GBLK * tt, LANES), F32),
                        pltpu.VMEM((N_SLAB, N_GBLK, LANES), F32)],
        compiler_params=_cparams("arbitrary"),
    )(u, dy, du_add, hb, bbd, ccd, a_re, a_im)


def _s5_discretize(lam_re, lam_im, log_dt, b_re, b_im, c_re, c_im):
    lr = jnp.minimum(lam_re, -1e-4)
    li = lam_im
    dt = jnp.exp(log_dt)[:, None]
    mag = jnp.exp(lr * dt)
    a_re = mag * jnp.cos(li * dt)
    a_im = mag * jnp.sin(li * dt)
    den = lr * lr + li * li
    nr = a_re - 1.0
    f_re = (nr * lr + a_im * li) / den
    f_im = (a_im * lr - nr * li) / den
    bb_re = f_re[..., None] * b_re - f_im[..., None] * b_im
    bb_im = f_re[..., None] * b_im + f_im[..., None] * b_re
    eye = jnp.eye(GROUPS_PER_BLOCK, dtype=F32)
    bb = jnp.stack([bb_re, bb_im]).reshape(2, N_GBLK, GROUPS_PER_BLOCK, STATE, GROUP_SIZE)
    bbd = jnp.einsum('rgjph,jk->gjhrkp', bb, eye).reshape(N_GBLK, LANES, STATE_COLS)
    cc = jnp.stack([c_re, -c_im]).reshape(2, N_GBLK, GROUPS_PER_BLOCK, GROUP_SIZE, STATE)
    ccd = jnp.einsum('rgjhp,jk->grjpkh', cc, eye).reshape(N_GBLK, STATE_COLS, LANES)

    def dense(a):
        return a.reshape(N_GBLK, N_SLAB // 2, LANES).transpose(1, 0, 2)

    return dense(a_re), dense(a_im), bbd, ccd


def _attn_tiles(tq):
    row = lax.broadcasted_iota(jnp.int32, (tq, tq), 0)
    col = lax.broadcasted_iota(jnp.int32, (tq, tq), 1)
    return col < row, (row > col).astype(BF16), (row >= col).astype(BF16)


def _attn_scores(qa, kj, c, diag, later, masked):
    z = lax.dot_general(qa, kj, (((1,), (1,)), ((), ())), preferred_element_type=F32) * (HEAD_DIM ** -0.5)
    lk = -(jnp.maximum(z, 0.0) + jnp.log(1.0 + jnp.exp(-jnp.abs(z))))
    if masked:
        lk = jnp.where(diag, lk, 0.0)
    hi = lk.astype(BF16)
    lo = (lk - hi.astype(F32)).astype(BF16)
    tail = jnp.dot(hi, later, preferred_element_type=F32) + jnp.dot(lo, later, preferred_element_type=F32)
    w = jnp.exp(z + lk + tail + c)
    if masked:
        w = jnp.where(diag, w, 0.0)
    return z, lk, w


def _head_masks():
    lane = lax.broadcasted_iota(jnp.int32, (1, LANES), 1)
    return [(lane >= a * HEAD_DIM) & (lane < (a + 1) * HEAD_DIM) for a in range(LANES // HEAD_DIM)]


def _any_alive(cs):
    alive = jnp.max(cs[0]) > EXP_UNDERFLOW
    for c in cs[1:]:
        alive = alive | (jnp.max(c) > EXP_UNDERFLOW)
    return alive


def _attn_fwd(q, k, v, *, tq=256):
    t = q.shape[0]
    tq = min(tq, t)
    n_pair = D_MODEL // LANES

    def body(q_ref, k_ref, v_ref, o_ref, ol_ref):
        i = pl.program_id(1)
        diag, later, _ = _attn_tiles(tq)
        masks = _head_masks()
        q2 = q_ref[...]
        qs = [jnp.where(hm, q2, jnp.zeros_like(q2)) for hm in masks]

        def block(j, state, masked):
            rows = pl.ds(pl.multiple_of(j * tq, tq), tq)
            kj, vj = k_ref[rows, :], v_ref[rows, :]
            new = []
            for qa, (c, acc, low) in zip(qs, state):
                _, lk, w = _attn_scores(qa, kj, c, diag, later, masked)
                w_hi = w.astype(BF16)
                w_lo = (w - w_hi.astype(F32)).astype(BF16)
                new.append((c + jnp.sum(lk, axis=1, keepdims=True),
                            acc + jnp.dot(w_hi, vj, preferred_element_type=F32),
                            low + jnp.dot(w_lo, vj, preferred_element_type=F32)))
            return tuple(new)

        zero = jnp.zeros((tq, LANES), F32)
        state = block(i, tuple((jnp.zeros((tq, 1), F32), zero, zero) for _ in masks), True)

        def cond(st):
            return (st[0] >= 0) & _any_alive([h[0] for h in st[1]])

        def step(st):
            return st[0] - 1, block(st[0], st[1], False)

        _, state = lax.while_loop(cond, step, (i - 1, state))
        o_ref[...] = sum(jnp.where(hm, h[1], 0.0) for hm, h in zip(masks, state))
        ol_ref[...] = sum(jnp.where(hm, h[2], 0.0) for hm, h in zip(masks, state))

    qspec = pl.BlockSpec((tq, LANES), lambda h, i: (i, h))
    kspec = pl.BlockSpec((t, LANES), lambda h, i: (0, h))
    full = jax.ShapeDtypeStruct((t, D_MODEL), F32)
    return _pallas(
        body, name="attn_fwd", grid=(n_pair, t // tq), in_specs=[qspec, kspec, kspec], out_specs=[qspec, qspec],
        out_shape=[full, full], compiler_params=_cparams("parallel", "arbitrary"),
    )(q, k, v)


def _attn_bwd(q, k, v, o, o_low, do, *, tq=256):
    t = q.shape[0]
    tq = min(tq, t)
    n_pair = D_MODEL // LANES

    def body(q_ref, k_ref, v_ref, o_ref, ol_ref, do_ref, dq_ref, dk_ref, dv_ref):
        i = pl.program_id(1)

        @pl.when(i == 0)
        def _():
            dk_ref[...] = jnp.zeros_like(dk_ref)
            dv_ref[...] = jnp.zeros_like(dv_ref)

        diag, later, later_eq = _attn_tiles(tq)
        masks = _head_masks()
        q2 = q_ref[...]
        do2 = do_ref[...]
        dd = do2.astype(BF16).astype(F32) * (o_ref[...] + ol_ref[...])
        qs = [jnp.where(hm, q2, jnp.zeros_like(q2)) for hm in masks]
        dos = [jnp.where(hm, do2, 0.0).astype(BF16) for hm in masks]
        totals = [jnp.sum(jnp.where(hm, dd, 0.0), axis=1, keepdims=True) for hm in masks]

        def block(j, state, masked):
            rows = pl.ds(pl.multiple_of(j * tq, tq), tq)
            kj, vj = k_ref[rows, :], v_ref[rows, :]
            new, dk, dv = [], 0.0, 0.0
            for qa, doa, total, (c, r, acc) in zip(qs, dos, totals, state):
                z, lk, w = _attn_scores(qa, kj, c, diag, later, masked)
                dw = lax.dot_general(doa, vj, (((1,), (1,)), ((), ())), preferred_element_type=F32)
                g = dw * w
                g_hi = g.astype(BF16)
                g_lo = (g - g_hi.astype(F32)).astype(BF16)
                from_here = (jnp.dot(g_hi, later_eq, preferred_element_type=F32)
                             + jnp.dot(g_lo, later_eq, preferred_element_type=F32))
                before = total - r - from_here
                beta = jnp.exp(z + lk)
                dz = g * (1.0 - beta) - beta * before
                if masked:
                    dz = jnp.where(diag, dz, 0.0)
                dz_bf = (dz * (HEAD_DIM ** -0.5)).astype(BF16)
                dk = dk + lax.dot_general(dz_bf, qa, (((0,), (0,)), ((), ())), preferred_element_type=F32)
                dv = dv + lax.dot_general(w.astype(BF16), doa, (((0,), (0,)), ((), ())), preferred_element_type=F32)
                new.append((c + jnp.sum(lk, axis=1, keepdims=True), r + jnp.sum(g, axis=1, keepdims=True),
                            acc + jnp.dot(dz_bf, kj, preferred_element_type=F32)))
            dk_ref[rows, :] += dk
            dv_ref[rows, :] += dv
            return tuple(new)

        zc = jnp.zeros((tq, 1), F32)
        state = block(i, tuple((zc, zc, jnp.zeros((tq, LANES), F32)) for _ in masks), True)

        def cond(st):
            return (st[0] >= 0) & _any_alive([h[0] for h in st[1]])

        def step(st):
            return st[0] - 1, block(st[0], st[1], False)

        _, state = lax.while_loop(cond, step, (i - 1, state))
        dq_ref[...] = sum(jnp.where(hm, h[2], 0.0) for hm, h in zip(masks, state))

    qspec = pl.BlockSpec((tq, LANES), lambda h, i: (i, h))
    kspec = pl.BlockSpec((t, LANES), lambda h, i: (0, h))
    full = jax.ShapeDtypeStruct((t, D_MODEL), F32)
    return _pallas(
        body, name="attn_bwd", grid=(n_pair, t // tq), in_specs=[qspec, kspec, kspec, qspec, qspec, qspec],
        out_specs=[qspec, kspec, kspec], out_shape=[full, full, full],
        compiler_params=_cparams("parallel", "arbitrary"),
    )(q, k, v, o, o_low, do)


N_CHIP = 4
MESH_ID = pl.DeviceIdType.MESH
ANY_SPEC = pl.BlockSpec(memory_space=pl.ANY)


def _position():
    return lax.axis_index("x"), lax.axis_index("y"), lax.axis_index("c")


def _other_chips(x, y):
    return [(1 - x, y), (x, 1 - y), (1 - x, 1 - y)]


def _gather(arrs, *, name):
    n = len(arrs)

    def body(*refs):
        ins, outs = refs[:n], refs[n:2 * n]
        send_sems, recv_sems, local_sems = refs[2 * n:]
        x, y, c = _position()
        me, sibling = (x, y, c), (x, y, 1 - c)
        chips = _other_chips(x, y)

        def copy(a, k, block, to, src=None):
            slot = outs[a].at[4 * block[0] + 2 * block[1] + block[2]]
            return pltpu.make_async_remote_copy(
                src_ref=slot if src is None else src, dst_ref=slot, send_sem=send_sems.at[a, k],
                recv_sem=recv_sems.at[a, k], device_id=to, device_id_type=MESH_ID)

        local = [pltpu.make_async_copy(ins[a], outs[a].at[4 * x + 2 * y + c], local_sems.at[a]) for a in range(n)]
        first = []
        for a in range(n):
            local[a].start()
            first.append(copy(a, 0, me, sibling, src=ins[a]))
            first += [copy(a, 1 + j, me, (*chip, c), src=ins[a]) for j, chip in enumerate(chips)]
        for cp in first:
            cp.start()
        passed = []
        for a in range(n):
            for j, chip in enumerate(chips):
                copy(a, 1 + j, (*chip, c), me).wait_recv()
                passed.append(copy(a, 4 + j, (*chip, c), sibling))
                passed[-1].start()
        for a in range(n):
            copy(a, 0, sibling, me).wait_recv()
            for j, chip in enumerate(chips):
                copy(a, 4 + j, (*chip, 1 - c), me).wait_recv()
        for cp in first + passed:
            cp.wait_send()
        for cp in local:
            cp.wait()

    return _pallas(
        body, name=name, in_specs=[ANY_SPEC] * n, out_specs=[ANY_SPEC] * n,
        out_shape=[jax.ShapeDtypeStruct((N_DEV,) + a.shape, a.dtype) for a in arrs],
        scratch_shapes=[pltpu.SemaphoreType.DMA((n, N_DEV - 1)), pltpu.SemaphoreType.DMA((n, N_DEV - 1)),
                        pltpu.SemaphoreType.DMA((n,))],
    )(*arrs)


def _pair_exchange(arrs, *, name):
    n = len(arrs)

    def body(*refs):
        ins, kept, got = refs[:n], refs[n:2 * n], refs[2 * n:3 * n]
        send_sems, recv_sems, local_sems = refs[3 * n:]
        x, y, c = _position()
        sends, locals_ = [], []
        for a in range(n):
            for ch in range(N_CHIP):
                locals_.append(pltpu.make_async_copy(ins[a].at[2 * ch + c], kept[a].at[ch], local_sems.at[a, ch]))
                sends.append(pltpu.make_async_remote_copy(
                    src_ref=ins[a].at[2 * ch + (1 - c)], dst_ref=got[a].at[ch], send_sem=send_sems.at[a, ch],
                    recv_sem=recv_sems.at[a, ch], device_id=(x, y, 1 - c), device_id_type=MESH_ID))
        for cp in locals_ + sends:
            cp.start()
        for cp in sends:
            cp.wait_send()
            cp.wait_recv()
        for cp in locals_:
            cp.wait()

    quarter = [jax.ShapeDtypeStruct((N_CHIP,) + a.shape[1:], a.dtype) for a in arrs]
    res = _pallas(
        body, name=name, in_specs=[ANY_SPEC] * n, out_specs=[ANY_SPEC] * (2 * n), out_shape=quarter + quarter,
        scratch_shapes=[pltpu.SemaphoreType.DMA((n, N_CHIP)), pltpu.SemaphoreType.DMA((n, N_CHIP)),
                        pltpu.SemaphoreType.DMA((n, N_CHIP))],
    )(*arrs)
    return res[:n], res[n:]


def _chip_exchange(arrs, *, name):
    n = len(arrs)

    def body(*refs):
        ins, outs = refs[:n], refs[n:2 * n]
        send_sems, recv_sems, local_sems = refs[2 * n:]
        x, y, c = _position()
        mine = 2 * x + y
        chips = _other_chips(x, y)
        sends, recvs, locals_ = [], [], []
        for a in range(n):
            locals_.append(pltpu.make_async_copy(ins[a].at[mine], outs[a].at[mine], local_sems.at[a]))
            for j, (px, py) in enumerate(chips):
                sends.append(pltpu.make_async_remote_copy(
                    src_ref=ins[a].at[2 * px + py], dst_ref=outs[a].at[mine], send_sem=send_sems.at[a, j],
                    recv_sem=recv_sems.at[a, j], device_id=(px, py, c), device_id_type=MESH_ID))
                recvs.append(pltpu.make_async_remote_copy(
                    src_ref=ins[a].at[2 * px + py], dst_ref=outs[a].at[2 * px + py], send_sem=send_sems.at[a, j],
                    recv_sem=recv_sems.at[a, j], device_id=(px, py, c), device_id_type=MESH_ID))
        for cp in locals_ + sends:
            cp.start()
        for cp in sends:
            cp.wait_send()
        for cp in recvs:
            cp.wait_recv()
        for cp in locals_:
            cp.wait()

    return _pallas(
        body, name=name, in_specs=[ANY_SPEC] * n, out_specs=[ANY_SPEC] * n,
        out_shape=[jax.ShapeDtypeStruct(a.shape, a.dtype) for a in arrs],
        scratch_shapes=[pltpu.SemaphoreType.DMA((n, N_CHIP - 1)), pltpu.SemaphoreType.DMA((n, N_CHIP - 1)),
                        pltpu.SemaphoreType.DMA((n,))],
    )(*arrs)


def _pair_sum(a, b, *, name):
    s_, r, c = a.shape
    tr = r if r <= 256 else 256

    def body(a_ref, b_ref, o_ref):
        o_ref[...] = a_ref[...] + b_ref[...]

    spec = pl.BlockSpec((s_, tr, c), lambda i: (0, i, 0))
    return _pallas(body, name=name, grid=(r // tr,), in_specs=[spec, spec], out_specs=spec,
                   out_shape=jax.ShapeDtypeStruct(a.shape, a.dtype), compiler_params=_cparams("parallel"))(a, b)


def _reduce_scatter(arrs, names):
    kept, got = _pair_exchange(arrs, name="pair_exchange")
    sums = [_pair_sum(k_, g_, name="pair_sum_" + n) for k_, g_, n in zip(kept, got, names)]
    return _chip_exchange(sums, name="chip_exchange")


def _adamw(parts, w, m, v, *, name):
    s, r, c = parts.shape
    tr = r if r <= 256 else 256

    def body(p_ref, w_ref, m_ref, v_ref, g_ref, d_ref, nm_ref, nv_ref):
        g = p_ref[0]
        for j in range(1, s):
            g = g + p_ref[j]
        nm = ADAM_B1 * m_ref[...] + (1.0 - ADAM_B1) * g
        nv = ADAM_B2 * v_ref[...] + (1.0 - ADAM_B2) * (g * g)
        m_hat = nm / (1.0 - ADAM_B1 ** ADAM_STEP)
        v_hat = nv / (1.0 - ADAM_B2 ** ADAM_STEP)
        g_ref[...] = g
        d_ref[...] = -ADAM_LR * (m_hat / (jnp.sqrt(v_hat) + ADAM_EPS) + ADAM_WD * w_ref[...])
        nm_ref[...] = nm
        nv_ref[...] = nv

    spec = pl.BlockSpec((tr, c), lambda i: (i, 0))
    one = jax.ShapeDtypeStruct((r, c), F32)
    return _pallas(
        body, name=name, grid=(r // tr,), in_specs=[pl.BlockSpec((s, tr, c), lambda i: (0, i, 0)), spec, spec, spec],
        out_specs=[spec] * 4, out_shape=[one] * 4, compiler_params=_cparams("parallel"),
    )(parts, w, m, v)


def _cols_to_blocks(g, n_blk):
    r = g.shape[0]
    return g.reshape(r, n_blk, g.shape[1] // n_blk).transpose(1, 0, 2)


def _blocks_to_cols(gathered):
    n_blk, r, c = gathered.shape
    return gathered.transpose(1, 0, 2).reshape(r, n_blk * c)


def kernel(x, p, a_norm_pre, a_norm_post, a_w_in, a_lam_re, a_lam_im, a_log_dt, a_b_re, a_b_im, a_c_re, a_c_im, a_d_skip, a_w_glu, a_b_glu, a_w_out, kv_norm, w_kv, b_norm_pre, b_norm_post, b_w_in, b_w_out, ple_w_proj, ple_w_gate, loss_target, m_a_norm_pre, m_a_norm_post, m_a_w_in, m_a_lam_re, m_a_lam_im, m_a_log_dt, m_a_b_re, m_a_b_im, m_a_c_re, m_a_c_im, m_a_d_skip, m_a_w_glu, m_a_b_glu, m_a_w_out, m_kv_norm, m_w_kv, m_b_norm_pre, m_b_norm_post, m_b_w_in, m_b_w_out, m_ple_w_proj, m_ple_w_gate, v_a_norm_pre, v_a_norm_post, v_a_w_in, v_a_lam_re, v_a_lam_im, v_a_log_dt, v_a_b_re, v_a_b_im, v_a_c_re, v_a_c_im, v_a_d_skip, v_a_w_glu, v_a_b_glu, v_a_w_out, v_kv_norm, v_w_kv, v_b_norm_pre, v_b_norm_post, v_b_w_in, v_b_w_out, v_ple_w_proj, v_ple_w_gate):
    weights = dict(a_norm_pre=a_norm_pre, a_norm_post=a_norm_post, a_w_in=a_w_in, a_lam_re=a_lam_re, a_lam_im=a_lam_im, a_log_dt=a_log_dt, a_b_re=a_b_re, a_b_im=a_b_im, a_c_re=a_c_re, a_c_im=a_c_im, a_d_skip=a_d_skip, a_w_glu=a_w_glu, a_b_glu=a_b_glu, a_w_out=a_w_out, kv_norm=kv_norm, w_kv=w_kv, b_norm_pre=b_norm_pre, b_norm_post=b_norm_post, b_w_in=b_w_in, b_w_out=b_w_out, ple_w_proj=ple_w_proj, ple_w_gate=ple_w_gate)
    mom_m = dict(a_norm_pre=m_a_norm_pre, a_norm_post=m_a_norm_post, a_w_in=m_a_w_in, a_lam_re=m_a_lam_re, a_lam_im=m_a_lam_im, a_log_dt=m_a_log_dt, a_b_re=m_a_b_re, a_b_im=m_a_b_im, a_c_re=m_a_c_re, a_c_im=m_a_c_im, a_d_skip=m_a_d_skip, a_w_glu=m_a_w_glu, a_b_glu=m_a_b_glu, a_w_out=m_a_w_out, kv_norm=m_kv_norm, w_kv=m_w_kv, b_norm_pre=m_b_norm_pre, b_norm_post=m_b_norm_post, b_w_in=m_b_w_in, b_w_out=m_b_w_out, ple_w_proj=m_ple_w_proj, ple_w_gate=m_ple_w_gate)
    mom_v = dict(a_norm_pre=v_a_norm_pre, a_norm_post=v_a_norm_post, a_w_in=v_a_w_in, a_lam_re=v_a_lam_re, a_lam_im=v_a_lam_im, a_log_dt=v_a_log_dt, a_b_re=v_a_b_re, a_b_im=v_a_b_im, a_c_re=v_a_c_re, a_c_im=v_a_c_im, a_d_skip=v_a_d_skip, a_w_glu=v_a_w_glu, a_b_glu=v_a_b_glu, a_w_out=v_a_w_out, kv_norm=v_kv_norm, w_kv=v_w_kv, b_norm_pre=v_b_norm_pre, b_norm_post=v_b_norm_post, b_w_in=v_b_w_in, b_w_out=v_b_w_out, ple_w_proj=v_ple_w_proj, ple_w_gate=v_ple_w_gate)
    names = list(weights)

    t = x.shape[1]
    x0 = x[0]
    p0, p1 = p[0, 0], p[1, 0]
    tgt = loss_target[0]

    mats = dict(a_w_in=a_w_in[0], a_w_glu=a_w_glu[0], a_w_out=a_w_out[0], w_kv=w_kv, b_w_in=b_w_in[0],
                b_w_out=b_w_out[0], ple_w_proj=ple_w_proj.reshape(2 * 256, LANES),
                ple_w_gate=ple_w_gate.reshape(2 * LANES, D_MODEL))
    vec_names = ["a_norm_pre", "a_norm_post", "a_d_skip", "a_b_glu"]
    vec_pack = jnp.concatenate([weights[n] for n in vec_names], axis=0)
    gathered = _gather([m_.astype(BF16) for m_ in mats.values()] + [vec_pack], name="gather_weights")
    gw = dict(zip(mats, gathered[:-1]))
    vec_full = gathered[-1].transpose(1, 0, 2).reshape(len(vec_names), D_MODEL)
    g_apre, g_apost, d_skip, b_glu = (vec_full[i:i + 1] for i in range(4))
    w_u, w_ga = _blocks_to_cols(gw["a_w_in"][:4]), _blocks_to_cols(gw["a_w_in"][4:])
    w_glu = gw["a_w_glu"].reshape(D_MODEL, D_MODEL)
    w_out = gw["a_w_out"].reshape(D_MODEL, D_MODEL)
    w_k, w_v = _blocks_to_cols(gw["w_kv"][:4]), _blocks_to_cols(gw["w_kv"][4:])
    w_q, w_gb = _blocks_to_cols(gw["b_w_in"][:4]), _blocks_to_cols(gw["b_w_in"][4:])
    w_bout = gw["b_w_out"].reshape(D_MODEL, D_MODEL)
    wp = gw["ple_w_proj"].reshape(N_DEV, 2, 256, LANES)
    w_p0, w_p1 = _blocks_to_cols(wp[:, 0]), _blocks_to_cols(wp[:, 1])
    wg = gw["ple_w_gate"].reshape(N_DEV, 2, LANES, D_MODEL)
    w_g0, w_g1 = wg[:, 0].reshape(D_MODEL, D_MODEL), wg[:, 1].reshape(D_MODEL, D_MODEL)
    g_kv, g_bpre, g_bpost = kv_norm.reshape(1, D_MODEL), b_norm_pre, b_norm_post

    s5_params = (a_lam_re[0], a_lam_im[0], a_log_dt[0], a_b_re[0], a_b_im[0], a_c_re[0], a_c_im[0])
    (a_re, a_im, bbd, ccd), s5_vjp = jax.vjp(_s5_discretize, *s5_params)
    bbd_bf, ccd_bf = bbd.astype(BF16), ccd.astype(BF16)

    f_a2 = lambda ys, u, d: (_gelu(ys + d * u),)
    f_a3 = lambda g, s, ga, b: (g * _sigmoid(s + b) * _silu(ga),)
    f_ple = lambda xin, pg, pp: (xin + _sigmoid(pg) * pp,)
    f_res_rms = lambda xin, y, g: (xin + _rms(y, g),)
    f_rms = lambda y, g: (_rms(y, g),)
    f_k = lambda x2, gk, gp: (_rms(x2, gk), _rms(x2, gp))
    f_b3 = lambda o, gb: (o * _silu(gb),)

    (h0,) = _ew(f_rms, [x0], [g_apre], [BF16], name="a_norm_pre")
    u = _mm(h0, w_u, name="a_proj_u")
    ga = _mm(h0, w_ga, name="a_proj_gate")
    ys, hb = _s5_fwd(u, bbd_bf, ccd_bf, a_re, a_im)
    (g_act,) = _ew(f_a2, [ys, u], [d_skip], [BF16], name="a_gelu")
    s_glu = _mm(g_act, w_glu, name="a_glu")
    (y3,) = _ew(lambda ys_, u_, s_, ga_, d_, b_: f_a3(_gelu(ys_ + d_ * u_), s_, ga_, b_),
                [ys, u, s_glu, ga], [d_skip, b_glu], [BF16], name="a_gate")
    y4 = _mm(y3, w_out, name="a_out")
    (x1,) = _ew(f_res_rms, [x0, y4], [g_apost], [F32], name="a_norm_post")
    pg0 = _mm(x1, w_g0, name="ple0_gate")
    pp0 = _mm(p0, w_p0, name="ple0_proj")
    (x2,) = _ew(f_ple, [x1, pg0, pp0], [], [F32], name="ple0")
    kvn, h1 = _ew(f_k, [x2], [g_kv, g_bpre], [BF16, BF16], name="b_norms")
    k_ = _mm(kvn, w_k, out_dtype=BF16, name="kv_k")
    v_ = _mm(kvn, w_v, out_dtype=BF16, name="kv_v")
    q_ = _mm(h1, w_q, out_dtype=BF16, name="b_proj_q")
    gb = _mm(h1, w_gb, name="b_proj_gate")
    o, o_low = _attn_fwd(q_, k_, v_)
    (y5in,) = _ew(f_b3, [o, gb], [], [BF16], name="b_gate")
    y5 = _mm(y5in, w_bout, name="b_out")
    (x3,) = _ew(f_res_rms, [x2, y5], [g_bpost], [F32], name="b_norm_post")
    pg1 = _mm(x3, w_g1, name="ple1_gate")
    pp1 = _mm(p1, w_p1, name="ple1_proj")

    def f_loss(xin, pg, pp, tg):
        err = xin + _sigmoid(pg) * pp - tg
        return err * (1.0 / D_MODEL), (0.5 / D_MODEL) * jnp.sum(err * err).reshape(1, 1)

    dx4, loss_part = _ew(f_loss, [x3, pg1, pp1, tgt], [], [F32], n_red=1, name="ple1_loss")
    loss = lax.psum(loss_part[0, 0], ("x", "y", "c"))

    grads = {}
    f_ple_gate = lambda pg, pp: (_sigmoid(pg) * pp,)
    dpg1, dpp1 = _ew_bwd(f_ple_gate, [pg1, pp1], [], [dx4], [0, 1], [], [BF16, BF16], name="ple1_bwd")
    dx3 = _mm(dpg1, w_g1, tb=True, add=dx4, name="ple1_gate_dx")
    d_wg1 = _mm(x3, dpg1, ta=True, name="ple1_gate_dw")
    d_wp1 = _mm(p1, dpp1, ta=True, name="ple1_proj_dw")
    dy5, d_gbpost = _ew_bwd(f_rms, [y5], [g_bpost], [dx3], [0], [0], [BF16], name="b_norm_post_bwd")
    dy5in = _mm(dy5, w_bout, tb=True, name="b_out_dx")
    grads["b_w_out"] = _mm(y5in, dy5, ta=True, name="b_out_dw").reshape(N_DEV, LANES, D_MODEL)
    do, dgb = _ew_bwd(f_b3, [o, gb], [], [dy5in], [0, 1], [], [F32, BF16], name="b_gate_bwd")
    dq, dk, dv = _attn_bwd(q_, k_, v_, o, o_low, do)
    dh1 = _mm(dgb, w_gb, tb=True, add=_mm(dq, w_q, tb=True, name="b_proj_q_dx"), name="b_proj_gate_dx")
    grads["b_w_in"] = jnp.concatenate([_cols_to_blocks(_mm(h1, dq, ta=True, name="b_proj_q_dw"), 4),
                                       _cols_to_blocks(_mm(h1, dgb, ta=True, name="b_proj_gate_dw"), 4)])
    dkvn = _mm(dv, w_v, tb=True, add=_mm(dk, w_k, tb=True, name="kv_k_dx"), name="kv_v_dx")
    grads["w_kv"] = jnp.concatenate([_cols_to_blocks(_mm(kvn, dk, ta=True, name="kv_k_dw"), 4),
                                     _cols_to_blocks(_mm(kvn, dv, ta=True, name="kv_v_dw"), 4)])
    f_k3 = lambda x2_, gk, gp: f_k(x2_, gk, gp) + (x2_,)
    dx2, d_gkv, d_gbpre = _ew_bwd(f_k3, [x2], [g_kv, g_bpre], [dkvn, dh1, dx3], [0], [0, 1], [F32], name="b_norms_bwd")
    dpg0, dpp0 = _ew_bwd(f_ple_gate, [pg0, pp0], [], [dx2], [0, 1], [], [BF16, BF16], name="ple0_bwd")
    dx1 = _mm(dpg0, w_g0, tb=True, add=dx2, name="ple0_gate_dx")
    d_wg0 = _mm(x1, dpg0, ta=True, name="ple0_gate_dw")
    d_wp0 = _mm(p0, dpp0, ta=True, name="ple0_proj_dw")
    grads["ple_w_gate"] = jnp.stack([d_wg0.reshape(N_DEV, LANES, D_MODEL), d_wg1.reshape(N_DEV, LANES, D_MODEL)],
                                    axis=1).reshape(N_DEV, 2 * LANES, D_MODEL)
    grads["ple_w_proj"] = jnp.stack([_cols_to_blocks(d_wp0, N_DEV), _cols_to_blocks(d_wp1, N_DEV)],
                                    axis=1).reshape(N_DEV, 2 * 256, LANES)
    dy4, d_gapost = _ew_bwd(f_rms, [y4], [g_apost], [dx1], [0], [0], [BF16], name="a_norm_post_bwd")
    dy3 = _mm(dy4, w_out, tb=True, name="a_out_dx")
    grads["a_w_out"] = _mm(y3, dy4, ta=True, name="a_out_dw").reshape(N_DEV, LANES, D_MODEL)
    dg_direct, ds, dga, d_bglu = _a_gate_bwd(ys, u, s_glu, ga, d_skip, b_glu, dy3, f_a3)
    dg = _mm(ds, w_glu, tb=True, add=dg_direct, name="a_glu_dx")
    grads["a_w_glu"] = _mm(g_act, ds, ta=True, name="a_glu_dw").reshape(N_DEV, LANES, D_MODEL)
    dys, du_elem, d_dskip = _ew_bwd(f_a2, [ys, u], [d_skip], [dg], [0, 1], [0], [F32, F32], name="a_gelu_bwd")
    du, d_bbd, d_ccd, d_are, d_aim = _s5_bwd(u, dys, du_elem, hb, bbd_bf, ccd_bf, a_re, a_im)
    dh0 = _mm(dga, w_ga, tb=True, add=_mm(du, w_u, tb=True, name="a_proj_u_dx"), name="a_proj_gate_dx")
    grads["a_w_in"] = jnp.concatenate([_cols_to_blocks(_mm(h0, du, ta=True, name="a_proj_u_dw"), 4),
                                       _cols_to_blocks(_mm(h0, dga, ta=True, name="a_proj_gate_dw"), 4)])
    f_rms_res = lambda x0_, g: (_rms(x0_, g), x0_)
    grad_x, d_gapre = _ew_bwd(f_rms_res, [x0], [g_apre], [dh0, dx1], [0], [0], [F32], name="a_norm_pre_bwd")

    d_s5 = s5_vjp((d_are, d_aim, d_bbd, d_ccd))
    small = dict(zip(["a_lam_re", "a_lam_im", "a_log_dt", "a_b_re", "a_b_im", "a_c_re", "a_c_im"], d_s5))
    small.update(kv_norm=d_gkv, b_norm_pre=d_gbpre, b_norm_post=d_gbpost)
    small_names = list(small)
    small_flat = jnp.concatenate([small[n].reshape(-1) for n in small_names])
    small_pack = jnp.pad(small_flat, (0, -small_flat.size % (8 * LANES))).reshape(-1, LANES)
    vec_grads = jnp.concatenate([d_gapre, d_gapost, d_dskip, d_bglu], axis=0)
    vec_scatter = vec_grads.reshape(len(vec_names), N_DEV, LANES).transpose(1, 0, 2)
    mat_names = list(mats)
    scattered = _reduce_scatter([grads[n] for n in mat_names] + [vec_scatter], mat_names + ["vectors"])
    (small_all,) = _gather([small_pack], name="gather_small_grads")

    out_g, out_d, out_m, out_v = {}, {}, {}, {}

    def update(n, parts):
        shape = weights[n].shape
        rc = parts.shape[1:]
        res = _adamw(parts, weights[n].reshape(rc), mom_m[n].reshape(rc), mom_v[n].reshape(rc), name="adamw_" + n)
        out_g[n], out_d[n], out_m[n], out_v[n] = (r.reshape(shape) for r in res)

    for n, parts in zip(mat_names, scattered[:-1]):
        update(n, parts)
    for i, n in enumerate(vec_names):
        update(n, scattered[-1][:, i:i + 1, :])
    off = 0
    for n in small_names:
        size = small[n].size
        rows = size // LANES if size % LANES == 0 and size >= LANES else 1
        seg = small_all.reshape(N_DEV, -1)[:, off:off + size].reshape(N_DEV, rows, size // rows)
        update(n, seg)
        off += size

    return (loss, grad_x[None], *[out_g[n] for n in names], *[out_d[n] for n in names],
            *[out_m[n] for n in names], *[out_v[n] for n in names])


def _a_gate_bwd(ys, u, s_glu, ga, d_skip, b_glu, dy3, f_a3):
    def fn(ys_, u_, s_, ga_, ct, d_, b_):
        g = _gelu(ys_ + d_ * u_)
        _, vjp = jax.vjp(lambda g_, s__, ga__, b__: f_a3(g_, s__, ga__, b__)[0], g, s_, ga_, b_)
        dg, ds, dga, db = vjp(ct)
        return dg, ds, dga, db

    return _ew(fn, [ys, u, s_glu, ga, dy3], [d_skip, b_glu], [F32, BF16, BF16], n_red=1, name="a_gate_bwd")
```

```python
import functools
import math

import jax
import jax.numpy as jnp
from jax import lax
from jax.experimental import pallas as pl
from jax.experimental.pallas import tpu as pltpu

F32 = jnp.float32
BF16 = jnp.bfloat16

N_DEV = 8
D_MODEL = 1024
GROUP_SIZE = 16
N_GROUPS = D_MODEL // GROUP_SIZE
STATE = 64
HEAD_DIM = 64
EPS = 1e-6
LANES = 128
GROUPS_PER_BLOCK = LANES // GROUP_SIZE
N_GBLK = N_GROUPS // GROUPS_PER_BLOCK
STATE_COLS = 2 * GROUPS_PER_BLOCK * STATE
N_SLAB = STATE_COLS // LANES
VMEM_LIMIT = 56 * 1024 * 1024
EXP_UNDERFLOW = -104.0
MM_TILE = 1024
EW_ROWS_BYTES = 24 * 1024 * 1024

ADAM_LR = 0.001
ADAM_B1 = 0.9
ADAM_B2 = 0.999
ADAM_EPS = 1e-08
ADAM_WD = 0.01
ADAM_STEP = 10


def _pallas(body, **kw):
    return pl.pallas_call(body, **kw)


def _cparams(*sem):
    return pltpu.CompilerParams(dimension_semantics=sem, vmem_limit_bytes=VMEM_LIMIT)


def _mm(a, b, *, ta=False, tb=False, add=None, out_dtype=F32, name):
    m, k = (a.shape[1], a.shape[0]) if ta else a.shape
    n = b.shape[0] if tb else b.shape[1]
    tm, tn, tk = min(m, MM_TILE), min(n, MM_TILE), min(k, MM_TILE)
    nk = k // tk
    grid = (m // tm, n // tn, nk)
    a_spec = pl.BlockSpec((tk, tm), lambda i, j, l: (l, i)) if ta else pl.BlockSpec((tm, tk), lambda i, j, l: (i, l))
    b_spec = pl.BlockSpec((tn, tk), lambda i, j, l: (j, l)) if tb else pl.BlockSpec((tk, tn), lambda i, j, l: (l, j))
    o_spec = pl.BlockSpec((tm, tn), lambda i, j, l: (i, j))
    dims = (((0 if ta else 1,), (1 if tb else 0,)), ((), ()))
    has_add = add is not None
    assert nk == 1 or (out_dtype == F32 and not has_add)

    def body(*refs):
        a_ref, b_ref, o_ref = refs[0], refs[1], refs[-1]
        part = lax.dot_general(a_ref[...].astype(BF16), b_ref[...].astype(BF16), dims, preferred_element_type=F32)
        if nk == 1:
            if has_add:
                part = part + refs[2][...].astype(F32)
            o_ref[...] = part.astype(out_dtype)
        else:
            l = pl.program_id(2)

            @pl.when(l == 0)
            def _():
                o_ref[...] = part

            @pl.when(l > 0)
            def _():
                o_ref[...] += part

    ins = [a, b] + ([add] if has_add else [])
    specs = [a_spec, b_spec] + ([o_spec] if has_add else [])
    return _pallas(
        body, name=name, grid=grid, in_specs=specs, out_specs=o_spec,
        out_shape=jax.ShapeDtypeStruct((m, n), out_dtype),
        compiler_params=_cparams("parallel", "parallel", "arbitrary"),
    )(*ins)


def _row_spec(arr, tm):
    return pl.BlockSpec((tm, arr.shape[1]), lambda i: (i, 0))


def _full_spec(arr):
    return pl.BlockSpec(arr.shape, lambda i: (0,) * arr.ndim)


def _ew_tile_rows(t, row_bytes):
    tm = 1024
    while tm > 8 and (t % tm or 2 * tm * row_bytes > EW_ROWS_BYTES):
        tm //= 2
    return tm


def _ew(fn, rows, vecs, out_dtypes, *, n_red=0, name):
    t = rows[0].shape[0]
    nr, nv = len(rows), len(vecs)

    def tile_shapes(tm):
        return jax.eval_shape(fn, *[jax.ShapeDtypeStruct((tm, r.shape[1]), F32) for r in rows],
                              *[jax.ShapeDtypeStruct(v.shape, F32) for v in vecs])

    probe = tile_shapes(8)
    row_bytes = sum(r.shape[1] * r.dtype.itemsize for r in rows)
    row_bytes += sum(s.shape[1] * jnp.dtype(dt).itemsize for s, dt in zip(probe, out_dtypes))
    tm = _ew_tile_rows(t, row_bytes)
    shapes = tile_shapes(tm)
    n_out = len(shapes) - n_red

    def body(*refs):
        ins = [r[...].astype(F32) for r in refs[:nr + nv]]
        outs = fn(*ins)
        o_refs = refs[nr + nv:]
        for o_ref, o in zip(o_refs[:n_out], outs[:n_out]):
            o_ref[...] = o.astype(o_ref.dtype)
        i = pl.program_id(0)
        for o_ref, o in zip(o_refs[n_out:], outs[n_out:]):
            @pl.when(i == 0)
            def _(o_ref=o_ref, o=o):
                o_ref[...] = o

            @pl.when(i > 0)
            def _(o_ref=o_ref, o=o):
                o_ref[...] += o

    out_shape = [jax.ShapeDtypeStruct((t, s.shape[1]), dt) for s, dt in zip(shapes[:n_out], out_dtypes)]
    out_shape += [jax.ShapeDtypeStruct(s.shape, F32) for s in shapes[n_out:]]
    out_specs = [pl.BlockSpec((tm, s.shape[1]), lambda i: (i, 0)) for s in shapes[:n_out]]
    out_specs += [pl.BlockSpec(s.shape, lambda i: (0, 0)) for s in shapes[n_out:]]
    return _pallas(
        body, name=name, grid=(t // tm,),
        in_specs=[_row_spec(r, tm) for r in rows] + [_full_spec(v) for v in vecs],
        out_specs=out_specs, out_shape=out_shape,
        compiler_params=_cparams("arbitrary" if n_red else "parallel"),
    )(*rows, *vecs)


def _ew_bwd(fn, rows, vecs, cts, row_wrt, vec_wrt, out_dtypes, *, name):
    nr, nv, nc = len(rows), len(vecs), len(cts)

    def grad_fn(*tiles):
        ins = list(tiles[:nr + nv])
        ct = tiles[nr + nv:]
        wrt = list(row_wrt) + [nr + j for j in vec_wrt]

        def f(*sel):
            full = list(ins)
            for idx, s in zip(wrt, sel):
                full[idx] = s
            return tuple(fn(*full))

        _, vjp = jax.vjp(f, *[ins[idx] for idx in wrt])
        return vjp(tuple(ct))

    return _ew_reordered(grad_fn, rows, vecs, cts, out_dtypes, len(vec_wrt), name)


def _ew_reordered(grad_fn, rows, vecs, cts, out_dtypes, n_red, name):
    nr, nv = len(rows), len(vecs)

    def fn(*tiles):
        r, c, v = tiles[:nr], tiles[nr:nr + len(cts)], tiles[nr + len(cts):]
        return grad_fn(*r, *v, *c)

    return _ew(fn, list(rows) + list(cts), vecs, out_dtypes, n_red=n_red, name=name)


def _rms(x, g):
    return x * lax.rsqrt(jnp.mean(x * x, axis=-1, keepdims=True) + EPS) * g


def _sigmoid(x):
    return 1.0 / (1.0 + jnp.exp(-x))


def _silu(x):
    return x * _sigmoid(x)


def _gelu(x):
    return 0.5 * x * (1.0 + jnp.tanh(math.sqrt(2.0 / math.pi) * (x + 0.044715 * x * x * x)))


def _token_rows(t):
    return pl.ds(pl.multiple_of(t * N_GBLK, N_GBLK), N_GBLK)


def _block_rows(gb, tt):
    return pl.ds(gb, tt, stride=N_GBLK)


def _s5_expand(u_bf, bbd_ref, xs_ref, tt):
    for gb in range(N_GBLK):
        xg = jnp.dot(u_bf[:, gb * LANES:(gb + 1) * LANES], bbd_ref[gb], preferred_element_type=F32)
        for l in range(N_SLAB):
            xs_ref[l, _block_rows(gb, tt), :] = xg[:, l * LANES:(l + 1) * LANES]


def _s5_forward_scan(xs_ref, ar, ai, state, tt):
    half = N_SLAB // 2

    def step(t, st):
        new = [None] * N_SLAB
        for l in range(half):
            sr, si = st[l], st[half + l]
            rows = _token_rows(t)
            nr_ = ar[l] * sr - ai[l] * si + xs_ref[l, rows, :]
            ni_ = ar[l] * si + ai[l] * sr + xs_ref[half + l, rows, :]
            xs_ref[l, rows, :] = nr_
            xs_ref[half + l, rows, :] = ni_
            new[l], new[half + l] = nr_, ni_
        return tuple(new)

    return lax.fori_loop(0, tt, step, tuple(state), unroll=4)


def _slab_rows(ref, gb, tt):
    return jnp.concatenate([ref[l, _block_rows(gb, tt), :] for l in range(N_SLAB)], axis=1)


def _s5_fwd(u, bbd, ccd, a_re, a_im, *, tt=256):
    t = u.shape[0]
    nt = t // tt
    half = N_SLAB // 2

    def body(u_ref, bbd_ref, ccd_ref, ar_ref, ai_ref, y_ref, hb_ref, xs_ref, st_ref):
        i = pl.program_id(0)

        @pl.when(i == 0)
        def _():
            st_ref[...] = jnp.zeros_like(st_ref)

        hb_ref[0] = st_ref[...]
        _s5_expand(u_ref[...].astype(BF16), bbd_ref, xs_ref, tt)
        ar = [ar_ref[l] for l in range(half)]
        ai = [ai_ref[l] for l in range(half)]
        last = _s5_forward_scan(xs_ref, ar, ai, [st_ref[l] for l in range(N_SLAB)], tt)
        for l in range(N_SLAB):
            st_ref[l] = last[l]
        for gb in range(N_GBLK):
            sg = _slab_rows(xs_ref, gb, tt).astype(BF16)
            y_ref[:, gb * LANES:(gb + 1) * LANES] = jnp.dot(sg, ccd_ref[gb], preferred_element_type=F32)

    return _pallas(
        body, name="s5_fwd", grid=(nt,),
        in_specs=[pl.BlockSpec((tt, D_MODEL), lambda i: (i, 0)), _full_spec(bbd), _full_spec(ccd),
                  _full_spec(a_re), _full_spec(a_im)],
        out_specs=[pl.BlockSpec((tt, D_MODEL), lambda i: (i, 0)),
                   pl.BlockSpec((1, N_SLAB, N_GBLK, LANES), lambda i: (i, 0, 0, 0))],
        out_shape=[jax.ShapeDtypeStruct((t, D_MODEL), F32), jax.ShapeDtypeStruct((nt, N_SLAB, N_GBLK, LANES), F32)],
        scratch_shapes=[pltpu.VMEM((N_SLAB, N_GBLK * tt, LANES), F32), pltpu.VMEM((N_SLAB, N_GBLK, LANES), F32)],
        compiler_params=_cparams("arbitrary"),
    )(u, bbd, ccd, a_re, a_im)


def _s5_bwd(u, dy, du_add, hb, bbd, ccd, a_re, a_im, *, tt=256):
    t = u.shape[0]
    nt = t // tt
    half = N_SLAB // 2
    rev = lambda i: (nt - 1 - i, 0)

    def body(u_ref, dy_ref, dua_ref, hb_ref, bbd_ref, ccd_ref, ar_ref, ai_ref,
             du_ref, dbbd_ref, dccd_ref, dar_ref, dai_ref, xs_ref, es_ref, lam_ref):
        i = pl.program_id(0)

        @pl.when(i == 0)
        def _():
            lam_ref[...] = jnp.zeros_like(lam_ref)
            dbbd_ref[...] = jnp.zeros_like(dbbd_ref)
            dccd_ref[...] = jnp.zeros_like(dccd_ref)
            dar_ref[...] = jnp.zeros_like(dar_ref)
            dai_ref[...] = jnp.zeros_like(dai_ref)

        u_bf = u_ref[...].astype(BF16)
        dy_bf = dy_ref[...].astype(BF16)
        _s5_expand(u_bf, bbd_ref, xs_ref, tt)
        ar = [ar_ref[l] for l in range(half)]
        ai = [ai_ref[l] for l in range(half)]
        start = [hb_ref[0, l] for l in range(N_SLAB)]
        _s5_forward_scan(xs_ref, ar, ai, start, tt)
        for gb in range(N_GBLK):
            eg = lax.dot_general(dy_bf[:, gb * LANES:(gb + 1) * LANES], ccd_ref[gb], (((1,), (1,)), ((), ())),
                                 preferred_element_type=F32)
            for l in range(N_SLAB):
                es_ref[l, _block_rows(gb, tt), :] = eg[:, l * LANES:(l + 1) * LANES]

        def lam_step(t_, lam, prev, acc):
            new_lam, new_acc = [None] * N_SLAB, [None] * N_SLAB
            rows = _token_rows(t_)
            for l in range(half):
                lr = es_ref[l, rows, :] + ar[l] * lam[l] + ai[l] * lam[half + l]
                li = es_ref[half + l, rows, :] - ai[l] * lam[l] + ar[l] * lam[half + l]
                es_ref[l, rows, :] = lr
                es_ref[half + l, rows, :] = li
                pr, pi = prev(l), prev(half + l)
                new_acc[l] = acc[l] + lr * pr + li * pi
                new_acc[half + l] = acc[half + l] - lr * pi + li * pr
                new_lam[l], new_lam[half + l] = lr, li
            return tuple(new_lam), tuple(new_acc)

        def loop_body(k, carry):
            lam, acc = carry
            t_ = tt - 1 - k
            return lam_step(t_, lam, lambda l: xs_ref[l, _token_rows(t_ - 1), :], acc)

        zero = tuple(jnp.zeros((N_GBLK, LANES), F32) for _ in range(N_SLAB))
        lam0 = tuple(lam_ref[l] for l in range(N_SLAB))
        lam, acc = lax.fori_loop(0, tt - 1, loop_body, (lam0, zero), unroll=2)
        lam, acc = lam_step(0, lam, lambda l: start[l], acc)
        for l in range(N_SLAB):
            lam_ref[l] = lam[l]
        for l in range(half):
            dar_ref[l] += acc[l]
            dai_ref[l] += acc[half + l]

        for gb in range(N_GBLK):
            cols = slice(gb * LANES, (gb + 1) * LANES)
            lg = _slab_rows(es_ref, gb, tt).astype(BF16)
            sg = _slab_rows(xs_ref, gb, tt).astype(BF16)
            du_ref[:, cols] = (dua_ref[:, cols] + lax.dot_general(
                lg, bbd_ref[gb], (((1,), (1,)), ((), ())), preferred_element_type=F32)).astype(du_ref.dtype)
            dbbd_ref[gb] += lax.dot_general(u_bf[:, cols], lg, (((0,), (0,)), ((), ())), preferred_element_type=F32)
            dccd_ref[gb] += lax.dot_general(sg, dy_bf[:, cols], (((0,), (0,)), ((), ())), preferred_element_type=F32)

    row = pl.BlockSpec((tt, D_MODEL), rev)
    small = pl.BlockSpec((half, N_GBLK, LANES), lambda i: (0, 0, 0))
    return _pallas(
        body, name="s5_bwd", grid=(nt,),
        in_specs=[row, row, row, pl.BlockSpec((1, N_SLAB, N_GBLK, LANES), lambda i: (nt - 1 - i, 0, 0, 0)),
                  _full_spec(bbd), _full_spec(ccd), _full_spec(a_re), _full_spec(a_im)],
        out_specs=[row, _full_spec(bbd), _full_spec(ccd), small, small],
        out_shape=[jax.ShapeDtypeStruct((t, D_MODEL), BF16), jax.ShapeDtypeStruct(bbd.shape, F32),
                   jax.ShapeDtypeStruct(ccd.shape, F32), jax.ShapeDtypeStruct(a_re.shape, F32),
                   jax.ShapeDtypeStruct(a_im.shape, F32)],
        scratch_shapes=[pltpu.VMEM((N_SLAB, N_GBLK * tt, LANES), F32), pltpu.VMEM((N_SLAB, N_GBLK * tt, LANES), F32),
                        pltpu.VMEM((N_SLAB, N_GBLK, LANES), F32)],
        compiler_params=_cparams("arbitrary"),
    )(u, dy, du_add, hb, bbd, ccd, a_re, a_im)


def _s5_discretize(lam_re, lam_im, log_dt, b_re, b_im, c_re, c_im):
    lr = jnp.minimum(lam_re, -1e-4)
    li = lam_im
    dt = jnp.exp(log_dt)[:, None]
    mag = jnp.exp(lr * dt)
    a_re = mag * jnp.cos(li * dt)
    a_im = mag * jnp.sin(li * dt)
    den = lr * lr + li * li
    nr = a_re - 1.0
    f_re = (nr * lr + a_im * li) / den
    f_im = (a_im * lr - nr * li) / den
    bb_re = f_re[..., None] * b_re - f_im[..., None] * b_im
    bb_im = f_re[..., None] * b_im + f_im[..., None] * b_re
    eye = jnp.eye(GROUPS_PER_BLOCK, dtype=F32)
    bb = jnp.stack([bb_re, bb_im]).reshape(2, N_GBLK, GROUPS_PER_BLOCK, STATE, GROUP_SIZE)
    bbd = jnp.einsum('rgjph,jk->gjhrkp', bb, eye).reshape(N_GBLK, LANES, STATE_COLS)
    cc = jnp.stack([c_re, -c_im]).reshape(2, N_GBLK, GROUPS_PER_BLOCK, GROUP_SIZE, STATE)
    ccd = jnp.einsum('rgjhp,jk->grjpkh', cc, eye).reshape(N_GBLK, STATE_COLS, LANES)

    def dense(a):
        return a.reshape(N_GBLK, N_SLAB // 2, LANES).transpose(1, 0, 2)

    return dense(a_re), dense(a_im), bbd, ccd


def _attn_tiles(tq):
    row = lax.broadcasted_iota(jnp.int32, (tq, tq), 0)
    col = lax.broadcasted_iota(jnp.int32, (tq, tq), 1)
    return col < row, (row > col).astype(BF16), (row >= col).astype(BF16)


def _split_bf16(x):
    hi = x.astype(BF16)
    return hi, (x - hi.astype(F32)).astype(BF16)


def _attn_scores(qs, kj, cs, diag, later, masked):
    zs = [lax.dot_general(qa, kj, (((1,), (1,)), ((), ())), preferred_element_type=F32) * (HEAD_DIM ** -0.5) for qa in qs]
    lks = [-(jnp.maximum(z, 0.0) + jnp.log(1.0 + jnp.exp(-jnp.abs(z)))) for z in zs]
    if masked:
        lks = [jnp.where(diag, lk, 0.0) for lk in lks]
    parts = [_split_bf16(lk) for lk in lks]
    tails = [jnp.dot(hi, later, preferred_element_type=F32) + jnp.dot(lo, later, preferred_element_type=F32)
             for hi, lo in parts]
    ws = [jnp.exp(z + lk + tail + c) for z, lk, tail, c in zip(zs, lks, tails, cs)]
    if masked:
        ws = [jnp.where(diag, w, 0.0) for w in ws]
    return zs, lks, ws


def _head_masks():
    lane = lax.broadcasted_iota(jnp.int32, (1, LANES), 1)
    return [(lane >= a * HEAD_DIM) & (lane < (a + 1) * HEAD_DIM) for a in range(LANES // HEAD_DIM)]


def _any_alive(cs):
    alive = jnp.max(cs[0]) > EXP_UNDERFLOW
    for c in cs[1:]:
        alive = alive | (jnp.max(c) > EXP_UNDERFLOW)
    return alive


def _attn_fwd(q, k, v, *, tq=256):
    t = q.shape[0]
    tq = min(tq, t)
    n_pair = D_MODEL // LANES

    def body(q_ref, k_ref, v_ref, o_ref, ol_ref):
        i = pl.program_id(1)
        diag, later, _ = _attn_tiles(tq)
        masks = _head_masks()
        q2 = q_ref[...]
        qs = [jnp.where(hm, q2, jnp.zeros_like(q2)) for hm in masks]

        def block(j, state, masked):
            rows = pl.ds(pl.multiple_of(j * tq, tq), tq)
            kj, vj = k_ref[rows, :], v_ref[rows, :]
            _, lks, ws = _attn_scores(qs, kj, [h[0] for h in state], diag, later, masked)
            parts = [_split_bf16(w) for w in ws]
            return tuple((c + jnp.sum(lk, axis=1, keepdims=True),
                          acc + jnp.dot(w_hi, vj, preferred_element_type=F32),
                          low + jnp.dot(w_lo, vj, preferred_element_type=F32))
                         for (c, acc, low), lk, (w_hi, w_lo) in zip(state, lks, parts))

        zero = jnp.zeros((tq, LANES), F32)
        state = block(i, tuple((jnp.zeros((tq, 1), F32), zero, zero) for _ in masks), True)

        def cond(st):
            return (st[0] >= 0) & _any_alive([h[0] for h in st[1]])

        def step(st):
            return st[0] - 1, block(st[0], st[1], False)

        _, state = lax.while_loop(cond, step, (i - 1, state))
        o_ref[...] = sum(jnp.where(hm, h[1], 0.0) for hm, h in zip(masks, state))
        ol_ref[...] = sum(jnp.where(hm, h[2], 0.0) for hm, h in zip(masks, state))

    qspec = pl.BlockSpec((tq, LANES), lambda h, i: (i, h))
    kspec = pl.BlockSpec((t, LANES), lambda h, i: (0, h))
    full = jax.ShapeDtypeStruct((t, D_MODEL), F32)
    return _pallas(
        body, name="attn_fwd", grid=(n_pair, t // tq), in_specs=[qspec, kspec, kspec], out_specs=[qspec, qspec],
        out_shape=[full, full], compiler_params=_cparams("parallel", "arbitrary"),
    )(q, k, v)


def _attn_bwd(q, k, v, o, o_low, do, *, tq=256):
    t = q.shape[0]
    tq = min(tq, t)
    n_pair = D_MODEL // LANES

    def body(q_ref, k_ref, v_ref, o_ref, ol_ref, do_ref, dq_ref, dk_ref, dv_ref):
        i = pl.program_id(1)

        @pl.when(i == 0)
        def _():
            dk_ref[...] = jnp.zeros_like(dk_ref)
            dv_ref[...] = jnp.zeros_like(dv_ref)

        diag, later, later_eq = _attn_tiles(tq)
        masks = _head_masks()
        q2 = q_ref[...]
        do2 = do_ref[...]
        dd = do2.astype(BF16).astype(F32) * (o_ref[...] + ol_ref[...])
        qs = [jnp.where(hm, q2, jnp.zeros_like(q2)) for hm in masks]
        dos = [jnp.where(hm, do2, 0.0).astype(BF16) for hm in masks]
        totals = [jnp.sum(jnp.where(hm, dd, 0.0), axis=1, keepdims=True) for hm in masks]

        def block(j, state, masked):
            rows = pl.ds(pl.multiple_of(j * tq, tq), tq)
            kj, vj = k_ref[rows, :], v_ref[rows, :]
            nt = (((1,), (1,)), ((), ()))
            tn = (((0,), (0,)), ((), ()))
            zs, lks, ws = _attn_scores(qs, kj, [h[0] for h in state], diag, later, masked)
            dws = [lax.dot_general(doa, vj, nt, preferred_element_type=F32) for doa in dos]
            gs = [dw * w for dw, w in zip(dws, ws)]
            parts = [_split_bf16(g) for g in gs]
            from_here = [jnp.dot(hi, later_eq, preferred_element_type=F32) + jnp.dot(lo, later_eq, preferred_element_type=F32)
                         for hi, lo in parts]
            dzs = []
            for z, lk, g, fh, total, h in zip(zs, lks, gs, from_here, totals, state):
                beta = jnp.exp(z + lk)
                dz = g * (1.0 - beta) - beta * (total - h[1] - fh)
                if masked:
                    dz = jnp.where(diag, dz, 0.0)
                dzs.append((dz * (HEAD_DIM ** -0.5)).astype(BF16))
            dk = sum(lax.dot_general(dz, qa, tn, preferred_element_type=F32) for dz, qa in zip(dzs, qs))
            dv = sum(lax.dot_general(w.astype(BF16), doa, tn, preferred_element_type=F32) for w, doa in zip(ws, dos))
            new = [(c + jnp.sum(lk, axis=1, keepdims=True), r + jnp.sum(g, axis=1, keepdims=True),
                    acc + jnp.dot(dz, kj, preferred_element_type=F32))
                   for (c, r, acc), lk, g, dz in zip(state, lks, gs, dzs)]
            dk_ref[rows, :] += dk
            dv_ref[rows, :] += dv
            return tuple(new)

        zc = jnp.zeros((tq, 1), F32)
        state = block(i, tuple((zc, zc, jnp.zeros((tq, LANES), F32)) for _ in masks), True)

        def cond(st):
            return (st[0] >= 0) & _any_alive([h[0] for h in st[1]])

        def step(st):
            return st[0] - 1, block(st[0], st[1], False)

        _, state = lax.while_loop(cond, step, (i - 1, state))
        dq_ref[...] = sum(jnp.where(hm, h[2], 0.0) for hm, h in zip(masks, state))

    qspec = pl.BlockSpec((tq, LANES), lambda h, i: (i, h))
    kspec = pl.BlockSpec((t, LANES), lambda h, i: (0, h))
    full = jax.ShapeDtypeStruct((t, D_MODEL), F32)
    return _pallas(
        body, name="attn_bwd", grid=(n_pair, t // tq), in_specs=[qspec, kspec, kspec, qspec, qspec, qspec],
        out_specs=[qspec, kspec, kspec], out_shape=[full, full, full],
        compiler_params=_cparams("parallel", "arbitrary"),
    )(q, k, v, o, o_low, do)


N_CHIP = 4
MESH_ID = pl.DeviceIdType.MESH
ANY_SPEC = pl.BlockSpec(memory_space=pl.ANY)


def _position():
    return lax.axis_index("x"), lax.axis_index("y"), lax.axis_index("c")


def _other_chips(x, y):
    return [(1 - x, y), (x, 1 - y), (1 - x, 1 - y)]


def _gather(arrs, *, name):
    n = len(arrs)

    def body(*refs):
        ins, outs = refs[:n], refs[n:2 * n]
        send_sems, recv_sems, local_sems = refs[2 * n:]
        x, y, c = _position()
        me, sibling = (x, y, c), (x, y, 1 - c)
        chips = _other_chips(x, y)

        def copy(a, k, block, to, src=None):
            slot = outs[a].at[4 * block[0] + 2 * block[1] + block[2]]
            return pltpu.make_async_remote_copy(
                src_ref=slot if src is None else src, dst_ref=slot, send_sem=send_sems.at[a, k],
                recv_sem=recv_sems.at[a, k], device_id=to, device_id_type=MESH_ID)

        local = [pltpu.make_async_copy(ins[a], outs[a].at[4 * x + 2 * y + c], local_sems.at[a]) for a in range(n)]
        first = []
        for a in range(n):
            local[a].start()
            first.append(copy(a, 0, me, sibling, src=ins[a]))
            first += [copy(a, 1 + j, me, (*chip, c), src=ins[a]) for j, chip in enumerate(chips)]
        for cp in first:
            cp.start()
        passed = []
        for a in range(n):
            for j, chip in enumerate(chips):
                copy(a, 1 + j, (*chip, c), me).wait_recv()
                passed.append(copy(a, 4 + j, (*chip, c), sibling))
                passed[-1].start()
        for a in range(n):
            copy(a, 0, sibling, me).wait_recv()
            for j, chip in enumerate(chips):
                copy(a, 4 + j, (*chip, 1 - c), me).wait_recv()
        for cp in first + passed:
            cp.wait_send()
        for cp in local:
            cp.wait()

    return _pallas(
        body, name=name, in_specs=[ANY_SPEC] * n, out_specs=[ANY_SPEC] * n,
        out_shape=[jax.ShapeDtypeStruct((N_DEV,) + a.shape, a.dtype) for a in arrs],
        scratch_shapes=[pltpu.SemaphoreType.DMA((n, N_DEV - 1)), pltpu.SemaphoreType.DMA((n, N_DEV - 1)),
                        pltpu.SemaphoreType.DMA((n,))],
    )(*arrs)


def _pair_exchange(arrs, *, name):
    n = len(arrs)

    def body(*refs):
        ins, got = refs[:n], refs[n:2 * n]
        send_sems, recv_sems = refs[2 * n:]
        x, y, c = _position()
        sends = []
        for a in range(n):
            for ch in range(N_CHIP):
                sends.append(pltpu.make_async_remote_copy(
                    src_ref=ins[a].at[2 * ch + (1 - c)], dst_ref=got[a].at[ch], send_sem=send_sems.at[a, ch],
                    recv_sem=recv_sems.at[a, ch], device_id=(x, y, 1 - c), device_id_type=MESH_ID))
        for cp in sends:
            cp.start()
        for cp in sends:
            cp.wait_send()
            cp.wait_recv()

    return _pallas(
        body, name=name, in_specs=[ANY_SPEC] * n, out_specs=[ANY_SPEC] * n,
        out_shape=[jax.ShapeDtypeStruct((N_CHIP,) + a.shape[1:], a.dtype) for a in arrs],
        scratch_shapes=[pltpu.SemaphoreType.DMA((n, N_CHIP)), pltpu.SemaphoreType.DMA((n, N_CHIP))],
    )(*arrs)


def _chip_exchange(arrs, *, name):
    n = len(arrs)

    def body(*refs):
        ins, outs = refs[:n], refs[n:2 * n]
        send_sems, recv_sems, local_sems = refs[2 * n:]
        x, y, c = _position()
        mine = 2 * x + y
        chips = _other_chips(x, y)
        sends, recvs, locals_ = [], [], []
        for a in range(n):
            locals_.append(pltpu.make_async_copy(ins[a].at[mine], outs[a].at[mine], local_sems.at[a]))
            for j, (px, py) in enumerate(chips):
                sends.append(pltpu.make_async_remote_copy(
                    src_ref=ins[a].at[2 * px + py], dst_ref=outs[a].at[mine], send_sem=send_sems.at[a, j],
                    recv_sem=recv_sems.at[a, j], device_id=(px, py, c), device_id_type=MESH_ID))
                recvs.append(pltpu.make_async_remote_copy(
                    src_ref=ins[a].at[2 * px + py], dst_ref=outs[a].at[2 * px + py], send_sem=send_sems.at[a, j],
                    recv_sem=recv_sems.at[a, j], device_id=(px, py, c), device_id_type=MESH_ID))
        for cp in locals_ + sends:
            cp.start()
        for cp in sends:
            cp.wait_send()
        for cp in recvs:
            cp.wait_recv()
        for cp in locals_:
            cp.wait()

    return _pallas(
        body, name=name, in_specs=[ANY_SPEC] * n, out_specs=[ANY_SPEC] * n,
        out_shape=[jax.ShapeDtypeStruct(a.shape, a.dtype) for a in arrs],
        scratch_shapes=[pltpu.SemaphoreType.DMA((n, N_CHIP - 1)), pltpu.SemaphoreType.DMA((n, N_CHIP - 1)),
                        pltpu.SemaphoreType.DMA((n,))],
    )(*arrs)


def _pair_sum(mine, got, core, *, name):
    s_, r, c = got.shape
    tr = r if r <= 256 else 256

    def body(core_ref, a_ref, b_ref, o_ref):
        o_ref[...] = a_ref[...] + b_ref[...]

    quarter = pl.BlockSpec((s_, tr, c), lambda i, core_ref: (0, i, 0))
    return _pallas(
        body, name=name, out_shape=jax.ShapeDtypeStruct(got.shape, got.dtype),
        grid_spec=pltpu.PrefetchScalarGridSpec(
            num_scalar_prefetch=1, grid=(r // tr,),
            in_specs=[pl.BlockSpec((s_, None, tr, c), lambda i, core_ref: (0, core_ref[0], i, 0)), quarter],
            out_specs=quarter),
        compiler_params=_cparams("parallel"),
    )(core, mine.reshape(s_, 2, r, c), got)


def _reduce_scatter(arrs, names):
    got = _pair_exchange(arrs, name="pair_exchange")
    core = lax.axis_index("c").astype(jnp.int32).reshape(1)
    sums = [_pair_sum(a, g_, core, name="pair_sum_" + n) for a, g_, n in zip(arrs, got, names)]
    return _chip_exchange(sums, name="chip_exchange")


def _adamw(parts, w, m, v, *, name):
    s, r, c = parts.shape
    tr = r if r <= 256 else 256

    def body(p_ref, w_ref, m_ref, v_ref, g_ref, d_ref, nm_ref, nv_ref):
        g = p_ref[0]
        for j in range(1, s):
            g = g + p_ref[j]
        nm = ADAM_B1 * m_ref[...] + (1.0 - ADAM_B1) * g
        nv = ADAM_B2 * v_ref[...] + (1.0 - ADAM_B2) * (g * g)
        m_hat = nm / (1.0 - ADAM_B1 ** ADAM_STEP)
        v_hat = nv / (1.0 - ADAM_B2 ** ADAM_STEP)
        g_ref[...] = g
        d_ref[...] = -ADAM_LR * (m_hat / (jnp.sqrt(v_hat) + ADAM_EPS) + ADAM_WD * w_ref[...])
        nm_ref[...] = nm
        nv_ref[...] = nv

    spec = pl.BlockSpec((tr, c), lambda i: (i, 0))
    one = jax.ShapeDtypeStruct((r, c), F32)
    return _pallas(
        body, name=name, grid=(r // tr,), in_specs=[pl.BlockSpec((s, tr, c), lambda i: (0, i, 0)), spec, spec, spec],
        out_specs=[spec] * 4, out_shape=[one] * 4, compiler_params=_cparams("parallel"),
    )(parts, w, m, v)


def _cols_to_blocks(g, n_blk):
    r = g.shape[0]
    return g.reshape(r, n_blk, g.shape[1] // n_blk).transpose(1, 0, 2)


def _blocks_to_cols(gathered):
    n_blk, r, c = gathered.shape
    return gathered.transpose(1, 0, 2).reshape(r, n_blk * c)


def kernel(x, p, a_norm_pre, a_norm_post, a_w_in, a_lam_re, a_lam_im, a_log_dt, a_b_re, a_b_im, a_c_re, a_c_im, a_d_skip, a_w_glu, a_b_glu, a_w_out, kv_norm, w_kv, b_norm_pre, b_norm_post, b_w_in, b_w_out, ple_w_proj, ple_w_gate, loss_target, m_a_norm_pre, m_a_norm_post, m_a_w_in, m_a_lam_re, m_a_lam_im, m_a_log_dt, m_a_b_re, m_a_b_im, m_a_c_re, m_a_c_im, m_a_d_skip, m_a_w_glu, m_a_b_glu, m_a_w_out, m_kv_norm, m_w_kv, m_b_norm_pre, m_b_norm_post, m_b_w_in, m_b_w_out, m_ple_w_proj, m_ple_w_gate, v_a_norm_pre, v_a_norm_post, v_a_w_in, v_a_lam_re, v_a_lam_im, v_a_log_dt, v_a_b_re, v_a_b_im, v_a_c_re, v_a_c_im, v_a_d_skip, v_a_w_glu, v_a_b_glu, v_a_w_out, v_kv_norm, v_w_kv, v_b_norm_pre, v_b_norm_post, v_b_w_in, v_b_w_out, v_ple_w_proj, v_ple_w_gate):
    weights = dict(a_norm_pre=a_norm_pre, a_norm_post=a_norm_post, a_w_in=a_w_in, a_lam_re=a_lam_re, a_lam_im=a_lam_im, a_log_dt=a_log_dt, a_b_re=a_b_re, a_b_im=a_b_im, a_c_re=a_c_re, a_c_im=a_c_im, a_d_skip=a_d_skip, a_w_glu=a_w_glu, a_b_glu=a_b_glu, a_w_out=a_w_out, kv_norm=kv_norm, w_kv=w_kv, b_norm_pre=b_norm_pre, b_norm_post=b_norm_post, b_w_in=b_w_in, b_w_out=b_w_out, ple_w_proj=ple_w_proj, ple_w_gate=ple_w_gate)
    mom_m = dict(a_norm_pre=m_a_norm_pre, a_norm_post=m_a_norm_post, a_w_in=m_a_w_in, a_lam_re=m_a_lam_re, a_lam_im=m_a_lam_im, a_log_dt=m_a_log_dt, a_b_re=m_a_b_re, a_b_im=m_a_b_im, a_c_re=m_a_c_re, a_c_im=m_a_c_im, a_d_skip=m_a_d_skip, a_w_glu=m_a_w_glu, a_b_glu=m_a_b_glu, a_w_out=m_a_w_out, kv_norm=m_kv_norm, w_kv=m_w_kv, b_norm_pre=m_b_norm_pre, b_norm_post=m_b_norm_post, b_w_in=m_b_w_in, b_w_out=m_b_w_out, ple_w_proj=m_ple_w_proj, ple_w_gate=m_ple_w_gate)
    mom_v = dict(a_norm_pre=v_a_norm_pre, a_norm_post=v_a_norm_post, a_w_in=v_a_w_in, a_lam_re=v_a_lam_re, a_lam_im=v_a_lam_im, a_log_dt=v_a_log_dt, a_b_re=v_a_b_re, a_b_im=v_a_b_im, a_c_re=v_a_c_re, a_c_im=v_a_c_im, a_d_skip=v_a_d_skip, a_w_glu=v_a_w_glu, a_b_glu=v_a_b_glu, a_w_out=v_a_w_out, kv_norm=v_kv_norm, w_kv=v_w_kv, b_norm_pre=v_b_norm_pre, b_norm_post=v_b_norm_post, b_w_in=v_b_w_in, b_w_out=v_b_w_out, ple_w_proj=v_ple_w_proj, ple_w_gate=v_ple_w_gate)
    names = list(weights)

    t = x.shape[1]
    x0 = x[0]
    p0, p1 = p[0, 0], p[1, 0]
    tgt = loss_target[0]

    mats = dict(a_w_in=a_w_in[0], a_w_glu=a_w_glu[0], a_w_out=a_w_out[0], w_kv=w_kv, b_w_in=b_w_in[0],
                b_w_out=b_w_out[0], ple_w_proj=ple_w_proj.reshape(2 * 256, LANES),
                ple_w_gate=ple_w_gate.reshape(2 * LANES, D_MODEL))
    vec_names = ["a_norm_pre", "a_norm_post", "a_d_skip", "a_b_glu"]
    vec_pack = jnp.concatenate([weights[n] for n in vec_names], axis=0)
    gathered = _gather([m_.astype(BF16) for m_ in mats.values()] + [vec_pack], name="gather_weights")
    gw = dict(zip(mats, gathered[:-1]))
    vec_full = gathered[-1].transpose(1, 0, 2).reshape(len(vec_names), D_MODEL)
    g_apre, g_apost, d_skip, b_glu = (vec_full[i:i + 1] for i in range(4))
    w_u, w_ga = _blocks_to_cols(gw["a_w_in"][:4]), _blocks_to_cols(gw["a_w_in"][4:])
    w_glu = gw["a_w_glu"].reshape(D_MODEL, D_MODEL)
    w_out = gw["a_w_out"].reshape(D_MODEL, D_MODEL)
    w_k, w_v = _blocks_to_cols(gw["w_kv"][:4]), _blocks_to_cols(gw["w_kv"][4:])
    w_q, w_gb = _blocks_to_cols(gw["b_w_in"][:4]), _blocks_to_cols(gw["b_w_in"][4:])
    w_bout = gw["b_w_out"].reshape(D_MODEL, D_MODEL)
    wp = gw["ple_w_proj"].reshape(N_DEV, 2, 256, LANES)
    w_p0, w_p1 = _blocks_to_cols(wp[:, 0]), _blocks_to_cols(wp[:, 1])
    wg = gw["ple_w_gate"].reshape(N_DEV, 2, LANES, D_MODEL)
    w_g0, w_g1 = wg[:, 0].reshape(D_MODEL, D_MODEL), wg[:, 1].reshape(D_MODEL, D_MODEL)
    g_kv, g_bpre, g_bpost = kv_norm.reshape(1, D_MODEL), b_norm_pre, b_norm_post

    s5_params = (a_lam_re[0], a_lam_im[0], a_log_dt[0], a_b_re[0], a_b_im[0], a_c_re[0], a_c_im[0])
    (a_re, a_im, bbd, ccd), s5_vjp = jax.vjp(_s5_discretize, *s5_params)
    bbd_bf, ccd_bf = bbd.astype(BF16), ccd.astype(BF16)

    f_a2 = lambda ys, u, d: (_gelu(ys + d * u),)
    f_a3 = lambda g, s, ga, b: (g * _sigmoid(s + b) * _silu(ga),)
    f_ple = lambda xin, pg, pp: (xin + _sigmoid(pg) * pp,)
    f_res_rms = lambda xin, y, g: (xin + _rms(y, g),)
    f_rms = lambda y, g: (_rms(y, g),)
    f_k = lambda x2, gk, gp: (_rms(x2, gk), _rms(x2, gp))
    f_b3 = lambda o, gb: (o * _silu(gb),)

    (h0,) = _ew(f_rms, [x0], [g_apre], [BF16], name="a_norm_pre")
    u = _mm(h0, w_u, name="a_proj_u")
    ga = _mm(h0, w_ga, name="a_proj_gate")
    ys, hb = _s5_fwd(u, bbd_bf, ccd_bf, a_re, a_im)
    (g_act,) = _ew(f_a2, [ys, u], [d_skip], [BF16], name="a_gelu")
    s_glu = _mm(g_act, w_glu, name="a_glu")
    (y3,) = _ew(lambda ys_, u_, s_, ga_, d_, b_: f_a3(_gelu(ys_ + d_ * u_), s_, ga_, b_),
                [ys, u, s_glu, ga], [d_skip, b_glu], [BF16], name="a_gate")
    y4 = _mm(y3, w_out, name="a_out")
    (x1,) = _ew(f_res_rms, [x0, y4], [g_apost], [F32], name="a_norm_post")
    pg0 = _mm(x1, w_g0, name="ple0_gate")
    pp0 = _mm(p0, w_p0, name="ple0_proj")
    (x2,) = _ew(f_ple, [x1, pg0, pp0], [], [F32], name="ple0")
    kvn, h1 = _ew(f_k, [x2], [g_kv, g_bpre], [BF16, BF16], name="b_norms")
    k_ = _mm(kvn, w_k, out_dtype=BF16, name="kv_k")
    v_ = _mm(kvn, w_v, out_dtype=BF16, name="kv_v")
    q_ = _mm(h1, w_q, out_dtype=BF16, name="b_proj_q")
    gb = _mm(h1, w_gb, name="b_proj_gate")
    o, o_low = _attn_fwd(q_, k_, v_)
    (y5in,) = _ew(f_b3, [o, gb], [], [BF16], name="b_gate")
    y5 = _mm(y5in, w_bout, name="b_out")
    (x3,) = _ew(f_res_rms, [x2, y5], [g_bpost], [F32], name="b_norm_post")
    pg1 = _mm(x3, w_g1, name="ple1_gate")
    pp1 = _mm(p1, w_p1, name="ple1_proj")

    def f_loss(xin, pg, pp, tg):
        err = xin + _sigmoid(pg) * pp - tg
        return err * (1.0 / D_MODEL), (0.5 / D_MODEL) * jnp.sum(err * err).reshape(1, 1)

    dx4, loss_part = _ew(f_loss, [x3, pg1, pp1, tgt], [], [F32], n_red=1, name="ple1_loss")
    loss = lax.psum(loss_part[0, 0], ("x", "y", "c"))

    grads = {}
    f_ple_gate = lambda pg, pp: (_sigmoid(pg) * pp,)
    dpg1, dpp1 = _ew_bwd(f_ple_gate, [pg1, pp1], [], [dx4], [0, 1], [], [BF16, BF16], name="ple1_bwd")
    dx3 = _mm(dpg1, w_g1, tb=True, add=dx4, name="ple1_gate_dx")
    d_wg1 = _mm(x3, dpg1, ta=True, name="ple1_gate_dw")
    d_wp1 = _mm(p1, dpp1, ta=True, name="ple1_proj_dw")
    dy5, d_gbpost = _ew_bwd(f_rms, [y5], [g_bpost], [dx3], [0], [0], [BF16], name="b_norm_post_bwd")
    dy5in = _mm(dy5, w_bout, tb=True, name="b_out_dx")
    grads["b_w_out"] = _mm(y5in, dy5, ta=True, name="b_out_dw").reshape(N_DEV, LANES, D_MODEL)
    do, dgb = _ew_bwd(f_b3, [o, gb], [], [dy5in], [0, 1], [], [F32, BF16], name="b_gate_bwd")
    dq, dk, dv = _attn_bwd(q_, k_, v_, o, o_low, do)
    dh1 = _mm(dgb, w_gb, tb=True, add=_mm(dq, w_q, tb=True, name="b_proj_q_dx"), name="b_proj_gate_dx")
    grads["b_w_in"] = jnp.concatenate([_cols_to_blocks(_mm(h1, dq, ta=True, name="b_proj_q_dw"), 4),
                                       _cols_to_blocks(_mm(h1, dgb, ta=True, name="b_proj_gate_dw"), 4)])
    dkvn = _mm(dv, w_v, tb=True, add=_mm(dk, w_k, tb=True, name="kv_k_dx"), name="kv_v_dx")
    grads["w_kv"] = jnp.concatenate([_cols_to_blocks(_mm(kvn, dk, ta=True, name="kv_k_dw"), 4),
                                     _cols_to_blocks(_mm(kvn, dv, ta=True, name="kv_v_dw"), 4)])
    f_k3 = lambda x2_, gk, gp: f_k(x2_, gk, gp) + (x2_,)
    dx2, d_gkv, d_gbpre = _ew_bwd(f_k3, [x2], [g_kv, g_bpre], [dkvn, dh1, dx3], [0], [0, 1], [F32], name="b_norms_bwd")
    dpg0, dpp0 = _ew_bwd(f_ple_gate, [pg0, pp0], [], [dx2], [0, 1], [], [BF16, BF16], name="ple0_bwd")
    dx1 = _mm(dpg0, w_g0, tb=True, add=dx2, name="ple0_gate_dx")
    d_wg0 = _mm(x1, dpg0, ta=True, name="ple0_gate_dw")
    d_wp0 = _mm(p0, dpp0, ta=True, name="ple0_proj_dw")
    grads["ple_w_gate"] = jnp.stack([d_wg0.reshape(N_DEV, LANES, D_MODEL), d_wg1.reshape(N_DEV, LANES, D_MODEL)],
                                    axis=1).reshape(N_DEV, 2 * LANES, D_MODEL)
    grads["ple_w_proj"] = jnp.stack([_cols_to_blocks(d_wp0, N_DEV), _cols_to_blocks(d_wp1, N_DEV)],
                                    axis=1).reshape(N_DEV, 2 * 256, LANES)
    dy4, d_gapost = _ew_bwd(f_rms, [y4], [g_apost], [dx1], [0], [0], [BF16], name="a_norm_post_bwd")
    dy3 = _mm(dy4, w_out, tb=True, name="a_out_dx")
    grads["a_w_out"] = _mm(y3, dy4, ta=True, name="a_out_dw").reshape(N_DEV, LANES, D_MODEL)
    dg_direct, ds, dga, d_bglu = _a_gate_bwd(ys, u, s_glu, ga, d_skip, b_glu, dy3, f_a3)
    dg = _mm(ds, w_glu, tb=True, add=dg_direct, name="a_glu_dx")
    grads["a_w_glu"] = _mm(g_act, ds, ta=True, name="a_glu_dw").reshape(N_DEV, LANES, D_MODEL)
    dys, du_elem, d_dskip = _ew_bwd(f_a2, [ys, u], [d_skip], [dg], [0, 1], [0], [F32, F32], name="a_gelu_bwd")
    du, d_bbd, d_ccd, d_are, d_aim = _s5_bwd(u, dys, du_elem, hb, bbd_bf, ccd_bf, a_re, a_im)
    dh0 = _mm(dga, w_ga, tb=True, add=_mm(du, w_u, tb=True, name="a_proj_u_dx"), name="a_proj_gate_dx")
    grads["a_w_in"] = jnp.concatenate([_cols_to_blocks(_mm(h0, du, ta=True, name="a_proj_u_dw"), 4),
                                       _cols_to_blocks(_mm(h0, dga, ta=True, name="a_proj_gate_dw"), 4)])
    f_rms_res = lambda x0_, g: (_rms(x0_, g), x0_)
    grad_x, d_gapre = _ew_bwd(f_rms_res, [x0], [g_apre], [dh0, dx1], [0], [0], [F32], name="a_norm_pre_bwd")

    d_s5 = s5_vjp((d_are, d_aim, d_bbd, d_ccd))
    small = dict(zip(["a_lam_re", "a_lam_im", "a_log_dt", "a_b_re", "a_b_im", "a_c_re", "a_c_im"], d_s5))
    small.update(kv_norm=d_gkv, b_norm_pre=d_gbpre, b_norm_post=d_gbpost)
    small_names = list(small)
    small_flat = jnp.concatenate([small[n].reshape(-1) for n in small_names])
    small_pack = jnp.pad(small_flat, (0, -small_flat.size % (8 * LANES))).reshape(-1, LANES)
    vec_grads = jnp.concatenate([d_gapre, d_gapost, d_dskip, d_bglu], axis=0)
    vec_scatter = vec_grads.reshape(len(vec_names), N_DEV, LANES).transpose(1, 0, 2)
    mat_names = list(mats)
    scattered = _reduce_scatter([grads[n] for n in mat_names] + [vec_scatter], mat_names + ["vectors"])
    (small_all,) = _gather([small_pack], name="gather_small_grads")

    out_g, out_d, out_m, out_v = {}, {}, {}, {}

    def update(n, parts):
        shape = weights[n].shape
        rc = parts.shape[1:]
        res = _adamw(parts, weights[n].reshape(rc), mom_m[n].reshape(rc), mom_v[n].reshape(rc), name="adamw_" + n)
        out_g[n], out_d[n], out_m[n], out_v[n] = (r.reshape(shape) for r in res)

    for n, parts in zip(mat_names, scattered[:-1]):
        update(n, parts)
    for i, n in enumerate(vec_names):
        update(n, scattered[-1][:, i:i + 1, :])
    off = 0
    for n in small_names:
        size = small[n].size
        rows = size // LANES if size % LANES == 0 and size >= LANES else 1
        seg = small_all.reshape(N_DEV, -1)[:, off:off + size].reshape(N_DEV, rows, size // rows)
        update(n, seg)
        off += size

    return (loss, grad_x[None], *[out_g[n] for n in names], *[out_d[n] for n in names],
            *[out_m[n] for n in names], *[out_v[n] for n in names])


def _a_gate_bwd(ys, u, s_glu, ga, d_skip, b_glu, dy3, f_a3):
    def fn(ys_, u_, s_, ga_, ct, d_, b_):
        g = _gelu(ys_ + d_ * u_)
        _, vjp = jax.vjp(lambda g_, s__, ga__, b__: f_a3(g_, s__, ga__, b__)[0], g, s_, ga_, b_)
        dg, ds, dga, db = vjp(ct)
        return dg, ds, dga, db

    return _ew(fn, [ys, u, s_glu, ga, dy3], [d_skip, b_glu], [F32, BF16, BF16], n_red=1, name="a_gate_bwd")
```

```python
import functools
import math

import jax
import jax.numpy as jnp
from jax import lax
from jax.experimental import pallas as pl
from jax.experimental.pallas import tpu as pltpu

F32 = jnp.float32
BF16 = jnp.bfloat16

N_DEV = 8
D_MODEL = 1024
GROUP_SIZE = 16
N_GROUPS = D_MODEL // GROUP_SIZE
STATE = 64
HEAD_DIM = 64
EPS = 1e-6
LANES = 128
GROUPS_PER_BLOCK = LANES // GROUP_SIZE
N_GBLK = N_GROUPS // GROUPS_PER_BLOCK
STATE_COLS = 2 * GROUPS_PER_BLOCK * STATE
N_SLAB = STATE_COLS // LANES
VMEM_LIMIT = 56 * 1024 * 1024
EXP_UNDERFLOW = -104.0
ATTN_LANES = 256
MM_TILE = 1024
EW_ROWS_BYTES = 24 * 1024 * 1024

ADAM_LR = 0.001
ADAM_B1 = 0.9
ADAM_B2 = 0.999
ADAM_EPS = 1e-08
ADAM_WD = 0.01
ADAM_STEP = 10


def _pallas(body, **kw):
    return pl.pallas_call(body, **kw)


def _cparams(*sem):
    return pltpu.CompilerParams(dimension_semantics=sem, vmem_limit_bytes=VMEM_LIMIT)


def _mm(a, b, *, ta=False, tb=False, add=None, out_dtype=F32, name):
    m, k = (a.shape[1], a.shape[0]) if ta else a.shape
    n = b.shape[0] if tb else b.shape[1]
    tm, tn, tk = min(m, MM_TILE), min(n, MM_TILE), min(k, MM_TILE)
    nk = k // tk
    grid = (m // tm, n // tn, nk)
    a_spec = pl.BlockSpec((tk, tm), lambda i, j, l: (l, i)) if ta else pl.BlockSpec((tm, tk), lambda i, j, l: (i, l))
    b_spec = pl.BlockSpec((tn, tk), lambda i, j, l: (j, l)) if tb else pl.BlockSpec((tk, tn), lambda i, j, l: (l, j))
    o_spec = pl.BlockSpec((tm, tn), lambda i, j, l: (i, j))
    dims = (((0 if ta else 1,), (1 if tb else 0,)), ((), ()))
    has_add = add is not None
    assert nk == 1 or (out_dtype == F32 and not has_add)

    def body(*refs):
        a_ref, b_ref, o_ref = refs[0], refs[1], refs[-1]
        part = lax.dot_general(a_ref[...].astype(BF16), b_ref[...].astype(BF16), dims, preferred_element_type=F32)
        if nk == 1:
            if has_add:
                part = part + refs[2][...].astype(F32)
            o_ref[...] = part.astype(out_dtype)
        else:
            l = pl.program_id(2)

            @pl.when(l == 0)
            def _():
                o_ref[...] = part

            @pl.when(l > 0)
            def _():
                o_ref[...] += part

    ins = [a, b] + ([add] if has_add else [])
    specs = [a_spec, b_spec] + ([o_spec] if has_add else [])
    return _pallas(
        body, name=name, grid=grid, in_specs=specs, out_specs=o_spec,
        out_shape=jax.ShapeDtypeStruct((m, n), out_dtype),
        compiler_params=_cparams("parallel", "parallel", "arbitrary"),
    )(*ins)


def _row_spec(arr, tm):
    return pl.BlockSpec((tm, arr.shape[1]), lambda i: (i, 0))


def _full_spec(arr):
    return pl.BlockSpec(arr.shape, lambda i: (0,) * arr.ndim)


def _ew_tile_rows(t, row_bytes):
    tm = 1024
    while tm > 8 and (t % tm or 2 * tm * row_bytes > EW_ROWS_BYTES):
        tm //= 2
    return tm


def _ew(fn, rows, vecs, out_dtypes, *, mats=(), n_red=0, name):
    t = rows[0].shape[0]
    nr, nv, nm = len(rows), len(vecs), len(mats)

    def tile_shapes(tm):
        return jax.eval_shape(fn, *[jax.ShapeDtypeStruct((tm, r.shape[1]), F32) for r in rows],
                              *[jax.ShapeDtypeStruct(v.shape, F32) for v in vecs],
                              *[jax.ShapeDtypeStruct(m_.shape, m_.dtype) for m_ in mats])

    probe = tile_shapes(8)
    row_bytes = sum(r.shape[1] * r.dtype.itemsize for r in rows)
    row_bytes += sum(s.shape[1] * jnp.dtype(dt).itemsize for s, dt in zip(probe, out_dtypes))
    tm = _ew_tile_rows(t, row_bytes)
    shapes = tile_shapes(tm)
    n_out = len(shapes) - n_red

    def body(*refs):
        ins = [r[...].astype(F32) for r in refs[:nr + nv]] + [r[...] for r in refs[nr + nv:nr + nv + nm]]
        outs = fn(*ins)
        o_refs = refs[nr + nv + nm:]
        for o_ref, o in zip(o_refs[:n_out], outs[:n_out]):
            o_ref[...] = o.astype(o_ref.dtype)
        i = pl.program_id(0)
        for o_ref, o in zip(o_refs[n_out:], outs[n_out:]):
            @pl.when(i == 0)
            def _(o_ref=o_ref, o=o):
                o_ref[...] = o

            @pl.when(i > 0)
            def _(o_ref=o_ref, o=o):
                o_ref[...] += o

    out_shape = [jax.ShapeDtypeStruct((t, s.shape[1]), dt) for s, dt in zip(shapes[:n_out], out_dtypes)]
    out_shape += [jax.ShapeDtypeStruct(s.shape, F32) for s in shapes[n_out:]]
    out_specs = [pl.BlockSpec((tm, s.shape[1]), lambda i: (i, 0)) for s in shapes[:n_out]]
    out_specs += [pl.BlockSpec(s.shape, lambda i: (0, 0)) for s in shapes[n_out:]]
    return _pallas(
        body, name=name, grid=(t // tm,),
        in_specs=[_row_spec(r, tm) for r in rows] + [_full_spec(v) for v in list(vecs) + list(mats)],
        out_specs=out_specs, out_shape=out_shape,
        compiler_params=_cparams("arbitrary" if n_red else "parallel"),
    )(*rows, *vecs, *mats)


def _dot(a, w):
    return jnp.dot(a.astype(BF16), w, preferred_element_type=F32)


def _dot_t(a, w):
    return lax.dot_general(a.astype(BF16), w, (((1,), (1,)), ((), ())), preferred_element_type=F32)


def _ew_bwd(fn, rows, vecs, cts, row_wrt, vec_wrt, out_dtypes, *, name):
    nr, nv, nc = len(rows), len(vecs), len(cts)

    def grad_fn(*tiles):
        ins = list(tiles[:nr + nv])
        ct = tiles[nr + nv:]
        wrt = list(row_wrt) + [nr + j for j in vec_wrt]

        def f(*sel):
            full = list(ins)
            for idx, s in zip(wrt, sel):
                full[idx] = s
            return tuple(fn(*full))

        _, vjp = jax.vjp(f, *[ins[idx] for idx in wrt])
        return vjp(tuple(ct))

    return _ew_reordered(grad_fn, rows, vecs, cts, out_dtypes, len(vec_wrt), name)


def _ew_reordered(grad_fn, rows, vecs, cts, out_dtypes, n_red, name):
    nr, nv = len(rows), len(vecs)

    def fn(*tiles):
        r, c, v = tiles[:nr], tiles[nr:nr + len(cts)], tiles[nr + len(cts):]
        return grad_fn(*r, *v, *c)

    return _ew(fn, list(rows) + list(cts), vecs, out_dtypes, n_red=n_red, name=name)


def _rms(x, g):
    return x * lax.rsqrt(jnp.mean(x * x, axis=-1, keepdims=True) + EPS) * g


def _sigmoid(x):
    return 1.0 / (1.0 + jnp.exp(-x))


def _silu(x):
    return x * _sigmoid(x)


def _gelu(x):
    return 0.5 * x * (1.0 + jnp.tanh(math.sqrt(2.0 / math.pi) * (x + 0.044715 * x * x * x)))


def _token_rows(t):
    return pl.ds(pl.multiple_of(t * N_GBLK, N_GBLK), N_GBLK)


def _block_rows(gb, tt):
    return pl.ds(gb, tt, stride=N_GBLK)


def _s5_expand(u_bf, bbd_ref, xs_ref, tt):
    for gb in range(N_GBLK):
        xg = jnp.dot(u_bf[:, gb * LANES:(gb + 1) * LANES], bbd_ref[gb], preferred_element_type=F32)
        for l in range(N_SLAB):
            xs_ref[l, _block_rows(gb, tt), :] = xg[:, l * LANES:(l + 1) * LANES]


def _s5_forward_scan(xs_ref, ar, ai, state, tt):
    half = N_SLAB // 2

    def step(t, st):
        new = [None] * N_SLAB
        for l in range(half):
            sr, si = st[l], st[half + l]
            rows = _token_rows(t)
            nr_ = ar[l] * sr - ai[l] * si + xs_ref[l, rows, :]
            ni_ = ar[l] * si + ai[l] * sr + xs_ref[half + l, rows, :]
            xs_ref[l, rows, :] = nr_
            xs_ref[half + l, rows, :] = ni_
            new[l], new[half + l] = nr_, ni_
        return tuple(new)

    return lax.fori_loop(0, tt, step, tuple(state), unroll=4)


def _slab_rows(ref, gb, tt):
    return jnp.concatenate([ref[l, _block_rows(gb, tt), :] for l in range(N_SLAB)], axis=1)


def _s5_fwd(u, bbd, ccd, a_re, a_im, *, tt=256):
    t = u.shape[0]
    nt = t // tt
    half = N_SLAB // 2

    def body(u_ref, bbd_ref, ccd_ref, ar_ref, ai_ref, y_ref, hb_ref, xs_ref, st_ref):
        i = pl.program_id(0)

        @pl.when(i == 0)
        def _():
            st_ref[...] = jnp.zeros_like(st_ref)

        hb_ref[0] = st_ref[...]
        _s5_expand(u_ref[...].astype(BF16), bbd_ref, xs_ref, tt)
        ar = [ar_ref[l] for l in range(half)]
        ai = [ai_ref[l] for l in range(half)]
        last = _s5_forward_scan(xs_ref, ar, ai, [st_ref[l] for l in range(N_SLAB)], tt)
        for l in range(N_SLAB):
            st_ref[l] = last[l]
        for gb in range(N_GBLK):
            sg = _slab_rows(xs_ref, gb, tt).astype(BF16)
            y_ref[:, gb * LANES:(gb + 1) * LANES] = jnp.dot(sg, ccd_ref[gb], preferred_element_type=F32)

    return _pallas(
        body, name="s5_fwd", grid=(nt,),
        in_specs=[pl.BlockSpec((tt, D_MODEL), lambda i: (i, 0)), _full_spec(bbd), _full_spec(ccd),
                  _full_spec(a_re), _full_spec(a_im)],
        out_specs=[pl.BlockSpec((tt, D_MODEL), lambda i: (i, 0)),
                   pl.BlockSpec((1, N_SLAB, N_GBLK, LANES), lambda i: (i, 0, 0, 0))],
        out_shape=[jax.ShapeDtypeStruct((t, D_MODEL), F32), jax.ShapeDtypeStruct((nt, N_SLAB, N_GBLK, LANES), F32)],
        scratch_shapes=[pltpu.VMEM((N_SLAB, N_GBLK * tt, LANES), F32), pltpu.VMEM((N_SLAB, N_GBLK, LANES), F32)],
        compiler_params=_cparams("arbitrary"),
    )(u, bbd, ccd, a_re, a_im)


def _s5_bwd(u, dy, du_add, hb, bbd, ccd, a_re, a_im, *, tt=256):
    t = u.shape[0]
    nt = t // tt
    half = N_SLAB // 2
    rev = lambda i: (nt - 1 - i, 0)

    def body(u_ref, dy_ref, dua_ref, hb_ref, bbd_ref, ccd_ref, ar_ref, ai_ref,
             du_ref, dbbd_ref, dccd_ref, dar_ref, dai_ref, xs_ref, es_ref, lam_ref):
        i = pl.program_id(0)

        @pl.when(i == 0)
        def _():
            lam_ref[...] = jnp.zeros_like(lam_ref)
            dbbd_ref[...] = jnp.zeros_like(dbbd_ref)
            dccd_ref[...] = jnp.zeros_like(dccd_ref)
            dar_ref[...] = jnp.zeros_like(dar_ref)
            dai_ref[...] = jnp.zeros_like(dai_ref)

        u_bf = u_ref[...].astype(BF16)
        dy_bf = dy_ref[...].astype(BF16)
        _s5_expand(u_bf, bbd_ref, xs_ref, tt)
        ar = [ar_ref[l] for l in range(half)]
        ai = [ai_ref[l] for l in range(half)]
        start = [hb_ref[0, l] for l in range(N_SLAB)]
        _s5_forward_scan(xs_ref, ar, ai, start, tt)
        for gb in range(N_GBLK):
            eg = lax.dot_general(dy_bf[:, gb * LANES:(gb + 1) * LANES], ccd_ref[gb], (((1,), (1,)), ((), ())),
                                 preferred_element_type=F32)
            for l in range(N_SLAB):
                es_ref[l, _block_rows(gb, tt), :] = eg[:, l * LANES:(l + 1) * LANES]

        def lam_step(t_, lam, prev, acc):
            new_lam, new_acc = [None] * N_SLAB, [None] * N_SLAB
            rows = _token_rows(t_)
            for l in range(half):
                lr = es_ref[l, rows, :] + ar[l] * lam[l] + ai[l] * lam[half + l]
                li = es_ref[half + l, rows, :] - ai[l] * lam[l] + ar[l] * lam[half + l]
                es_ref[l, rows, :] = lr
                es_ref[half + l, rows, :] = li
                pr, pi = prev(l), prev(half + l)
                new_acc[l] = acc[l] + lr * pr + li * pi
                new_acc[half + l] = acc[half + l] - lr * pi + li * pr
                new_lam[l], new_lam[half + l] = lr, li
            return tuple(new_lam), tuple(new_acc)

        def loop_body(k, carry):
            lam, acc = carry
            t_ = tt - 1 - k
            return lam_step(t_, lam, lambda l: xs_ref[l, _token_rows(t_ - 1), :], acc)

        zero = tuple(jnp.zeros((N_GBLK, LANES), F32) for _ in range(N_SLAB))
        lam0 = tuple(lam_ref[l] for l in range(N_SLAB))
        lam, acc = lax.fori_loop(0, tt - 1, loop_body, (lam0, zero), unroll=2)
        lam, acc = lam_step(0, lam, lambda l: start[l], acc)
        for l in range(N_SLAB):
            lam_ref[l] = lam[l]
        for l in range(half):
            dar_ref[l] += acc[l]
            dai_ref[l] += acc[half + l]

        for gb in range(N_GBLK):
            cols = slice(gb * LANES, (gb + 1) * LANES)
            lg = _slab_rows(es_ref, gb, tt).astype(BF16)
            sg = _slab_rows(xs_ref, gb, tt).astype(BF16)
            du_ref[:, cols] = (dua_ref[:, cols] + lax.dot_general(
                lg, bbd_ref[gb], (((1,), (1,)), ((), ())), preferred_element_type=F32)).astype(du_ref.dtype)
            dbbd_ref[gb] += lax.dot_general(u_bf[:, cols], lg, (((0,), (0,)), ((), ())), preferred_element_type=F32)
            dccd_ref[gb] += lax.dot_general(sg, dy_bf[:, cols], (((0,), (0,)), ((), ())), preferred_element_type=F32)

    row = pl.BlockSpec((tt, D_MODEL), rev)
    small = pl.BlockSpec((half, N_GBLK, LANES), lambda i: (0, 0, 0))
    return _pallas(
        body, name="s5_bwd", grid=(nt,),
        in_specs=[row, row, row, pl.BlockSpec((1, N_SLAB, N_GBLK, LANES), lambda i: (nt - 1 - i, 0, 0, 0)),
                  _full_spec(bbd), _full_spec(ccd), _full_spec(a_re), _full_spec(a_im)],
        out_specs=[row, _full_spec(bbd), _full_spec(ccd), small, small],
        out_shape=[jax.ShapeDtypeStruct((t, D_MODEL), BF16), jax.ShapeDtypeStruct(bbd.shape, F32),
                   jax.ShapeDtypeStruct(ccd.shape, F32), jax.ShapeDtypeStruct(a_re.shape, F32),
                   jax.ShapeDtypeStruct(a_im.shape, F32)],
        scratch_shapes=[pltpu.VMEM((N_SLAB, N_GBLK * tt, LANES), F32), pltpu.VMEM((N_SLAB, N_GBLK * tt, LANES), F32),
                        pltpu.VMEM((N_SLAB, N_GBLK, LANES), F32)],
        compiler_params=_cparams("arbitrary"),
    )(u, dy, du_add, hb, bbd, ccd, a_re, a_im)


def _s5_discretize(lam_re, lam_im, log_dt, b_re, b_im, c_re, c_im):
    lr = jnp.minimum(lam_re, -1e-4)
    li = lam_im
    dt = jnp.exp(log_dt)[:, None]
    mag = jnp.exp(lr * dt)
    a_re = mag * jnp.cos(li * dt)
    a_im = mag * jnp.sin(li * dt)
    den = lr * lr + li * li
    nr = a_re - 1.0
    f_re = (nr * lr + a_im * li) / den
    f_im = (a_im * lr - nr * li) / den
    bb_re = f_re[..., None] * b_re - f_im[..., None] * b_im
    bb_im = f_re[..., None] * b_im + f_im[..., None] * b_re
    eye = jnp.eye(GROUPS_PER_BLOCK, dtype=F32)
    bb = jnp.stack([bb_re, bb_im]).reshape(2, N_GBLK, GROUPS_PER_BLOCK, STATE, GROUP_SIZE)
    bbd = jnp.einsum('rgjph,jk->gjhrkp', bb, eye).reshape(N_GBLK, LANES, STATE_COLS)
    cc = jnp.stack([c_re, -c_im]).reshape(2, N_GBLK, GROUPS_PER_BLOCK, GROUP_SIZE, STATE)
    ccd = jnp.einsum('rgjhp,jk->grjpkh', cc, eye).reshape(N_GBLK, STATE_COLS, LANES)

    def dense(a):
        return a.reshape(N_GBLK, N_SLAB // 2, LANES).transpose(1, 0, 2)

    return dense(a_re), dense(a_im), bbd, ccd


def _attn_tiles(tq):
    row = lax.broadcasted_iota(jnp.int32, (tq, tq), 0)
    col = lax.broadcasted_iota(jnp.int32, (tq, tq), 1)
    return col < row, (row > col).astype(BF16), (row >= col).astype(BF16)


def _split_bf16(x):
    hi = x.astype(BF16)
    return hi, (x - hi.astype(F32)).astype(BF16)


def _attn_scores(qs, kj, cs, diag, later, masked):
    zs = [lax.dot_general(qa, kj, (((1,), (1,)), ((), ())), preferred_element_type=F32) * (HEAD_DIM ** -0.5) for qa in qs]
    lks = [-(jnp.maximum(z, 0.0) + jnp.log(1.0 + jnp.exp(-jnp.abs(z)))) for z in zs]
    if masked:
        lks = [jnp.where(diag, lk, 0.0) for lk in lks]
    parts = [_split_bf16(lk) for lk in lks]
    tails = [jnp.dot(hi, later, preferred_element_type=F32) + jnp.dot(lo, later, preferred_element_type=F32)
             for hi, lo in parts]
    ws = [jnp.exp(z + lk + tail + c) for z, lk, tail, c in zip(zs, lks, tails, cs)]
    if masked:
        ws = [jnp.where(diag, w, 0.0) for w in ws]
    return zs, lks, ws


def _head_masks(width=LANES):
    lane = lax.broadcasted_iota(jnp.int32, (1, width), 1)
    return [(lane >= a * HEAD_DIM) & (lane < (a + 1) * HEAD_DIM) for a in range(width // HEAD_DIM)]


def _any_alive(cs):
    alive = jnp.max(cs[0]) > EXP_UNDERFLOW
    for c in cs[1:]:
        alive = alive | (jnp.max(c) > EXP_UNDERFLOW)
    return alive


def _attn_fwd(q, k, v, *, tq=256):
    t = q.shape[0]
    tq = min(tq, t)
    width = ATTN_LANES

    def body(q_ref, k_ref, v_ref, o_ref, ol_ref):
        i = pl.program_id(1)
        diag, later, _ = _attn_tiles(tq)
        masks = _head_masks(width)
        q2 = q_ref[...]
        qs = [jnp.where(hm, q2, jnp.zeros_like(q2)) for hm in masks]

        def block(j, state, masked):
            rows = pl.ds(pl.multiple_of(j * tq, tq), tq)
            kj, vj = k_ref[rows, :], v_ref[rows, :]
            _, lks, ws = _attn_scores(qs, kj, [h[0] for h in state], diag, later, masked)
            parts = [_split_bf16(w) for w in ws]
            return tuple((c + jnp.sum(lk, axis=1, keepdims=True),
                          acc + jnp.dot(w_hi, vj, preferred_element_type=F32),
                          low + jnp.dot(w_lo, vj, preferred_element_type=F32))
                         for (c, acc, low), lk, (w_hi, w_lo) in zip(state, lks, parts))

        zero = jnp.zeros((tq, width), F32)
        state = block(i, tuple((jnp.zeros((tq, 1), F32), zero, zero) for _ in masks), True)

        def cond(st):
            return (st[0] >= 0) & _any_alive([h[0] for h in st[1]])

        def step(st):
            return st[0] - 1, block(st[0], st[1], False)

        _, state = lax.while_loop(cond, step, (i - 1, state))
        o_ref[...] = sum(jnp.where(hm, h[1], 0.0) for hm, h in zip(masks, state))
        ol_ref[...] = sum(jnp.where(hm, h[2], 0.0) for hm, h in zip(masks, state))

    qspec = pl.BlockSpec((tq, width), lambda h, i: (i, h))
    kspec = pl.BlockSpec((t, width), lambda h, i: (0, h))
    full = jax.ShapeDtypeStruct((t, D_MODEL), F32)
    return _pallas(
        body, name="attn_fwd", grid=(D_MODEL // width, t // tq), in_specs=[qspec, kspec, kspec], out_specs=[qspec, qspec],
        out_shape=[full, full], compiler_params=_cparams("parallel", "arbitrary"),
    )(q, k, v)


def _attn_bwd(q, k, v, o, o_low, do, *, tq=256):
    t = q.shape[0]
    tq = min(tq, t)
    width = ATTN_LANES

    def body(q_ref, k_ref, v_ref, o_ref, ol_ref, do_ref, dq_ref, dk_ref, dv_ref):
        i = pl.program_id(1)

        @pl.when(i == 0)
        def _():
            dk_ref[...] = jnp.zeros_like(dk_ref)
            dv_ref[...] = jnp.zeros_like(dv_ref)

        diag, later, later_eq = _attn_tiles(tq)
        masks = _head_masks(width)
        q2 = q_ref[...]
        do2 = do_ref[...]
        dd = do2.astype(BF16).astype(F32) * (o_ref[...] + ol_ref[...])
        qs = [jnp.where(hm, q2, jnp.zeros_like(q2)) for hm in masks]
        dos = [jnp.where(hm, do2, 0.0).astype(BF16) for hm in masks]
        totals = [jnp.sum(jnp.where(hm, dd, 0.0), axis=1, keepdims=True) for hm in masks]

        def block(j, state, masked):
            rows = pl.ds(pl.multiple_of(j * tq, tq), tq)
            kj, vj = k_ref[rows, :], v_ref[rows, :]
            nt = (((1,), (1,)), ((), ()))
            tn = (((0,), (0,)), ((), ()))
            zs, lks, ws = _attn_scores(qs, kj, [h[0] for h in state], diag, later, masked)
            dws = [lax.dot_general(doa, vj, nt, preferred_element_type=F32) for doa in dos]
            gs = [dw * w for dw, w in zip(dws, ws)]
            parts = [_split_bf16(g) for g in gs]
            from_here = [jnp.dot(hi, later_eq, preferred_element_type=F32) + jnp.dot(lo, later_eq, preferred_element_type=F32)
                         for hi, lo in parts]
            dzs = []
            for z, lk, g, fh, total, h in zip(zs, lks, gs, from_here, totals, state):
                beta = jnp.exp(z + lk)
                dz = g * (1.0 - beta) - beta * (total - h[1] - fh)
                if masked:
                    dz = jnp.where(diag, dz, 0.0)
                dzs.append((dz * (HEAD_DIM ** -0.5)).astype(BF16))
            dk = sum(lax.dot_general(dz, qa, tn, preferred_element_type=F32) for dz, qa in zip(dzs, qs))
            dv = sum(lax.dot_general(w.astype(BF16), doa, tn, preferred_element_type=F32) for w, doa in zip(ws, dos))
            new = [(c + jnp.sum(lk, axis=1, keepdims=True), r + jnp.sum(g, axis=1, keepdims=True),
                    acc + jnp.dot(dz, kj, preferred_element_type=F32))
                   for (c, r, acc), lk, g, dz in zip(state, lks, gs, dzs)]
            dk_ref[rows, :] += dk
            dv_ref[rows, :] += dv
            return tuple(new)

        zc = jnp.zeros((tq, 1), F32)
        state = block(i, tuple((zc, zc, jnp.zeros((tq, width), F32)) for _ in masks), True)

        def cond(st):
            return (st[0] >= 0) & _any_alive([h[0] for h in st[1]])

        def step(st):
            return st[0] - 1, block(st[0], st[1], False)

        _, state = lax.while_loop(cond, step, (i - 1, state))
        dq_ref[...] = sum(jnp.where(hm, h[2], 0.0) for hm, h in zip(masks, state))

    qspec = pl.BlockSpec((tq, width), lambda h, i: (i, h))
    kspec = pl.BlockSpec((t, width), lambda h, i: (0, h))
    kin = pl.BlockSpec((t, width), lambda h, i: (0, h), pipeline_mode=pl.Buffered(1))
    full = jax.ShapeDtypeStruct((t, D_MODEL), F32)
    return _pallas(
        body, name="attn_bwd", grid=(D_MODEL // width, t // tq), in_specs=[qspec, kin, kin, qspec, qspec, qspec],
        out_specs=[qspec, kspec, kspec], out_shape=[full, full, full],
        compiler_params=_cparams("parallel", "arbitrary"),
    )(q, k, v, o, o_low, do)


N_CHIP = 4
MESH_ID = pl.DeviceIdType.MESH
ANY_SPEC = pl.BlockSpec(memory_space=pl.ANY)


def _position():
    return lax.axis_index("x"), lax.axis_index("y"), lax.axis_index("c")


def _other_chips(x, y):
    return [(1 - x, y), (x, 1 - y), (1 - x, 1 - y)]


def _gather(arrs, *, name):
    n = len(arrs)

    def body(*refs):
        ins, outs = refs[:n], refs[n:2 * n]
        send_sems, recv_sems, local_sems = refs[2 * n:]
        x, y, c = _position()
        me, sibling = (x, y, c), (x, y, 1 - c)
        chips = _other_chips(x, y)

        def copy(a, k, block, to, src=None):
            slot = outs[a].at[4 * block[0] + 2 * block[1] + block[2]]
            return pltpu.make_async_remote_copy(
                src_ref=slot if src is None else src, dst_ref=slot, send_sem=send_sems.at[a, k],
                recv_sem=recv_sems.at[a, k], device_id=to, device_id_type=MESH_ID)

        local = [pltpu.make_async_copy(ins[a], outs[a].at[4 * x + 2 * y + c], local_sems.at[a]) for a in range(n)]
        first = []
        for a in range(n):
            local[a].start()
            first.append(copy(a, 0, me, sibling, src=ins[a]))
            first += [copy(a, 1 + j, me, (*chip, c), src=ins[a]) for j, chip in enumerate(chips)]
        for cp in first:
            cp.start()
        passed = []
        for a in range(n):
            for j, chip in enumerate(chips):
                copy(a, 1 + j, (*chip, c), me).wait_recv()
                passed.append(copy(a, 4 + j, (*chip, c), sibling))
                passed[-1].start()
        for a in range(n):
            copy(a, 0, sibling, me).wait_recv()
            for j, chip in enumerate(chips):
                copy(a, 4 + j, (*chip, 1 - c), me).wait_recv()
        for cp in first + passed:
            cp.wait_send()
        for cp in local:
            cp.wait()

    return _pallas(
        body, name=name, in_specs=[ANY_SPEC] * n, out_specs=[ANY_SPEC] * n,
        out_shape=[jax.ShapeDtypeStruct((N_DEV,) + a.shape, a.dtype) for a in arrs],
        scratch_shapes=[pltpu.SemaphoreType.DMA((n, N_DEV - 1)), pltpu.SemaphoreType.DMA((n, N_DEV - 1)),
                        pltpu.SemaphoreType.DMA((n,))],
    )(*arrs)


def _pair_exchange(arrs, *, name):
    n = len(arrs)

    def body(*refs):
        ins, got = refs[:n], refs[n:2 * n]
        send_sems, recv_sems = refs[2 * n:]
        x, y, c = _position()
        sends = []
        for a in range(n):
            for ch in range(N_CHIP):
                sends.append(pltpu.make_async_remote_copy(
                    src_ref=ins[a].at[2 * ch + (1 - c)], dst_ref=got[a].at[ch], send_sem=send_sems.at[a, ch],
                    recv_sem=recv_sems.at[a, ch], device_id=(x, y, 1 - c), device_id_type=MESH_ID))
        for cp in sends:
            cp.start()
        for cp in sends:
            cp.wait_send()
            cp.wait_recv()

    return _pallas(
        body, name=name, in_specs=[ANY_SPEC] * n, out_specs=[ANY_SPEC] * n,
        out_shape=[jax.ShapeDtypeStruct((N_CHIP,) + a.shape[1:], a.dtype) for a in arrs],
        scratch_shapes=[pltpu.SemaphoreType.DMA((n, N_CHIP)), pltpu.SemaphoreType.DMA((n, N_CHIP))],
    )(*arrs)


def _chip_exchange(arrs, *, name):
    n = len(arrs)

    def body(*refs):
        ins, outs = refs[:n], refs[n:2 * n]
        send_sems, recv_sems, local_sems = refs[2 * n:]
        x, y, c = _position()
        mine = 2 * x + y
        chips = _other_chips(x, y)
        sends, recvs, locals_ = [], [], []
        for a in range(n):
            locals_.append(pltpu.make_async_copy(ins[a].at[mine], outs[a].at[mine], local_sems.at[a]))
            for j, (px, py) in enumerate(chips):
                sends.append(pltpu.make_async_remote_copy(
                    src_ref=ins[a].at[2 * px + py], dst_ref=outs[a].at[mine], send_sem=send_sems.at[a, j],
                    recv_sem=recv_sems.at[a, j], device_id=(px, py, c), device_id_type=MESH_ID))
                recvs.append(pltpu.make_async_remote_copy(
                    src_ref=ins[a].at[2 * px + py], dst_ref=outs[a].at[2 * px + py], send_sem=send_sems.at[a, j],
                    recv_sem=recv_sems.at[a, j], device_id=(px, py, c), device_id_type=MESH_ID))
        for cp in locals_ + sends:
            cp.start()
        for cp in sends:
            cp.wait_send()
        for cp in recvs:
            cp.wait_recv()
        for cp in locals_:
            cp.wait()

    return _pallas(
        body, name=name, in_specs=[ANY_SPEC] * n, out_specs=[ANY_SPEC] * n,
        out_shape=[jax.ShapeDtypeStruct(a.shape, a.dtype) for a in arrs],
        scratch_shapes=[pltpu.SemaphoreType.DMA((n, N_CHIP - 1)), pltpu.SemaphoreType.DMA((n, N_CHIP - 1)),
                        pltpu.SemaphoreType.DMA((n,))],
    )(*arrs)


def _pair_sum(mine, got, core, *, name):
    s_, r, c = got.shape
    tr = r if r <= 256 else 256

    def body(core_ref, a_ref, b_ref, o_ref):
        o_ref[...] = a_ref[...] + b_ref[...]

    quarter = pl.BlockSpec((s_, tr, c), lambda i, core_ref: (0, i, 0))
    return _pallas(
        body, name=name, out_shape=jax.ShapeDtypeStruct(got.shape, got.dtype),
        grid_spec=pltpu.PrefetchScalarGridSpec(
            num_scalar_prefetch=1, grid=(r // tr,),
            in_specs=[pl.BlockSpec((s_, None, tr, c), lambda i, core_ref: (0, core_ref[0], i, 0)), quarter],
            out_specs=quarter),
        compiler_params=_cparams("parallel"),
    )(core, mine.reshape(s_, 2, r, c), got)


def _reduce_scatter(arrs, names):
    got = _pair_exchange(arrs, name="pair_exchange")
    core = lax.axis_index("c").astype(jnp.int32).reshape(1)
    sums = [_pair_sum(a, g_, core, name="pair_sum_" + n) for a, g_, n in zip(arrs, got, names)]
    return _chip_exchange(sums, name="chip_exchange")


def _adamw(parts, w, m, v, *, name):
    s, r, c = parts.shape
    tr = r if r * c <= 256 * D_MODEL else 256

    def body(p_ref, w_ref, m_ref, v_ref, g_ref, d_ref, nm_ref, nv_ref):
        g = p_ref[0]
        for j in range(1, s):
            g = g + p_ref[j]
        nm = ADAM_B1 * m_ref[...] + (1.0 - ADAM_B1) * g
        nv = ADAM_B2 * v_ref[...] + (1.0 - ADAM_B2) * (g * g)
        m_hat = nm / (1.0 - ADAM_B1 ** ADAM_STEP)
        v_hat = nv / (1.0 - ADAM_B2 ** ADAM_STEP)
        g_ref[...] = g
        d_ref[...] = -ADAM_LR * (m_hat / (jnp.sqrt(v_hat) + ADAM_EPS) + ADAM_WD * w_ref[...])
        nm_ref[...] = nm
        nv_ref[...] = nv

    spec = pl.BlockSpec((tr, c), lambda i: (i, 0))
    one = jax.ShapeDtypeStruct((r, c), F32)
    return _pallas(
        body, name=name, grid=(r // tr,), in_specs=[pl.BlockSpec((s, tr, c), lambda i: (0, i, 0)), spec, spec, spec],
        out_specs=[spec] * 4, out_shape=[one] * 4, compiler_params=_cparams("parallel"),
    )(parts, w, m, v)


def _cols_to_blocks(g, n_blk):
    r = g.shape[0]
    return g.reshape(r, n_blk, g.shape[1] // n_blk).transpose(1, 0, 2)


def _blocks_to_cols(gathered):
    n_blk, r, c = gathered.shape
    return gathered.transpose(1, 0, 2).reshape(r, n_blk * c)


def kernel(x, p, a_norm_pre, a_norm_post, a_w_in, a_lam_re, a_lam_im, a_log_dt, a_b_re, a_b_im, a_c_re, a_c_im, a_d_skip, a_w_glu, a_b_glu, a_w_out, kv_norm, w_kv, b_norm_pre, b_norm_post, b_w_in, b_w_out, ple_w_proj, ple_w_gate, loss_target, m_a_norm_pre, m_a_norm_post, m_a_w_in, m_a_lam_re, m_a_lam_im, m_a_log_dt, m_a_b_re, m_a_b_im, m_a_c_re, m_a_c_im, m_a_d_skip, m_a_w_glu, m_a_b_glu, m_a_w_out, m_kv_norm, m_w_kv, m_b_norm_pre, m_b_norm_post, m_b_w_in, m_b_w_out, m_ple_w_proj, m_ple_w_gate, v_a_norm_pre, v_a_norm_post, v_a_w_in, v_a_lam_re, v_a_lam_im, v_a_log_dt, v_a_b_re, v_a_b_im, v_a_c_re, v_a_c_im, v_a_d_skip, v_a_w_glu, v_a_b_glu, v_a_w_out, v_kv_norm, v_w_kv, v_b_norm_pre, v_b_norm_post, v_b_w_in, v_b_w_out, v_ple_w_proj, v_ple_w_gate):
    weights = dict(a_norm_pre=a_norm_pre, a_norm_post=a_norm_post, a_w_in=a_w_in, a_lam_re=a_lam_re, a_lam_im=a_lam_im, a_log_dt=a_log_dt, a_b_re=a_b_re, a_b_im=a_b_im, a_c_re=a_c_re, a_c_im=a_c_im, a_d_skip=a_d_skip, a_w_glu=a_w_glu, a_b_glu=a_b_glu, a_w_out=a_w_out, kv_norm=kv_norm, w_kv=w_kv, b_norm_pre=b_norm_pre, b_norm_post=b_norm_post, b_w_in=b_w_in, b_w_out=b_w_out, ple_w_proj=ple_w_proj, ple_w_gate=ple_w_gate)
    mom_m = dict(a_norm_pre=m_a_norm_pre, a_norm_post=m_a_norm_post, a_w_in=m_a_w_in, a_lam_re=m_a_lam_re, a_lam_im=m_a_lam_im, a_log_dt=m_a_log_dt, a_b_re=m_a_b_re, a_b_im=m_a_b_im, a_c_re=m_a_c_re, a_c_im=m_a_c_im, a_d_skip=m_a_d_skip, a_w_glu=m_a_w_glu, a_b_glu=m_a_b_glu, a_w_out=m_a_w_out, kv_norm=m_kv_norm, w_kv=m_w_kv, b_norm_pre=m_b_norm_pre, b_norm_post=m_b_norm_post, b_w_in=m_b_w_in, b_w_out=m_b_w_out, ple_w_proj=m_ple_w_proj, ple_w_gate=m_ple_w_gate)
    mom_v = dict(a_norm_pre=v_a_norm_pre, a_norm_post=v_a_norm_post, a_w_in=v_a_w_in, a_lam_re=v_a_lam_re, a_lam_im=v_a_lam_im, a_log_dt=v_a_log_dt, a_b_re=v_a_b_re, a_b_im=v_a_b_im, a_c_re=v_a_c_re, a_c_im=v_a_c_im, a_d_skip=v_a_d_skip, a_w_glu=v_a_w_glu, a_b_glu=v_a_b_glu, a_w_out=v_a_w_out, kv_norm=v_kv_norm, w_kv=v_w_kv, b_norm_pre=v_b_norm_pre, b_norm_post=v_b_norm_post, b_w_in=v_b_w_in, b_w_out=v_b_w_out, ple_w_proj=v_ple_w_proj, ple_w_gate=v_ple_w_gate)
    names = list(weights)

    t = x.shape[1]
    x0 = x[0]
    p0, p1 = p[0, 0], p[1, 0]
    tgt = loss_target[0]

    mats = dict(a_w_in=a_w_in[0], a_w_glu=a_w_glu[0], a_w_out=a_w_out[0], w_kv=w_kv, b_w_in=b_w_in[0],
                b_w_out=b_w_out[0], ple_w_proj=ple_w_proj.reshape(2 * 256, LANES),
                ple_w_gate=ple_w_gate.reshape(2 * LANES, D_MODEL))
    vec_names = ["a_norm_pre", "a_norm_post", "a_d_skip", "a_b_glu"]
    vec_pack = jnp.concatenate([weights[n] for n in vec_names], axis=0)
    gathered = _gather([m_.astype(BF16) for m_ in mats.values()] + [vec_pack], name="gather_weights")
    gw = dict(zip(mats, gathered[:-1]))
    vec_full = gathered[-1].transpose(1, 0, 2).reshape(len(vec_names), D_MODEL)
    g_apre, g_apost, d_skip, b_glu = (vec_full[i:i + 1] for i in range(4))
    w_u, w_ga = _blocks_to_cols(gw["a_w_in"][:4]), _blocks_to_cols(gw["a_w_in"][4:])
    w_glu = gw["a_w_glu"].reshape(D_MODEL, D_MODEL)
    w_out = gw["a_w_out"].reshape(D_MODEL, D_MODEL)
    w_k, w_v = _blocks_to_cols(gw["w_kv"][:4]), _blocks_to_cols(gw["w_kv"][4:])
    w_q, w_gb = _blocks_to_cols(gw["b_w_in"][:4]), _blocks_to_cols(gw["b_w_in"][4:])
    w_bout = gw["b_w_out"].reshape(D_MODEL, D_MODEL)
    wp = gw["ple_w_proj"].reshape(N_DEV, 2, 256, LANES)
    w_p0, w_p1 = _blocks_to_cols(wp[:, 0]), _blocks_to_cols(wp[:, 1])
    wg = gw["ple_w_gate"].reshape(N_DEV, 2, LANES, D_MODEL)
    w_g0, w_g1 = wg[:, 0].reshape(D_MODEL, D_MODEL), wg[:, 1].reshape(D_MODEL, D_MODEL)
    g_kv, g_bpre, g_bpost = kv_norm.reshape(1, D_MODEL), b_norm_pre, b_norm_post

    s5_params = (a_lam_re[0], a_lam_im[0], a_log_dt[0], a_b_re[0], a_b_im[0], a_c_re[0], a_c_im[0])
    (a_re, a_im, bbd, ccd), s5_vjp = jax.vjp(_s5_discretize, *s5_params)
    bbd_bf, ccd_bf = bbd.astype(BF16), ccd.astype(BF16)

    f_a2 = lambda ys, u, d: (_gelu(ys + d * u),)
    f_a3 = lambda g, s, ga, b: (g * _sigmoid(s + b) * _silu(ga),)
    f_rms = lambda y, g: (_rms(y, g),)
    f_k = lambda x2, gk, gp: (_rms(x2, gk), _rms(x2, gp))
    f_b3 = lambda o, gb: (o * _silu(gb),)

    (h0,) = _ew(f_rms, [x0], [g_apre], [BF16], name="a_norm_pre")
    u = _mm(h0, w_u, name="a_proj_u")
    ga = _mm(h0, w_ga, name="a_proj_gate")
    ys, hb = _s5_fwd(u, bbd_bf, ccd_bf, a_re, a_im)
    (g_act,) = _ew(f_a2, [ys, u], [d_skip], [BF16], name="a_gelu")
    s_glu = _mm(g_act, w_glu, name="a_glu")
    def f_a_out(ys_, u_, s_, ga_, x0_, d_, b_, g_, w):
        (y3_,) = f_a3(_gelu(ys_ + d_ * u_), s_, ga_, b_)
        y3_ = y3_.astype(BF16)
        y4_ = _dot(y3_, w)
        return y3_, y4_, x0_ + _rms(y4_, g_)

    y3, y4, x1 = _ew(f_a_out, [ys, u, s_glu, ga, x0], [d_skip, b_glu, g_apost], [BF16, F32, F32], mats=[w_out],
                     name="a_gate_out_norm")

    def f_ple(xin, p_, wg_, wp_):
        pg, pp = _dot(xin, wg_), _dot(p_, wp_)
        return xin + _sigmoid(pg) * pp, pg, pp

    x2, pg0, pp0 = _ew(f_ple, [x1, p0], [], [F32, F32, F32], mats=[w_g0, w_p0], name="ple0")
    kvn, h1 = _ew(f_k, [x2], [g_kv, g_bpre], [BF16, BF16], name="b_norms")
    k_ = _mm(kvn, w_k, out_dtype=BF16, name="kv_k")
    v_ = _mm(kvn, w_v, out_dtype=BF16, name="kv_v")
    q_ = _mm(h1, w_q, out_dtype=BF16, name="b_proj_q")
    gb = _mm(h1, w_gb, name="b_proj_gate")
    o, o_low = _attn_fwd(q_, k_, v_)

    def f_b_out(o_, gb_, x2_, g_, w):
        y5in_ = f_b3(o_, gb_)[0].astype(BF16)
        y5_ = _dot(y5in_, w)
        return y5in_, y5_, x2_ + _rms(y5_, g_)

    y5in, y5, x3 = _ew(f_b_out, [o, gb, x2], [g_bpost], [BF16, F32, F32], mats=[w_bout], name="b_gate_out_norm")

    def f_ple_loss(xin, p_, tg, wg_, wp_):
        pg, pp = _dot(xin, wg_), _dot(p_, wp_)
        err = xin + _sigmoid(pg) * pp - tg
        return err * (1.0 / D_MODEL), pg, pp, (0.5 / D_MODEL) * jnp.sum(err * err).reshape(1, 1)

    dx4, pg1, pp1, loss_part = _ew(f_ple_loss, [x3, p1, tgt], [], [F32, F32, F32], mats=[w_g1, w_p1], n_red=1,
                                   name="ple1_loss")
    loss = lax.psum(loss_part[0, 0], ("x", "y", "c"))

    grads = {}
    def f_ple_bwd(dx, pg, pp, wg_):
        sg = _sigmoid(pg)
        dpg = (dx * pp * sg * (1.0 - sg)).astype(BF16)
        return dx + _dot_t(dpg, wg_), dpg, dx * sg

    def f_norm_bwd_mm(y, dx, g_, w):
        _, vjp = jax.vjp(_rms, y, g_)
        dy, dg = vjp(dx)
        dy = dy.astype(BF16)
        return _dot_t(dy, w), dy, dg

    dx3, dpg1, dpp1 = _ew(f_ple_bwd, [dx4, pg1, pp1], [], [F32, BF16, BF16], mats=[w_g1], name="ple1_bwd")
    d_wg1 = _mm(x3, dpg1, ta=True, name="ple1_gate_dw")
    d_wp1 = _mm(p1, dpp1, ta=True, name="ple1_proj_dw")
    dy5in, dy5, d_gbpost = _ew(f_norm_bwd_mm, [y5, dx3], [g_bpost], [F32, BF16], mats=[w_bout], n_red=1,
                               name="b_norm_post_bwd")
    grads["b_w_out"] = _mm(y5in, dy5, ta=True, name="b_out_dw").reshape(N_DEV, LANES, D_MODEL)
    do, dgb = _ew_bwd(f_b3, [o, gb], [], [dy5in], [0, 1], [], [F32, BF16], name="b_gate_bwd")
    dq, dk, dv = _attn_bwd(q_, k_, v_, o, o_low, do)
    dh1 = _mm(dgb, w_gb, tb=True, add=_mm(dq, w_q, tb=True, name="b_proj_q_dx"), name="b_proj_gate_dx")
    grads["b_w_in"] = jnp.concatenate([_cols_to_blocks(_mm(h1, dq, ta=True, name="b_proj_q_dw"), 4),
                                       _cols_to_blocks(_mm(h1, dgb, ta=True, name="b_proj_gate_dw"), 4)])
    dkvn = _mm(dv, w_v, tb=True, add=_mm(dk, w_k, tb=True, name="kv_k_dx"), name="kv_v_dx")
    grads["w_kv"] = jnp.concatenate([_cols_to_blocks(_mm(kvn, dk, ta=True, name="kv_k_dw"), 4),
                                     _cols_to_blocks(_mm(kvn, dv, ta=True, name="kv_v_dw"), 4)])
    f_k3 = lambda x2_, gk, gp: f_k(x2_, gk, gp) + (x2_,)
    dx2, d_gkv, d_gbpre = _ew_bwd(f_k3, [x2], [g_kv, g_bpre], [dkvn, dh1, dx3], [0], [0, 1], [F32], name="b_norms_bwd")
    dx1, dpg0, dpp0 = _ew(f_ple_bwd, [dx2, pg0, pp0], [], [F32, BF16, BF16], mats=[w_g0], name="ple0_bwd")
    d_wg0 = _mm(x1, dpg0, ta=True, name="ple0_gate_dw")
    d_wp0 = _mm(p0, dpp0, ta=True, name="ple0_proj_dw")
    grads["ple_w_gate"] = jnp.stack([d_wg0.reshape(N_DEV, LANES, D_MODEL), d_wg1.reshape(N_DEV, LANES, D_MODEL)],
                                    axis=1).reshape(N_DEV, 2 * LANES, D_MODEL)
    grads["ple_w_proj"] = jnp.stack([_cols_to_blocks(d_wp0, N_DEV), _cols_to_blocks(d_wp1, N_DEV)],
                                    axis=1).reshape(N_DEV, 2 * 256, LANES)
    dy3, dy4, d_gapost = _ew(f_norm_bwd_mm, [y4, dx1], [g_apost], [F32, BF16], mats=[w_out], n_red=1,
                             name="a_norm_post_bwd")
    grads["a_w_out"] = _mm(y3, dy4, ta=True, name="a_out_dw").reshape(N_DEV, LANES, D_MODEL)
    dg_direct, ds, dga, d_bglu = _a_gate_bwd(ys, u, s_glu, ga, d_skip, b_glu, dy3, f_a3)
    dg = _mm(ds, w_glu, tb=True, add=dg_direct, name="a_glu_dx")
    grads["a_w_glu"] = _mm(g_act, ds, ta=True, name="a_glu_dw").reshape(N_DEV, LANES, D_MODEL)
    dys, du_elem, d_dskip = _ew_bwd(f_a2, [ys, u], [d_skip], [dg], [0, 1], [0], [F32, F32], name="a_gelu_bwd")
    du, d_bbd, d_ccd, d_are, d_aim = _s5_bwd(u, dys, du_elem, hb, bbd_bf, ccd_bf, a_re, a_im)
    dh0 = _mm(dga, w_ga, tb=True, add=_mm(du, w_u, tb=True, name="a_proj_u_dx"), name="a_proj_gate_dx")
    grads["a_w_in"] = jnp.concatenate([_cols_to_blocks(_mm(h0, du, ta=True, name="a_proj_u_dw"), 4),
                                       _cols_to_blocks(_mm(h0, dga, ta=True, name="a_proj_gate_dw"), 4)])
    f_rms_res = lambda x0_, g: (_rms(x0_, g), x0_)
    grad_x, d_gapre = _ew_bwd(f_rms_res, [x0], [g_apre], [dh0, dx1], [0], [0], [F32], name="a_norm_pre_bwd")

    d_s5 = s5_vjp((d_are, d_aim, d_bbd, d_ccd))
    small = dict(zip(["a_lam_re", "a_lam_im", "a_log_dt", "a_b_re", "a_b_im", "a_c_re", "a_c_im"], d_s5))
    small.update(kv_norm=d_gkv, b_norm_pre=d_gbpre, b_norm_post=d_gbpost)
    small_names = list(small)
    small_flat = jnp.concatenate([small[n].reshape(-1) for n in small_names])
    small_pack = jnp.pad(small_flat, (0, -small_flat.size % (8 * LANES))).reshape(-1, LANES)
    vec_grads = jnp.concatenate([d_gapre, d_gapost, d_dskip, d_bglu], axis=0)
    vec_scatter = vec_grads.reshape(len(vec_names), N_DEV, LANES).transpose(1, 0, 2)
    mat_names = list(mats)
    scattered = _reduce_scatter([grads[n] for n in mat_names] + [vec_scatter], mat_names + ["vectors"])
    (small_all,) = _gather([small_pack], name="gather_small_grads")

    out_g, out_d, out_m, out_v = {}, {}, {}, {}

    def update(n, parts):
        shape = weights[n].shape
        rc = parts.shape[1:]
        res = _adamw(parts, weights[n].reshape(rc), mom_m[n].reshape(rc), mom_v[n].reshape(rc), name="adamw_" + n)
        out_g[n], out_d[n], out_m[n], out_v[n] = (r.reshape(shape) for r in res)

    for n, parts in zip(mat_names, scattered[:-1]):
        update(n, parts)
    for i, n in enumerate(vec_names):
        update(n, scattered[-1][:, i:i + 1, :])
    off = 0
    for n in small_names:
        size = small[n].size
        lead = weights[n].shape[:-1]
        rc = (math.prod(lead), weights[n].shape[-1])
        update(n, small_all.reshape(N_DEV, -1)[:, off:off + size].reshape((N_DEV,) + rc))
        off += size

    return (loss, grad_x[None], *[out_g[n] for n in names], *[out_d[n] for n in names],
            *[out_m[n] for n in names], *[out_v[n] for n in names])


def _a_gate_bwd(ys, u, s_glu, ga, d_skip, b_glu, dy3, f_a3):
    def fn(ys_, u_, s_, ga_, ct, d_, b_):
        g = _gelu(ys_ + d_ * u_)
        _, vjp = jax.vjp(lambda g_, s__, ga__, b__: f_a3(g_, s__, ga__, b__)[0], g, s_, ga_, b_)
        dg, ds, dga, db = vjp(ct)
        return dg, ds, dga, db

    return _ew(fn, [ys, u, s_glu, ga, dy3], [d_skip, b_glu], [F32, BF16, BF16], n_red=1, name="a_gate_bwd")
```

```python
import functools
import math

import jax
import jax.numpy as jnp
from jax import lax
from jax.experimental import pallas as pl
from jax.experimental.pallas import tpu as pltpu

F32 = jnp.float32
BF16 = jnp.bfloat16

N_DEV = 8
D_MODEL = 1024
GROUP_SIZE = 16
N_GROUPS = D_MODEL // GROUP_SIZE
STATE = 64
HEAD_DIM = 64
EPS = 1e-6
LANES = 128
GROUPS_PER_BLOCK = LANES // GROUP_SIZE
N_GBLK = N_GROUPS // GROUPS_PER_BLOCK
STATE_COLS = 2 * GROUPS_PER_BLOCK * STATE
N_SLAB = STATE_COLS // LANES
VMEM_LIMIT = 56 * 1024 * 1024
EXP_UNDERFLOW = -104.0
ATTN_LANES = 256
MM_TILE = 1024
EW_ROWS_BYTES = 24 * 1024 * 1024

ADAM_LR = 0.001
ADAM_B1 = 0.9
ADAM_B2 = 0.999
ADAM_EPS = 1e-08
ADAM_WD = 0.01
ADAM_STEP = 10


def _pallas(body, **kw):
    return pl.pallas_call(body, **kw)


def _cparams(*sem):
    return pltpu.CompilerParams(dimension_semantics=sem, vmem_limit_bytes=VMEM_LIMIT)


def _mm(a, b, *, ta=False, tb=False, add=None, out_dtype=F32, name):
    m, k = (a.shape[1], a.shape[0]) if ta else a.shape
    n = b.shape[0] if tb else b.shape[1]
    tm, tn, tk = min(m, MM_TILE), min(n, MM_TILE), min(k, MM_TILE)
    nk = k // tk
    grid = (m // tm, n // tn, nk)
    a_spec = pl.BlockSpec((tk, tm), lambda i, j, l: (l, i)) if ta else pl.BlockSpec((tm, tk), lambda i, j, l: (i, l))
    b_spec = pl.BlockSpec((tn, tk), lambda i, j, l: (j, l)) if tb else pl.BlockSpec((tk, tn), lambda i, j, l: (l, j))
    o_spec = pl.BlockSpec((tm, tn), lambda i, j, l: (i, j))
    dims = (((0 if ta else 1,), (1 if tb else 0,)), ((), ()))
    has_add = add is not None
    assert nk == 1 or (out_dtype == F32 and not has_add)

    def body(*refs):
        a_ref, b_ref, o_ref = refs[0], refs[1], refs[-1]
        part = lax.dot_general(a_ref[...].astype(BF16), b_ref[...].astype(BF16), dims, preferred_element_type=F32)
        if nk == 1:
            if has_add:
                part = part + refs[2][...].astype(F32)
            o_ref[...] = part.astype(out_dtype)
        else:
            l = pl.program_id(2)

            @pl.when(l == 0)
            def _():
                o_ref[...] = part

            @pl.when(l > 0)
            def _():
                o_ref[...] += part

    ins = [a, b] + ([add] if has_add else [])
    specs = [a_spec, b_spec] + ([o_spec] if has_add else [])
    return _pallas(
        body, name=name, grid=grid, in_specs=specs, out_specs=o_spec,
        out_shape=jax.ShapeDtypeStruct((m, n), out_dtype),
        compiler_params=_cparams("parallel", "parallel", "arbitrary"),
    )(*ins)


def _row_spec(arr, tm):
    return pl.BlockSpec((tm, arr.shape[1]), lambda i: (i, 0))


def _full_spec(arr):
    return pl.BlockSpec(arr.shape, lambda i: (0,) * arr.ndim)


def _ew_tile_rows(t, row_bytes):
    tm = 1024
    while tm > 8 and (t % tm or 2 * tm * row_bytes > EW_ROWS_BYTES):
        tm //= 2
    return tm


def _ew(fn, rows, vecs, out_dtypes, *, mats=(), n_red=0, name):
    t = rows[0].shape[0]
    nr, nv, nm = len(rows), len(vecs), len(mats)

    def tile_shapes(tm):
        return jax.eval_shape(fn, *[jax.ShapeDtypeStruct((tm, r.shape[1]), F32) for r in rows],
                              *[jax.ShapeDtypeStruct(v.shape, F32) for v in vecs],
                              *[jax.ShapeDtypeStruct(m_.shape, m_.dtype) for m_ in mats])

    probe = tile_shapes(8)
    row_bytes = sum(r.shape[1] * r.dtype.itemsize for r in rows)
    row_bytes += sum(s.shape[1] * jnp.dtype(dt).itemsize for s, dt in zip(probe, out_dtypes))
    tm = _ew_tile_rows(t, row_bytes)
    shapes = tile_shapes(tm)
    n_out = len(shapes) - n_red

    def body(*refs):
        ins = [r[...].astype(F32) for r in refs[:nr + nv]] + [r[...] for r in refs[nr + nv:nr + nv + nm]]
        outs = fn(*ins)
        o_refs = refs[nr + nv + nm:]
        for o_ref, o in zip(o_refs[:n_out], outs[:n_out]):
            o_ref[...] = o.astype(o_ref.dtype)
        i = pl.program_id(0)
        for o_ref, o in zip(o_refs[n_out:], outs[n_out:]):
            @pl.when(i == 0)
            def _(o_ref=o_ref, o=o):
                o_ref[...] = o

            @pl.when(i > 0)
            def _(o_ref=o_ref, o=o):
                o_ref[...] += o

    out_shape = [jax.ShapeDtypeStruct((t, s.shape[1]), dt) for s, dt in zip(shapes[:n_out], out_dtypes)]
    out_shape += [jax.ShapeDtypeStruct(s.shape, F32) for s in shapes[n_out:]]
    out_specs = [pl.BlockSpec((tm, s.shape[1]), lambda i: (i, 0)) for s in shapes[:n_out]]
    out_specs += [pl.BlockSpec(s.shape, lambda i: (0, 0)) for s in shapes[n_out:]]
    return _pallas(
        body, name=name, grid=(t // tm,),
        in_specs=[_row_spec(r, tm) for r in rows] + [_full_spec(v) for v in list(vecs) + list(mats)],
        out_specs=out_specs, out_shape=out_shape,
        compiler_params=_cparams("arbitrary" if n_red else "parallel"),
    )(*rows, *vecs, *mats)


def _dot(a, w):
    return jnp.dot(a.astype(BF16), w, preferred_element_type=F32)


def _dot_t(a, w):
    return lax.dot_general(a.astype(BF16), w, (((1,), (1,)), ((), ())), preferred_element_type=F32)


def _ew_bwd(fn, rows, vecs, cts, row_wrt, vec_wrt, out_dtypes, *, name):
    nr, nv, nc = len(rows), len(vecs), len(cts)

    def grad_fn(*tiles):
        ins = list(tiles[:nr + nv])
        ct = tiles[nr + nv:]
        wrt = list(row_wrt) + [nr + j for j in vec_wrt]

        def f(*sel):
            full = list(ins)
            for idx, s in zip(wrt, sel):
                full[idx] = s
            return tuple(fn(*full))

        _, vjp = jax.vjp(f, *[ins[idx] for idx in wrt])
        return vjp(tuple(ct))

    return _ew_reordered(grad_fn, rows, vecs, cts, out_dtypes, len(vec_wrt), name)


def _ew_reordered(grad_fn, rows, vecs, cts, out_dtypes, n_red, name):
    nr, nv = len(rows), len(vecs)

    def fn(*tiles):
        r, c, v = tiles[:nr], tiles[nr:nr + len(cts)], tiles[nr + len(cts):]
        return grad_fn(*r, *v, *c)

    return _ew(fn, list(rows) + list(cts), vecs, out_dtypes, n_red=n_red, name=name)


def _rms(x, g):
    return x * lax.rsqrt(jnp.mean(x * x, axis=-1, keepdims=True) + EPS) * g


def _sigmoid(x):
    return 1.0 / (1.0 + jnp.exp(-x))


def _silu(x):
    return x * _sigmoid(x)


def _gelu(x):
    return 0.5 * x * (1.0 + jnp.tanh(math.sqrt(2.0 / math.pi) * (x + 0.044715 * x * x * x)))


def _token_rows(t):
    return pl.ds(pl.multiple_of(t * N_GBLK, N_GBLK), N_GBLK)


def _block_rows(gb, tt):
    return pl.ds(gb, tt, stride=N_GBLK)


def _s5_expand(u_bf, bbd_ref, xs_ref, tt):
    for gb in range(N_GBLK):
        xg = jnp.dot(u_bf[:, gb * LANES:(gb + 1) * LANES], bbd_ref[gb], preferred_element_type=F32)
        for l in range(N_SLAB):
            xs_ref[l, _block_rows(gb, tt), :] = xg[:, l * LANES:(l + 1) * LANES]


def _s5_forward_scan(xs_ref, ar, ai, state, tt):
    half = N_SLAB // 2

    def step(t, st):
        new = [None] * N_SLAB
        for l in range(half):
            sr, si = st[l], st[half + l]
            rows = _token_rows(t)
            nr_ = ar[l] * sr - ai[l] * si + xs_ref[l, rows, :]
            ni_ = ar[l] * si + ai[l] * sr + xs_ref[half + l, rows, :]
            xs_ref[l, rows, :] = nr_
            xs_ref[half + l, rows, :] = ni_
            new[l], new[half + l] = nr_, ni_
        return tuple(new)

    return lax.fori_loop(0, tt, step, tuple(state), unroll=4)


def _slab_rows(ref, gb, tt):
    return jnp.concatenate([ref[l, _block_rows(gb, tt), :] for l in range(N_SLAB)], axis=1)


def _s5_fwd(u, bbd, ccd, a_re, a_im, *, tt=256):
    t = u.shape[0]
    nt = t // tt
    half = N_SLAB // 2

    def body(u_ref, bbd_ref, ccd_ref, ar_ref, ai_ref, y_ref, hb_ref, xs_ref, st_ref):
        i = pl.program_id(0)

        @pl.when(i == 0)
        def _():
            st_ref[...] = jnp.zeros_like(st_ref)

        hb_ref[0] = st_ref[...]
        _s5_expand(u_ref[...].astype(BF16), bbd_ref, xs_ref, tt)
        ar = [ar_ref[l] for l in range(half)]
        ai = [ai_ref[l] for l in range(half)]
        last = _s5_forward_scan(xs_ref, ar, ai, [st_ref[l] for l in range(N_SLAB)], tt)
        for l in range(N_SLAB):
            st_ref[l] = last[l]
        for gb in range(N_GBLK):
            sg = _slab_rows(xs_ref, gb, tt).astype(BF16)
            y_ref[:, gb * LANES:(gb + 1) * LANES] = jnp.dot(sg, ccd_ref[gb], preferred_element_type=F32)

    return _pallas(
        body, name="s5_fwd", grid=(nt,),
        in_specs=[pl.BlockSpec((tt, D_MODEL), lambda i: (i, 0)), _full_spec(bbd), _full_spec(ccd),
                  _full_spec(a_re), _full_spec(a_im)],
        out_specs=[pl.BlockSpec((tt, D_MODEL), lambda i: (i, 0)),
                   pl.BlockSpec((1, N_SLAB, N_GBLK, LANES), lambda i: (i, 0, 0, 0))],
        out_shape=[jax.ShapeDtypeStruct((t, D_MODEL), F32), jax.ShapeDtypeStruct((nt, N_SLAB, N_GBLK, LANES), F32)],
        scratch_shapes=[pltpu.VMEM((N_SLAB, N_GBLK * tt, LANES), F32), pltpu.VMEM((N_SLAB, N_GBLK, LANES), F32)],
        compiler_params=_cparams("arbitrary"),
    )(u, bbd, ccd, a_re, a_im)


def _s5_bwd(u, dy, du_add, hb, bbd, ccd, a_re, a_im, *, tt=256):
    t = u.shape[0]
    nt = t // tt
    half = N_SLAB // 2
    rev = lambda i: (nt - 1 - i, 0)

    def body(u_ref, dy_ref, dua_ref, hb_ref, bbd_ref, ccd_ref, ar_ref, ai_ref,
             du_ref, dbbd_ref, dccd_ref, dar_ref, dai_ref, xs_ref, es_ref, lam_ref):
        i = pl.program_id(0)

        @pl.when(i == 0)
        def _():
            lam_ref[...] = jnp.zeros_like(lam_ref)
            dbbd_ref[...] = jnp.zeros_like(dbbd_ref)
            dccd_ref[...] = jnp.zeros_like(dccd_ref)
            dar_ref[...] = jnp.zeros_like(dar_ref)
            dai_ref[...] = jnp.zeros_like(dai_ref)

        u_bf = u_ref[...].astype(BF16)
        dy_bf = dy_ref[...].astype(BF16)
        _s5_expand(u_bf, bbd_ref, xs_ref, tt)
        ar = [ar_ref[l] for l in range(half)]
        ai = [ai_ref[l] for l in range(half)]
        start = [hb_ref[0, l] for l in range(N_SLAB)]
        _s5_forward_scan(xs_ref, ar, ai, start, tt)
        for gb in range(N_GBLK):
            eg = lax.dot_general(dy_bf[:, gb * LANES:(gb + 1) * LANES], ccd_ref[gb], (((1,), (1,)), ((), ())),
                                 preferred_element_type=F32)
            for l in range(N_SLAB):
                es_ref[l, _block_rows(gb, tt), :] = eg[:, l * LANES:(l + 1) * LANES]

        def lam_step(t_, lam, prev, acc):
            new_lam, new_acc = [None] * N_SLAB, [None] * N_SLAB
            rows = _token_rows(t_)
            for l in range(half):
                lr = es_ref[l, rows, :] + ar[l] * lam[l] + ai[l] * lam[half + l]
                li = es_ref[half + l, rows, :] - ai[l] * lam[l] + ar[l] * lam[half + l]
                es_ref[l, rows, :] = lr
                es_ref[half + l, rows, :] = li
                pr, pi = prev(l), prev(half + l)
                new_acc[l] = acc[l] + lr * pr + li * pi
                new_acc[half + l] = acc[half + l] - lr * pi + li * pr
                new_lam[l], new_lam[half + l] = lr, li
            return tuple(new_lam), tuple(new_acc)

        def loop_body(k, carry):
            lam, acc = carry
            t_ = tt - 1 - k
            return lam_step(t_, lam, lambda l: xs_ref[l, _token_rows(t_ - 1), :], acc)

        zero = tuple(jnp.zeros((N_GBLK, LANES), F32) for _ in range(N_SLAB))
        lam0 = tuple(lam_ref[l] for l in range(N_SLAB))
        lam, acc = lax.fori_loop(0, tt - 1, loop_body, (lam0, zero), unroll=2)
        lam, acc = lam_step(0, lam, lambda l: start[l], acc)
        for l in range(N_SLAB):
            lam_ref[l] = lam[l]
        for l in range(half):
            dar_ref[l] += acc[l]
            dai_ref[l] += acc[half + l]

        for gb in range(N_GBLK):
            cols = slice(gb * LANES, (gb + 1) * LANES)
            lg = _slab_rows(es_ref, gb, tt).astype(BF16)
            sg = _slab_rows(xs_ref, gb, tt).astype(BF16)
            du_ref[:, cols] = (dua_ref[:, cols] + lax.dot_general(
                lg, bbd_ref[gb], (((1,), (1,)), ((), ())), preferred_element_type=F32)).astype(du_ref.dtype)
            dbbd_ref[gb] += lax.dot_general(u_bf[:, cols], lg, (((0,), (0,)), ((), ())), preferred_element_type=F32)
            dccd_ref[gb] += lax.dot_general(sg, dy_bf[:, cols], (((0,), (0,)), ((), ())), preferred_element_type=F32)

    row = pl.BlockSpec((tt, D_MODEL), rev)
    small = pl.BlockSpec((half, N_GBLK, LANES), lambda i: (0, 0, 0))
    return _pallas(
        body, name="s5_bwd", grid=(nt,),
        in_specs=[row, row, row, pl.BlockSpec((1, N_SLAB, N_GBLK, LANES), lambda i: (nt - 1 - i, 0, 0, 0)),
                  _full_spec(bbd), _full_spec(ccd), _full_spec(a_re), _full_spec(a_im)],
        out_specs=[row, _full_spec(bbd), _full_spec(ccd), small, small],
        out_shape=[jax.ShapeDtypeStruct((t, D_MODEL), BF16), jax.ShapeDtypeStruct(bbd.shape, F32),
                   jax.ShapeDtypeStruct(ccd.shape, F32), jax.ShapeDtypeStruct(a_re.shape, F32),
                   jax.ShapeDtypeStruct(a_im.shape, F32)],
        scratch_shapes=[pltpu.VMEM((N_SLAB, N_GBLK * tt, LANES), F32), pltpu.VMEM((N_SLAB, N_GBLK * tt, LANES), F32),
                        pltpu.VMEM((N_SLAB, N_GBLK, LANES), F32)],
        compiler_params=_cparams("arbitrary"),
    )(u, dy, du_add, hb, bbd, ccd, a_re, a_im)


def _s5_discretize(lam_re, lam_im, log_dt, b_re, b_im, c_re, c_im):
    lr = jnp.minimum(lam_re, -1e-4)
    li = lam_im
    dt = jnp.exp(log_dt)[:, None]
    mag = jnp.exp(lr * dt)
    a_re = mag * jnp.cos(li * dt)
    a_im = mag * jnp.sin(li * dt)
    den = lr * lr + li * li
    nr = a_re - 1.0
    f_re = (nr * lr + a_im * li) / den
    f_im = (a_im * lr - nr * li) / den
    bb_re = f_re[..., None] * b_re - f_im[..., None] * b_im
    bb_im = f_re[..., None] * b_im + f_im[..., None] * b_re
    eye = jnp.eye(GROUPS_PER_BLOCK, dtype=F32)
    bb = jnp.stack([bb_re, bb_im]).reshape(2, N_GBLK, GROUPS_PER_BLOCK, STATE, GROUP_SIZE)
    bbd = jnp.einsum('rgjph,jk->gjhrkp', bb, eye).reshape(N_GBLK, LANES, STATE_COLS)
    cc = jnp.stack([c_re, -c_im]).reshape(2, N_GBLK, GROUPS_PER_BLOCK, GROUP_SIZE, STATE)
    ccd = jnp.einsum('rgjhp,jk->grjpkh', cc, eye).reshape(N_GBLK, STATE_COLS, LANES)

    def dense(a):
        return a.reshape(N_GBLK, N_SLAB // 2, LANES).transpose(1, 0, 2)

    return dense(a_re), dense(a_im), bbd, ccd


def _attn_tiles(tq):
    row = lax.broadcasted_iota(jnp.int32, (tq, tq), 0)
    col = lax.broadcasted_iota(jnp.int32, (tq, tq), 1)
    return col < row, (row > col).astype(BF16), (row >= col).astype(BF16)


def _split_bf16(x):
    hi = x.astype(BF16)
    return hi, (x - hi.astype(F32)).astype(BF16)


def _attn_scores(qs, kj, cs, diag, later, masked):
    zs = [lax.dot_general(qa, kj, (((1,), (1,)), ((), ())), preferred_element_type=F32) * (HEAD_DIM ** -0.5) for qa in qs]
    lks = [-(jnp.maximum(z, 0.0) + jnp.log(1.0 + jnp.exp(-jnp.abs(z)))) for z in zs]
    if masked:
        lks = [jnp.where(diag, lk, 0.0) for lk in lks]
    parts = [_split_bf16(lk) for lk in lks]
    tails = [jnp.dot(hi, later, preferred_element_type=F32) + jnp.dot(lo, later, preferred_element_type=F32)
             for hi, lo in parts]
    ws = [jnp.exp(z + lk + tail + c) for z, lk, tail, c in zip(zs, lks, tails, cs)]
    if masked:
        ws = [jnp.where(diag, w, 0.0) for w in ws]
    return zs, lks, ws


def _head_masks(width=LANES):
    lane = lax.broadcasted_iota(jnp.int32, (1, width), 1)
    return [(lane >= a * HEAD_DIM) & (lane < (a + 1) * HEAD_DIM) for a in range(width // HEAD_DIM)]


def _any_alive(cs):
    alive = jnp.max(cs[0]) > EXP_UNDERFLOW
    for c in cs[1:]:
        alive = alive | (jnp.max(c) > EXP_UNDERFLOW)
    return alive


def _attn_fwd(q, k, v, *, tq=256):
    t = q.shape[0]
    tq = min(tq, t)
    width = ATTN_LANES

    def body(q_ref, k_ref, v_ref, o_ref, ol_ref):
        i = pl.program_id(1)
        diag, later, _ = _attn_tiles(tq)
        masks = _head_masks(width)
        q2 = q_ref[...]
        qs = [jnp.where(hm, q2, jnp.zeros_like(q2)) for hm in masks]

        def block(j, state, masked):
            rows = pl.ds(pl.multiple_of(j * tq, tq), tq)
            kj, vj = k_ref[rows, :], v_ref[rows, :]
            _, lks, ws = _attn_scores(qs, kj, [h[0] for h in state], diag, later, masked)
            parts = [_split_bf16(w) for w in ws]
            return tuple((c + jnp.sum(lk, axis=1, keepdims=True),
                          acc + jnp.dot(w_hi, vj, preferred_element_type=F32),
                          low + jnp.dot(w_lo, vj, preferred_element_type=F32))
                         for (c, acc, low), lk, (w_hi, w_lo) in zip(state, lks, parts))

        zero = jnp.zeros((tq, width), F32)
        state = block(i, tuple((jnp.zeros((tq, 1), F32), zero, zero) for _ in masks), True)

        def cond(st):
            return (st[0] >= 0) & _any_alive([h[0] for h in st[1]])

        def step(st):
            return st[0] - 1, block(st[0], st[1], False)

        _, state = lax.while_loop(cond, step, (i - 1, state))
        o_ref[...] = sum(jnp.where(hm, h[1], 0.0) for hm, h in zip(masks, state))
        ol_ref[...] = sum(jnp.where(hm, h[2], 0.0) for hm, h in zip(masks, state))

    qspec = pl.BlockSpec((tq, width), lambda h, i: (i, h))
    kspec = pl.BlockSpec((t, width), lambda h, i: (0, h))
    full = jax.ShapeDtypeStruct((t, D_MODEL), F32)
    return _pallas(
        body, name="attn_fwd", grid=(D_MODEL // width, t // tq), in_specs=[qspec, kspec, kspec], out_specs=[qspec, qspec],
        out_shape=[full, full], compiler_params=_cparams("parallel", "arbitrary"),
    )(q, k, v)


def _attn_bwd(q, k, v, o, o_low, do, *, tq=256):
    t = q.shape[0]
    tq = min(tq, t)
    width = ATTN_LANES

    def body(q_ref, k_ref, v_ref, o_ref, ol_ref, do_ref, dq_ref, dk_ref, dv_ref):
        i = pl.program_id(1)

        @pl.when(i == 0)
        def _():
            dk_ref[...] = jnp.zeros_like(dk_ref)
            dv_ref[...] = jnp.zeros_like(dv_ref)

        diag, later, later_eq = _attn_tiles(tq)
        masks = _head_masks(width)
        q2 = q_ref[...]
        do2 = do_ref[...]
        dd = do2.astype(BF16).astype(F32) * (o_ref[...] + ol_ref[...])
        qs = [jnp.where(hm, q2, jnp.zeros_like(q2)) for hm in masks]
        dos = [jnp.where(hm, do2, 0.0).astype(BF16) for hm in masks]
        totals = [jnp.sum(jnp.where(hm, dd, 0.0), axis=1, keepdims=True) for hm in masks]

        def block(j, state, masked):
            rows = pl.ds(pl.multiple_of(j * tq, tq), tq)
            kj, vj = k_ref[rows, :], v_ref[rows, :]
            nt = (((1,), (1,)), ((), ()))
            tn = (((0,), (0,)), ((), ()))
            zs, lks, ws = _attn_scores(qs, kj, [h[0] for h in state], diag, later, masked)
            dws = [lax.dot_general(doa, vj, nt, preferred_element_type=F32) for doa in dos]
            gs = [dw * w for dw, w in zip(dws, ws)]
            parts = [_split_bf16(g) for g in gs]
            from_here = [jnp.dot(hi, later_eq, preferred_element_type=F32) + jnp.dot(lo, later_eq, preferred_element_type=F32)
                         for hi, lo in parts]
            dzs = []
            for z, lk, g, fh, total, h in zip(zs, lks, gs, from_here, totals, state):
                beta = jnp.exp(z + lk)
                dz = g * (1.0 - beta) - beta * (total - h[1] - fh)
                if masked:
                    dz = jnp.where(diag, dz, 0.0)
                dzs.append((dz * (HEAD_DIM ** -0.5)).astype(BF16))
            dk = sum(lax.dot_general(dz, qa, tn, preferred_element_type=F32) for dz, qa in zip(dzs, qs))
            dv = sum(lax.dot_general(w.astype(BF16), doa, tn, preferred_element_type=F32) for w, doa in zip(ws, dos))
            new = [(c + jnp.sum(lk, axis=1, keepdims=True), r + jnp.sum(g, axis=1, keepdims=True),
                    acc + jnp.dot(dz, kj, preferred_element_type=F32))
                   for (c, r, acc), lk, g, dz in zip(state, lks, gs, dzs)]
            dk_ref[rows, :] += dk
            dv_ref[rows, :] += dv
            return tuple(new)

        zc = jnp.zeros((tq, 1), F32)
        state = block(i, tuple((zc, zc, jnp.zeros((tq, width), F32)) for _ in masks), True)

        def cond(st):
            return (st[0] >= 0) & _any_alive([h[0] for h in st[1]])

        def step(st):
            return st[0] - 1, block(st[0], st[1], False)

        _, state = lax.while_loop(cond, step, (i - 1, state))
        dq_ref[...] = sum(jnp.where(hm, h[2], 0.0) for hm, h in zip(masks, state))

    qspec = pl.BlockSpec((tq, width), lambda h, i: (i, h))
    kspec = pl.BlockSpec((t, width), lambda h, i: (0, h))
    kin = pl.BlockSpec((t, width), lambda h, i: (0, h), pipeline_mode=pl.Buffered(1))
    full = jax.ShapeDtypeStruct((t, D_MODEL), F32)
    return _pallas(
        body, name="attn_bwd", grid=(D_MODEL // width, t // tq), in_specs=[qspec, kin, kin, qspec, qspec, qspec],
        out_specs=[qspec, kspec, kspec], out_shape=[full, full, full],
        compiler_params=_cparams("parallel", "arbitrary"),
    )(q, k, v, o, o_low, do)


N_CHIP = 4
MESH_ID = pl.DeviceIdType.MESH
ANY_SPEC = pl.BlockSpec(memory_space=pl.ANY)


def _position():
    return lax.axis_index("x"), lax.axis_index("y"), lax.axis_index("c")


def _other_chips(x, y):
    return [(1 - x, y), (x, 1 - y), (1 - x, 1 - y)]


def _gather(arrs, *, name):
    n = len(arrs)

    def body(*refs):
        ins, outs = refs[:n], refs[n:2 * n]
        send_sems, recv_sems, local_sems = refs[2 * n:]
        x, y, c = _position()
        me, sibling = (x, y, c), (x, y, 1 - c)
        chips = _other_chips(x, y)

        def copy(a, k, block, to, src=None):
            slot = outs[a].at[4 * block[0] + 2 * block[1] + block[2]]
            return pltpu.make_async_remote_copy(
                src_ref=slot if src is None else src, dst_ref=slot, send_sem=send_sems.at[a, k],
                recv_sem=recv_sems.at[a, k], device_id=to, device_id_type=MESH_ID)

        local = [pltpu.make_async_copy(ins[a], outs[a].at[4 * x + 2 * y + c], local_sems.at[a]) for a in range(n)]
        first = []
        for a in range(n):
            local[a].start()
            first.append(copy(a, 0, me, sibling, src=ins[a]))
            first += [copy(a, 1 + j, me, (*chip, c), src=ins[a]) for j, chip in enumerate(chips)]
        for cp in first:
            cp.start()
        passed = []
        for a in range(n):
            for j, chip in enumerate(chips):
                copy(a, 1 + j, (*chip, c), me).wait_recv()
                passed.append(copy(a, 4 + j, (*chip, c), sibling))
                passed[-1].start()
        for a in range(n):
            copy(a, 0, sibling, me).wait_recv()
            for j, chip in enumerate(chips):
                copy(a, 4 + j, (*chip, 1 - c), me).wait_recv()
        for cp in first + passed:
            cp.wait_send()
        for cp in local:
            cp.wait()

    return _pallas(
        body, name=name, in_specs=[ANY_SPEC] * n, out_specs=[ANY_SPEC] * n,
        out_shape=[jax.ShapeDtypeStruct((N_DEV,) + a.shape, a.dtype) for a in arrs],
        scratch_shapes=[pltpu.SemaphoreType.DMA((n, N_DEV - 1)), pltpu.SemaphoreType.DMA((n, N_DEV - 1)),
                        pltpu.SemaphoreType.DMA((n,))],
    )(*arrs)


def _pair_exchange(arrs, *, name):
    n = len(arrs)

    def body(*refs):
        ins, got = refs[:n], refs[n:2 * n]
        send_sems, recv_sems = refs[2 * n:]
        x, y, c = _position()
        sends = []
        for a in range(n):
            for ch in range(N_CHIP):
                sends.append(pltpu.make_async_remote_copy(
                    src_ref=ins[a].at[2 * ch + (1 - c)], dst_ref=got[a].at[ch], send_sem=send_sems.at[a, ch],
                    recv_sem=recv_sems.at[a, ch], device_id=(x, y, 1 - c), device_id_type=MESH_ID))
        for cp in sends:
            cp.start()
        for cp in sends:
            cp.wait_send()
            cp.wait_recv()

    return _pallas(
        body, name=name, in_specs=[ANY_SPEC] * n, out_specs=[ANY_SPEC] * n,
        out_shape=[jax.ShapeDtypeStruct((N_CHIP,) + a.shape[1:], a.dtype) for a in arrs],
        scratch_shapes=[pltpu.SemaphoreType.DMA((n, N_CHIP)), pltpu.SemaphoreType.DMA((n, N_CHIP))],
    )(*arrs)


def _chip_exchange(arrs, *, name):
    n = len(arrs)

    def body(*refs):
        ins, outs = refs[:n], refs[n:2 * n]
        send_sems, recv_sems, local_sems = refs[2 * n:]
        x, y, c = _position()
        mine = 2 * x + y
        chips = _other_chips(x, y)
        sends, recvs, locals_ = [], [], []
        for a in range(n):
            locals_.append(pltpu.make_async_copy(ins[a].at[mine], outs[a].at[mine], local_sems.at[a]))
            for j, (px, py) in enumerate(chips):
                sends.append(pltpu.make_async_remote_copy(
                    src_ref=ins[a].at[2 * px + py], dst_ref=outs[a].at[mine], send_sem=send_sems.at[a, j],
                    recv_sem=recv_sems.at[a, j], device_id=(px, py, c), device_id_type=MESH_ID))
                recvs.append(pltpu.make_async_remote_copy(
                    src_ref=ins[a].at[2 * px + py], dst_ref=outs[a].at[2 * px + py], send_sem=send_sems.at[a, j],
                    recv_sem=recv_sems.at[a, j], device_id=(px, py, c), device_id_type=MESH_ID))
        for cp in locals_ + sends:
            cp.start()
        for cp in sends:
            cp.wait_send()
        for cp in recvs:
            cp.wait_recv()
        for cp in locals_:
            cp.wait()

    return _pallas(
        body, name=name, in_specs=[ANY_SPEC] * n, out_specs=[ANY_SPEC] * n,
        out_shape=[jax.ShapeDtypeStruct(a.shape, a.dtype) for a in arrs],
        scratch_shapes=[pltpu.SemaphoreType.DMA((n, N_CHIP - 1)), pltpu.SemaphoreType.DMA((n, N_CHIP - 1)),
                        pltpu.SemaphoreType.DMA((n,))],
    )(*arrs)


def _pair_sum(mine, got, core, *, name):
    s_, r, c = got.shape
    tr = r if r <= 256 else 256

    def body(core_ref, a_ref, b_ref, o_ref):
        o_ref[...] = a_ref[...] + b_ref[...]

    quarter = pl.BlockSpec((s_, tr, c), lambda i, core_ref: (0, i, 0))
    return _pallas(
        body, name=name, out_shape=jax.ShapeDtypeStruct(got.shape, got.dtype),
        grid_spec=pltpu.PrefetchScalarGridSpec(
            num_scalar_prefetch=1, grid=(r // tr,),
            in_specs=[pl.BlockSpec((s_, None, tr, c), lambda i, core_ref: (0, core_ref[0], i, 0)), quarter],
            out_specs=quarter),
        compiler_params=_cparams("parallel"),
    )(core, mine.reshape(s_, 2, r, c), got)


def _reduce_scatter(arrs, names):
    got = _pair_exchange(arrs, name="pair_exchange")
    core = lax.axis_index("c").astype(jnp.int32).reshape(1)
    sums = [_pair_sum(a, g_, core, name="pair_sum_" + n) for a, g_, n in zip(arrs, got, names)]
    return _chip_exchange(sums, name="chip_exchange")


def _adamw(parts, w, m, v, *, name):
    s, r, c = parts.shape
    tr = r if r * c <= 256 * D_MODEL else 256

    def body(p_ref, w_ref, m_ref, v_ref, g_ref, d_ref, nm_ref, nv_ref):
        g = p_ref[0]
        for j in range(1, s):
            g = g + p_ref[j]
        nm = ADAM_B1 * m_ref[...] + (1.0 - ADAM_B1) * g
        nv = ADAM_B2 * v_ref[...] + (1.0 - ADAM_B2) * (g * g)
        m_hat = nm / (1.0 - ADAM_B1 ** ADAM_STEP)
        v_hat = nv / (1.0 - ADAM_B2 ** ADAM_STEP)
        g_ref[...] = g
        d_ref[...] = -ADAM_LR * (m_hat / (jnp.sqrt(v_hat) + ADAM_EPS) + ADAM_WD * w_ref[...])
        nm_ref[...] = nm
        nv_ref[...] = nv

    spec = pl.BlockSpec((tr, c), lambda i: (i, 0))
    one = jax.ShapeDtypeStruct((r, c), F32)
    return _pallas(
        body, name=name, grid=(r // tr,), in_specs=[pl.BlockSpec((s, tr, c), lambda i: (0, i, 0)), spec, spec, spec],
        out_specs=[spec] * 4, out_shape=[one] * 4, compiler_params=_cparams("parallel"),
    )(parts, w, m, v)


def _cols_to_blocks(g, n_blk):
    r = g.shape[0]
    return g.reshape(r, n_blk, g.shape[1] // n_blk).transpose(1, 0, 2)


def _blocks_to_cols(gathered):
    n_blk, r, c = gathered.shape
    return gathered.transpose(1, 0, 2).reshape(r, n_blk * c)


def kernel(x, p, a_norm_pre, a_norm_post, a_w_in, a_lam_re, a_lam_im, a_log_dt, a_b_re, a_b_im, a_c_re, a_c_im, a_d_skip, a_w_glu, a_b_glu, a_w_out, kv_norm, w_kv, b_norm_pre, b_norm_post, b_w_in, b_w_out, ple_w_proj, ple_w_gate, loss_target, m_a_norm_pre, m_a_norm_post, m_a_w_in, m_a_lam_re, m_a_lam_im, m_a_log_dt, m_a_b_re, m_a_b_im, m_a_c_re, m_a_c_im, m_a_d_skip, m_a_w_glu, m_a_b_glu, m_a_w_out, m_kv_norm, m_w_kv, m_b_norm_pre, m_b_norm_post, m_b_w_in, m_b_w_out, m_ple_w_proj, m_ple_w_gate, v_a_norm_pre, v_a_norm_post, v_a_w_in, v_a_lam_re, v_a_lam_im, v_a_log_dt, v_a_b_re, v_a_b_im, v_a_c_re, v_a_c_im, v_a_d_skip, v_a_w_glu, v_a_b_glu, v_a_w_out, v_kv_norm, v_w_kv, v_b_norm_pre, v_b_norm_post, v_b_w_in, v_b_w_out, v_ple_w_proj, v_ple_w_gate):
    weights = dict(a_norm_pre=a_norm_pre, a_norm_post=a_norm_post, a_w_in=a_w_in, a_lam_re=a_lam_re, a_lam_im=a_lam_im, a_log_dt=a_log_dt, a_b_re=a_b_re, a_b_im=a_b_im, a_c_re=a_c_re, a_c_im=a_c_im, a_d_skip=a_d_skip, a_w_glu=a_w_glu, a_b_glu=a_b_glu, a_w_out=a_w_out, kv_norm=kv_norm, w_kv=w_kv, b_norm_pre=b_norm_pre, b_norm_post=b_norm_post, b_w_in=b_w_in, b_w_out=b_w_out, ple_w_proj=ple_w_proj, ple_w_gate=ple_w_gate)
    mom_m = dict(a_norm_pre=m_a_norm_pre, a_norm_post=m_a_norm_post, a_w_in=m_a_w_in, a_lam_re=m_a_lam_re, a_lam_im=m_a_lam_im, a_log_dt=m_a_log_dt, a_b_re=m_a_b_re, a_b_im=m_a_b_im, a_c_re=m_a_c_re, a_c_im=m_a_c_im, a_d_skip=m_a_d_skip, a_w_glu=m_a_w_glu, a_b_glu=m_a_b_glu, a_w_out=m_a_w_out, kv_norm=m_kv_norm, w_kv=m_w_kv, b_norm_pre=m_b_norm_pre, b_norm_post=m_b_norm_post, b_w_in=m_b_w_in, b_w_out=m_b_w_out, ple_w_proj=m_ple_w_proj, ple_w_gate=m_ple_w_gate)
    mom_v = dict(a_norm_pre=v_a_norm_pre, a_norm_post=v_a_norm_post, a_w_in=v_a_w_in, a_lam_re=v_a_lam_re, a_lam_im=v_a_lam_im, a_log_dt=v_a_log_dt, a_b_re=v_a_b_re, a_b_im=v_a_b_im, a_c_re=v_a_c_re, a_c_im=v_a_c_im, a_d_skip=v_a_d_skip, a_w_glu=v_a_w_glu, a_b_glu=v_a_b_glu, a_w_out=v_a_w_out, kv_norm=v_kv_norm, w_kv=v_w_kv, b_norm_pre=v_b_norm_pre, b_norm_post=v_b_norm_post, b_w_in=v_b_w_in, b_w_out=v_b_w_out, ple_w_proj=v_ple_w_proj, ple_w_gate=v_ple_w_gate)
    names = list(weights)

    t = x.shape[1]
    x0 = x[0]
    p0, p1 = p[0, 0], p[1, 0]
    tgt = loss_target[0]

    mats = dict(a_w_in=a_w_in[0], a_w_glu=a_w_glu[0], a_w_out=a_w_out[0], w_kv=w_kv, b_w_in=b_w_in[0],
                b_w_out=b_w_out[0], ple_w_proj=ple_w_proj.reshape(2 * 256, LANES),
                ple_w_gate=ple_w_gate.reshape(2 * LANES, D_MODEL))
    vec_names = ["a_norm_pre", "a_norm_post", "a_d_skip", "a_b_glu"]
    vec_pack = jnp.concatenate([weights[n] for n in vec_names], axis=0)
    gathered = _gather([m_.astype(BF16) for m_ in mats.values()] + [vec_pack], name="gather_weights")
    gw = dict(zip(mats, gathered[:-1]))
    vec_full = gathered[-1].transpose(1, 0, 2).reshape(len(vec_names), D_MODEL)
    g_apre, g_apost, d_skip, b_glu = (vec_full[i:i + 1] for i in range(4))
    w_u, w_ga = _blocks_to_cols(gw["a_w_in"][:4]), _blocks_to_cols(gw["a_w_in"][4:])
    w_glu = gw["a_w_glu"].reshape(D_MODEL, D_MODEL)
    w_out = gw["a_w_out"].reshape(D_MODEL, D_MODEL)
    w_k, w_v = _blocks_to_cols(gw["w_kv"][:4]), _blocks_to_cols(gw["w_kv"][4:])
    w_q, w_gb = _blocks_to_cols(gw["b_w_in"][:4]), _blocks_to_cols(gw["b_w_in"][4:])
    w_bout = gw["b_w_out"].reshape(D_MODEL, D_MODEL)
    wp = gw["ple_w_proj"].reshape(N_DEV, 2, 256, LANES)
    w_p0, w_p1 = _blocks_to_cols(wp[:, 0]), _blocks_to_cols(wp[:, 1])
    wg = gw["ple_w_gate"].reshape(N_DEV, 2, LANES, D_MODEL)
    w_g0, w_g1 = wg[:, 0].reshape(D_MODEL, D_MODEL), wg[:, 1].reshape(D_MODEL, D_MODEL)
    g_kv, g_bpre, g_bpost = kv_norm.reshape(1, D_MODEL), b_norm_pre, b_norm_post

    s5_params = (a_lam_re[0], a_lam_im[0], a_log_dt[0], a_b_re[0], a_b_im[0], a_c_re[0], a_c_im[0])
    (a_re, a_im, bbd, ccd), s5_vjp = jax.vjp(_s5_discretize, *s5_params)
    bbd_bf, ccd_bf = bbd.astype(BF16), ccd.astype(BF16)

    f_a2 = lambda ys, u, d: (_gelu(ys + d * u),)
    f_a3 = lambda g, s, ga, b: (g * _sigmoid(s + b) * _silu(ga),)
    f_rms = lambda y, g: (_rms(y, g),)
    f_k = lambda x2, gk, gp: (_rms(x2, gk), _rms(x2, gp))
    f_b3 = lambda o, gb: (o * _silu(gb),)

    def f_a_in(x0_, g_, wu_, wga_):
        h = _rms(x0_, g_).astype(BF16)
        return h, _dot(h, wu_), _dot(h, wga_)

    h0, u, ga = _ew(f_a_in, [x0], [g_apre], [BF16, F32, F32], mats=[w_u, w_ga], name="a_norm_proj")
    ys, hb = _s5_fwd(u, bbd_bf, ccd_bf, a_re, a_im)

    def f_a_glu(ys_, u_, d_, w):
        g = f_a2(ys_, u_, d_)[0].astype(BF16)
        return g, _dot(g, w)

    g_act, s_glu = _ew(f_a_glu, [ys, u], [d_skip], [BF16, F32], mats=[w_glu], name="a_gelu_glu")
    def f_a_out(ys_, u_, s_, ga_, x0_, d_, b_, g_, w):
        (y3_,) = f_a3(_gelu(ys_ + d_ * u_), s_, ga_, b_)
        y3_ = y3_.astype(BF16)
        y4_ = _dot(y3_, w)
        return y3_, y4_, x0_ + _rms(y4_, g_)

    y3, y4, x1 = _ew(f_a_out, [ys, u, s_glu, ga, x0], [d_skip, b_glu, g_apost], [BF16, F32, F32], mats=[w_out],
                     name="a_gate_out_norm")

    def f_ple(xin, p_, wg_, wp_):
        pg, pp = _dot(xin, wg_), _dot(p_, wp_)
        return xin + _sigmoid(pg) * pp, pg, pp

    x2, pg0, pp0 = _ew(f_ple, [x1, p0], [], [F32, F32, F32], mats=[w_g0, w_p0], name="ple0")
    def f_b_in(x2_, gk, gp, wk_, wv_, wq_, wgb_):
        kvn_, h1_ = (t_.astype(BF16) for t_ in f_k(x2_, gk, gp))
        return kvn_, h1_, _dot(kvn_, wk_), _dot(kvn_, wv_), _dot(h1_, wq_), _dot(h1_, wgb_)

    kvn, h1, k_, v_, q_, gb = _ew(f_b_in, [x2], [g_kv, g_bpre], [BF16, BF16, BF16, BF16, BF16, F32],
                                  mats=[w_k, w_v, w_q, w_gb], name="b_norms_proj")
    o, o_low = _attn_fwd(q_, k_, v_)

    def f_b_out(o_, gb_, x2_, g_, w):
        y5in_ = f_b3(o_, gb_)[0].astype(BF16)
        y5_ = _dot(y5in_, w)
        return y5in_, y5_, x2_ + _rms(y5_, g_)

    y5in, y5, x3 = _ew(f_b_out, [o, gb, x2], [g_bpost], [BF16, F32, F32], mats=[w_bout], name="b_gate_out_norm")

    def f_ple_loss(xin, p_, tg, wg_, wp_):
        pg, pp = _dot(xin, wg_), _dot(p_, wp_)
        err = xin + _sigmoid(pg) * pp - tg
        return err * (1.0 / D_MODEL), pg, pp, (0.5 / D_MODEL) * jnp.sum(err * err).reshape(1, 1)

    dx4, pg1, pp1, loss_part = _ew(f_ple_loss, [x3, p1, tgt], [], [F32, F32, F32], mats=[w_g1, w_p1], n_red=1,
                                   name="ple1_loss")
    loss = lax.psum(loss_part[0, 0], ("x", "y", "c"))

    grads = {}
    def f_ple_bwd(dx, pg, pp, wg_):
        sg = _sigmoid(pg)
        dpg = (dx * pp * sg * (1.0 - sg)).astype(BF16)
        return dx + _dot_t(dpg, wg_), dpg, dx * sg

    def f_norm_bwd_mm(y, dx, g_, w):
        _, vjp = jax.vjp(_rms, y, g_)
        dy, dg = vjp(dx)
        dy = dy.astype(BF16)
        return _dot_t(dy, w), dy, dg

    dx3, dpg1, dpp1 = _ew(f_ple_bwd, [dx4, pg1, pp1], [], [F32, BF16, BF16], mats=[w_g1], name="ple1_bwd")
    d_wg1 = _mm(x3, dpg1, ta=True, name="ple1_gate_dw")
    d_wp1 = _mm(p1, dpp1, ta=True, name="ple1_proj_dw")
    def f_b_out_bwd(y, dx, o_, gb_, g_, w):
        dyin, dy, dg = f_norm_bwd_mm(y, dx, g_, w)
        _, vjp = jax.vjp(lambda a, b: f_b3(a, b)[0], o_, gb_)
        do_, dgb_ = vjp(dyin)
        return dy, do_, dgb_, dg

    dy5, do, dgb, d_gbpost = _ew(f_b_out_bwd, [y5, dx3, o, gb], [g_bpost], [BF16, F32, BF16], mats=[w_bout], n_red=1,
                                 name="b_out_bwd")
    grads["b_w_out"] = _mm(y5in, dy5, ta=True, name="b_out_dw").reshape(N_DEV, LANES, D_MODEL)
    dq, dk, dv = _attn_bwd(q_, k_, v_, o, o_low, do)
    grads["b_w_in"] = jnp.concatenate([_cols_to_blocks(_mm(h1, dq, ta=True, name="b_proj_q_dw"), 4),
                                       _cols_to_blocks(_mm(h1, dgb, ta=True, name="b_proj_gate_dw"), 4)])
    grads["w_kv"] = jnp.concatenate([_cols_to_blocks(_mm(kvn, dk, ta=True, name="kv_k_dw"), 4),
                                     _cols_to_blocks(_mm(kvn, dv, ta=True, name="kv_v_dw"), 4)])
    def f_b_in_bwd(x2_, dx, dk_, dv_, dq_, dgb_, gk, gp, wk_, wv_, wq_, wgb_):
        dkvn = _dot_t(dk_, wk_) + _dot_t(dv_, wv_)
        dh1 = _dot_t(dq_, wq_) + _dot_t(dgb_, wgb_)
        _, vjp = jax.vjp(f_k, x2_, gk, gp)
        dx2_, dgk, dgp = vjp((dkvn, dh1))
        return dx + dx2_, dgk, dgp

    dx2, d_gkv, d_gbpre = _ew(f_b_in_bwd, [x2, dx3, dk, dv, dq, dgb], [g_kv, g_bpre], [F32],
                              mats=[w_k, w_v, w_q, w_gb], n_red=2, name="b_norms_proj_bwd")
    dx1, dpg0, dpp0 = _ew(f_ple_bwd, [dx2, pg0, pp0], [], [F32, BF16, BF16], mats=[w_g0], name="ple0_bwd")
    d_wg0 = _mm(x1, dpg0, ta=True, name="ple0_gate_dw")
    d_wp0 = _mm(p0, dpp0, ta=True, name="ple0_proj_dw")
    grads["ple_w_gate"] = jnp.stack([d_wg0.reshape(N_DEV, LANES, D_MODEL), d_wg1.reshape(N_DEV, LANES, D_MODEL)],
                                    axis=1).reshape(N_DEV, 2 * LANES, D_MODEL)
    grads["ple_w_proj"] = jnp.stack([_cols_to_blocks(d_wp0, N_DEV), _cols_to_blocks(d_wp1, N_DEV)],
                                    axis=1).reshape(N_DEV, 2 * 256, LANES)
    dy3, dy4, d_gapost = _ew(f_norm_bwd_mm, [y4, dx1], [g_apost], [F32, BF16], mats=[w_out], n_red=1,
                             name="a_norm_post_bwd")
    grads["a_w_out"] = _mm(y3, dy4, ta=True, name="a_out_dw").reshape(N_DEV, LANES, D_MODEL)
    def f_a_gate_bwd(ys_, u_, s_, ga_, ct, d_, b_, w):
        g, vjp_gelu = jax.vjp(lambda a, b, c: f_a2(a, b, c)[0], ys_, u_, d_)
        _, vjp_gate = jax.vjp(lambda a, b, c, e: f_a3(a, b, c, e)[0], g, s_, ga_, b_)
        dg, ds_, dga_, db = vjp_gate(ct)
        ds_ = ds_.astype(BF16)
        dys_, du_, dd = vjp_gelu(dg + _dot_t(ds_, w))
        return ds_, dga_, dys_, du_, db, dd

    ds, dga, dys, du_elem, d_bglu, d_dskip = _ew(f_a_gate_bwd, [ys, u, s_glu, ga, dy3], [d_skip, b_glu],
                                                 [BF16, BF16, F32, F32], mats=[w_glu], n_red=2, name="a_gate_bwd")
    grads["a_w_glu"] = _mm(g_act, ds, ta=True, name="a_glu_dw").reshape(N_DEV, LANES, D_MODEL)
    du, d_bbd, d_ccd, d_are, d_aim = _s5_bwd(u, dys, du_elem, hb, bbd_bf, ccd_bf, a_re, a_im)
    grads["a_w_in"] = jnp.concatenate([_cols_to_blocks(_mm(h0, du, ta=True, name="a_proj_u_dw"), 4),
                                       _cols_to_blocks(_mm(h0, dga, ta=True, name="a_proj_gate_dw"), 4)])
    def f_a_in_bwd(x0_, dx, du_, dga_, g_, wu_, wga_):
        dh0 = _dot_t(du_, wu_) + _dot_t(dga_, wga_)
        _, vjp = jax.vjp(_rms, x0_, g_)
        dx0, dg = vjp(dh0)
        return dx + dx0, dg

    grad_x, d_gapre = _ew(f_a_in_bwd, [x0, dx1, du, dga], [g_apre], [F32], mats=[w_u, w_ga], n_red=1,
                          name="a_norm_proj_bwd")

    d_s5 = s5_vjp((d_are, d_aim, d_bbd, d_ccd))
    small = dict(zip(["a_lam_re", "a_lam_im", "a_log_dt", "a_b_re", "a_b_im", "a_c_re", "a_c_im"], d_s5))
    small.update(kv_norm=d_gkv, b_norm_pre=d_gbpre, b_norm_post=d_gbpost)
    small_names = list(small)
    small_flat = jnp.concatenate([small[n].reshape(-1) for n in small_names])
    small_pack = jnp.pad(small_flat, (0, -small_flat.size % (8 * LANES))).reshape(-1, LANES)
    vec_grads = jnp.concatenate([d_gapre, d_gapost, d_dskip, d_bglu], axis=0)
    vec_scatter = vec_grads.reshape(len(vec_names), N_DEV, LANES).transpose(1, 0, 2)
    mat_names = list(mats)
    scattered = _reduce_scatter([grads[n] for n in mat_names] + [vec_scatter], mat_names + ["vectors"])
    (small_all,) = _gather([small_pack], name="gather_small_grads")

    out_g, out_d, out_m, out_v = {}, {}, {}, {}

    def update(n, parts):
        shape = weights[n].shape
        rc = parts.shape[1:]
        res = _adamw(parts, weights[n].reshape(rc), mom_m[n].reshape(rc), mom_v[n].reshape(rc), name="adamw_" + n)
        out_g[n], out_d[n], out_m[n], out_v[n] = (r.reshape(shape) for r in res)

    for n, parts in zip(mat_names, scattered[:-1]):
        update(n, parts)
    for i, n in enumerate(vec_names):
        update(n, scattered[-1][:, i:i + 1, :])
    off = 0
    for n in small_names:
        size = small[n].size
        rc = (size // LANES, LANES) if size % LANES == 0 else (1, size)
        update(n, small_all.reshape(N_DEV, -1)[:, off:off + size].reshape((N_DEV,) + rc))
        off += size

    return (loss, grad_x[None], *[out_g[n] for n in names], *[out_d[n] for n in names],
            *[out_m[n] for n in names], *[out_v[n] for n in names])
```

```python
import functools
import math

import jax
import jax.numpy as jnp
from jax import lax
from jax.experimental import pallas as pl
from jax.experimental.pallas import tpu as pltpu

F32 = jnp.float32
BF16 = jnp.bfloat16

N_DEV = 8
D_MODEL = 1024
GROUP_SIZE = 16
N_GROUPS = D_MODEL // GROUP_SIZE
STATE = 64
HEAD_DIM = 64
EPS = 1e-6
LANES = 128
GROUPS_PER_BLOCK = LANES // GROUP_SIZE
N_GBLK = N_GROUPS // GROUPS_PER_BLOCK
STATE_COLS = 2 * GROUPS_PER_BLOCK * STATE
N_SLAB = STATE_COLS // LANES
VMEM_LIMIT = 56 * 1024 * 1024
EXP_UNDERFLOW = -104.0
ATTN_LANES = 256
MM_TILE = 1024
EW_ROWS_BYTES = 24 * 1024 * 1024

ADAM_LR = 0.001
ADAM_B1 = 0.9
ADAM_B2 = 0.999
ADAM_EPS = 1e-08
ADAM_WD = 0.01
ADAM_STEP = 10


def _pallas(body, **kw):
    return pl.pallas_call(body, **kw)


def _cparams(*sem):
    return pltpu.CompilerParams(dimension_semantics=sem, vmem_limit_bytes=VMEM_LIMIT)


def _mm(a, b, *, ta=False, tb=False, add=None, out_dtype=F32, name):
    m, k = (a.shape[1], a.shape[0]) if ta else a.shape
    n = b.shape[0] if tb else b.shape[1]
    tm, tn, tk = min(m, MM_TILE), min(n, MM_TILE), min(k, MM_TILE)
    nk = k // tk
    grid = (m // tm, n // tn, nk)
    a_spec = pl.BlockSpec((tk, tm), lambda i, j, l: (l, i)) if ta else pl.BlockSpec((tm, tk), lambda i, j, l: (i, l))
    b_spec = pl.BlockSpec((tn, tk), lambda i, j, l: (j, l)) if tb else pl.BlockSpec((tk, tn), lambda i, j, l: (l, j))
    o_spec = pl.BlockSpec((tm, tn), lambda i, j, l: (i, j))
    dims = (((0 if ta else 1,), (1 if tb else 0,)), ((), ()))
    has_add = add is not None
    assert nk == 1 or (out_dtype == F32 and not has_add)

    def body(*refs):
        a_ref, b_ref, o_ref = refs[0], refs[1], refs[-1]
        part = lax.dot_general(a_ref[...].astype(BF16), b_ref[...].astype(BF16), dims, preferred_element_type=F32)
        if nk == 1:
            if has_add:
                part = part + refs[2][...].astype(F32)
            o_ref[...] = part.astype(out_dtype)
        else:
            l = pl.program_id(2)

            @pl.when(l == 0)
            def _():
                o_ref[...] = part

            @pl.when(l > 0)
            def _():
                o_ref[...] += part

    ins = [a, b] + ([add] if has_add else [])
    specs = [a_spec, b_spec] + ([o_spec] if has_add else [])
    return _pallas(
        body, name=name, grid=grid, in_specs=specs, out_specs=o_spec,
        out_shape=jax.ShapeDtypeStruct((m, n), out_dtype),
        compiler_params=_cparams("parallel", "parallel", "arbitrary"),
    )(*ins)


def _mm_dw_cols(a, bs, n_blk, *, name):
    t, m = a.shape
    n = bs[0].shape[1]
    c = n // n_blk
    tk = min(t, MM_TILE)
    nb = len(bs)

    def body(*refs):
        a_ref, b_refs, o_ref = refs[0], refs[1:1 + nb], refs[-1]
        l = pl.program_id(0)

        @pl.when(l == 0)
        def _():
            o_ref[...] = jnp.zeros_like(o_ref)

        a_bf = a_ref[...].astype(BF16)
        for bi, b_ref in enumerate(b_refs):
            part = lax.dot_general(a_bf, b_ref[...].astype(BF16), (((0,), (0,)), ((), ())), preferred_element_type=F32)
            for j in range(n_blk):
                o_ref[bi * n_blk + j] += part[:, j * c:(j + 1) * c]

    return _pallas(
        body, name=name, grid=(t // tk,),
        in_specs=[pl.BlockSpec((tk, m), lambda l: (l, 0))] + [pl.BlockSpec((tk, n), lambda l: (l, 0))] * nb,
        out_specs=pl.BlockSpec((nb * n_blk, m, c), lambda l: (0, 0, 0)),
        out_shape=jax.ShapeDtypeStruct((nb * n_blk, m, c), F32),
        compiler_params=_cparams("arbitrary"),
    )(a, *bs)


def _row_spec(arr, tm):
    return pl.BlockSpec((tm, arr.shape[1]), lambda i: (i, 0))


def _full_spec(arr):
    return pl.BlockSpec(arr.shape, lambda i: (0,) * arr.ndim)


def _ew_tile_rows(t, row_bytes):
    tm = 1024
    while tm > 8 and (t % tm or 2 * tm * row_bytes > EW_ROWS_BYTES):
        tm //= 2
    return tm


def _ew(fn, rows, vecs, out_dtypes, *, mats=(), n_red=0, name):
    t = rows[0].shape[0]
    nr, nv, nm = len(rows), len(vecs), len(mats)

    def tile_shapes(tm):
        return jax.eval_shape(fn, *[jax.ShapeDtypeStruct((tm, r.shape[1]), F32) for r in rows],
                              *[jax.ShapeDtypeStruct(v.shape, F32) for v in vecs],
                              *[jax.ShapeDtypeStruct(m_.shape, m_.dtype) for m_ in mats])

    probe = tile_shapes(8)
    row_bytes = sum(r.shape[1] * r.dtype.itemsize for r in rows)
    row_bytes += sum(s.shape[1] * jnp.dtype(dt).itemsize for s, dt in zip(probe, out_dtypes))
    tm = _ew_tile_rows(t, row_bytes)
    shapes = tile_shapes(tm)
    n_out = len(shapes) - n_red

    def body(*refs):
        ins = [r[...].astype(F32) for r in refs[:nr + nv]] + [r[...] for r in refs[nr + nv:nr + nv + nm]]
        outs = fn(*ins)
        o_refs = refs[nr + nv + nm:]
        for o_ref, o in zip(o_refs[:n_out], outs[:n_out]):
            o_ref[...] = o.astype(o_ref.dtype)
        i = pl.program_id(0)
        for o_ref, o in zip(o_refs[n_out:], outs[n_out:]):
            @pl.when(i == 0)
            def _(o_ref=o_ref, o=o):
                o_ref[...] = o

            @pl.when(i > 0)
            def _(o_ref=o_ref, o=o):
                o_ref[...] += o

    out_shape = [jax.ShapeDtypeStruct((t, s.shape[1]), dt) for s, dt in zip(shapes[:n_out], out_dtypes)]
    out_shape += [jax.ShapeDtypeStruct(s.shape, F32) for s in shapes[n_out:]]
    out_specs = [pl.BlockSpec((tm, s.shape[1]), lambda i: (i, 0)) for s in shapes[:n_out]]
    out_specs += [pl.BlockSpec(s.shape, lambda i: (0, 0)) for s in shapes[n_out:]]
    return _pallas(
        body, name=name, grid=(t // tm,),
        in_specs=[_row_spec(r, tm) for r in rows] + [_full_spec(v) for v in list(vecs) + list(mats)],
        out_specs=out_specs, out_shape=out_shape,
        compiler_params=_cparams("arbitrary" if n_red else "parallel"),
    )(*rows, *vecs, *mats)


def _dot(a, w):
    a = a.astype(BF16)
    if w.ndim == 3:
        return jnp.concatenate([jnp.dot(a, w[j], preferred_element_type=F32) for j in range(w.shape[0])], axis=1)
    return jnp.dot(a, w, preferred_element_type=F32)


def _dot_t(a, w):
    a = a.astype(BF16)
    nt = (((1,), (1,)), ((), ()))
    if w.ndim == 3:
        c = w.shape[2]
        return sum(lax.dot_general(a[:, j * c:(j + 1) * c], w[j], nt, preferred_element_type=F32) for j in range(w.shape[0]))
    return lax.dot_general(a, w, nt, preferred_element_type=F32)


def _ew_bwd(fn, rows, vecs, cts, row_wrt, vec_wrt, out_dtypes, *, name):
    nr, nv, nc = len(rows), len(vecs), len(cts)

    def grad_fn(*tiles):
        ins = list(tiles[:nr + nv])
        ct = tiles[nr + nv:]
        wrt = list(row_wrt) + [nr + j for j in vec_wrt]

        def f(*sel):
            full = list(ins)
            for idx, s in zip(wrt, sel):
                full[idx] = s
            return tuple(fn(*full))

        _, vjp = jax.vjp(f, *[ins[idx] for idx in wrt])
        return vjp(tuple(ct))

    return _ew_reordered(grad_fn, rows, vecs, cts, out_dtypes, len(vec_wrt), name)


def _ew_reordered(grad_fn, rows, vecs, cts, out_dtypes, n_red, name):
    nr, nv = len(rows), len(vecs)

    def fn(*tiles):
        r, c, v = tiles[:nr], tiles[nr:nr + len(cts)], tiles[nr + len(cts):]
        return grad_fn(*r, *v, *c)

    return _ew(fn, list(rows) + list(cts), vecs, out_dtypes, n_red=n_red, name=name)


def _rms(x, g):
    return x * lax.rsqrt(jnp.mean(x * x, axis=-1, keepdims=True) + EPS) * g


def _sigmoid(x):
    return 1.0 / (1.0 + jnp.exp(-x))


def _silu(x):
    return x * _sigmoid(x)


def _gelu(x):
    return 0.5 * x * (1.0 + jnp.tanh(math.sqrt(2.0 / math.pi) * (x + 0.044715 * x * x * x)))


def _token_rows(t):
    return pl.ds(pl.multiple_of(t * N_GBLK, N_GBLK), N_GBLK)


def _block_rows(gb, tt):
    return pl.ds(gb, tt, stride=N_GBLK)


def _s5_expand(u_bf, bbd_ref, xs_ref, tt):
    for gb in range(N_GBLK):
        xg = jnp.dot(u_bf[:, gb * LANES:(gb + 1) * LANES], bbd_ref[gb], preferred_element_type=F32)
        for l in range(N_SLAB):
            xs_ref[l, _block_rows(gb, tt), :] = xg[:, l * LANES:(l + 1) * LANES]


def _s5_forward_scan(xs_ref, ar, ai, state, tt):
    half = N_SLAB // 2

    def step(t, st):
        new = [None] * N_SLAB
        for l in range(half):
            sr, si = st[l], st[half + l]
            rows = _token_rows(t)
            nr_ = ar[l] * sr - ai[l] * si + xs_ref[l, rows, :]
            ni_ = ar[l] * si + ai[l] * sr + xs_ref[half + l, rows, :]
            xs_ref[l, rows, :] = nr_
            xs_ref[half + l, rows, :] = ni_
            new[l], new[half + l] = nr_, ni_
        return tuple(new)

    return lax.fori_loop(0, tt, step, tuple(state), unroll=4)


def _slab_rows(ref, gb, tt):
    return jnp.concatenate([ref[l, _block_rows(gb, tt), :] for l in range(N_SLAB)], axis=1)


def _s5_fwd(u, bbd, ccd, a_re, a_im, *, tt=256):
    t = u.shape[0]
    nt = t // tt
    half = N_SLAB // 2

    def body(u_ref, bbd_ref, ccd_ref, ar_ref, ai_ref, y_ref, hb_ref, xs_ref, st_ref):
        i = pl.program_id(0)

        @pl.when(i == 0)
        def _():
            st_ref[...] = jnp.zeros_like(st_ref)

        hb_ref[0] = st_ref[...]
        _s5_expand(u_ref[...].astype(BF16), bbd_ref, xs_ref, tt)
        ar = [ar_ref[l] for l in range(half)]
        ai = [ai_ref[l] for l in range(half)]
        last = _s5_forward_scan(xs_ref, ar, ai, [st_ref[l] for l in range(N_SLAB)], tt)
        for l in range(N_SLAB):
            st_ref[l] = last[l]
        for gb in range(N_GBLK):
            sg = _slab_rows(xs_ref, gb, tt).astype(BF16)
            y_ref[:, gb * LANES:(gb + 1) * LANES] = jnp.dot(sg, ccd_ref[gb], preferred_element_type=F32)

    return _pallas(
        body, name="s5_fwd", grid=(nt,),
        in_specs=[pl.BlockSpec((tt, D_MODEL), lambda i: (i, 0)), _full_spec(bbd), _full_spec(ccd),
                  _full_spec(a_re), _full_spec(a_im)],
        out_specs=[pl.BlockSpec((tt, D_MODEL), lambda i: (i, 0)),
                   pl.BlockSpec((1, N_SLAB, N_GBLK, LANES), lambda i: (i, 0, 0, 0))],
        out_shape=[jax.ShapeDtypeStruct((t, D_MODEL), F32), jax.ShapeDtypeStruct((nt, N_SLAB, N_GBLK, LANES), F32)],
        scratch_shapes=[pltpu.VMEM((N_SLAB, N_GBLK * tt, LANES), F32), pltpu.VMEM((N_SLAB, N_GBLK, LANES), F32)],
        compiler_params=_cparams("arbitrary"),
    )(u, bbd, ccd, a_re, a_im)


def _s5_bwd(u, dy, du_add, hb, bbd, ccd, a_re, a_im, *, tt=256):
    t = u.shape[0]
    nt = t // tt
    half = N_SLAB // 2
    rev = lambda i: (nt - 1 - i, 0)

    def body(u_ref, dy_ref, dua_ref, hb_ref, bbd_ref, ccd_ref, ar_ref, ai_ref,
             du_ref, dbbd_ref, dccd_ref, dar_ref, dai_ref, xs_ref, es_ref, lam_ref):
        i = pl.program_id(0)

        @pl.when(i == 0)
        def _():
            lam_ref[...] = jnp.zeros_like(lam_ref)
            dbbd_ref[...] = jnp.zeros_like(dbbd_ref)
            dccd_ref[...] = jnp.zeros_like(dccd_ref)
            dar_ref[...] = jnp.zeros_like(dar_ref)
            dai_ref[...] = jnp.zeros_like(dai_ref)

        u_bf = u_ref[...].astype(BF16)
        dy_bf = dy_ref[...].astype(BF16)
        _s5_expand(u_bf, bbd_ref, xs_ref, tt)
        ar = [ar_ref[l] for l in range(half)]
        ai = [ai_ref[l] for l in range(half)]
        start = [hb_ref[0, l] for l in range(N_SLAB)]
        _s5_forward_scan(xs_ref, ar, ai, start, tt)
        for gb in range(N_GBLK):
            eg = lax.dot_general(dy_bf[:, gb * LANES:(gb + 1) * LANES], ccd_ref[gb], (((1,), (1,)), ((), ())),
                                 preferred_element_type=F32)
            for l in range(N_SLAB):
                es_ref[l, _block_rows(gb, tt), :] = eg[:, l * LANES:(l + 1) * LANES]

        def lam_step(k, lam):
            rows = _token_rows(tt - 1 - k)
            new_lam = [None] * N_SLAB
            for l in range(half):
                lr = es_ref[l, rows, :] + ar[l] * lam[l] + ai[l] * lam[half + l]
                li = es_ref[half + l, rows, :] - ai[l] * lam[l] + ar[l] * lam[half + l]
                es_ref[l, rows, :] = lr
                es_ref[half + l, rows, :] = li
                new_lam[l], new_lam[half + l] = lr, li
            return tuple(new_lam)

        lam = lax.fori_loop(0, tt, lam_step, tuple(lam_ref[l] for l in range(N_SLAB)), unroll=4)
        for l in range(N_SLAB):
            lam_ref[l] = lam[l]

        def over_tokens(prod):
            return jnp.sum(prod.reshape(tt - 1, N_GBLK, LANES), axis=0)

        later_rows, earlier_rows = pl.ds(N_GBLK, (tt - 1) * N_GBLK), pl.ds(0, (tt - 1) * N_GBLK)
        first = pl.ds(0, N_GBLK)
        for l in range(half):
            lr, li = es_ref[l, later_rows, :], es_ref[half + l, later_rows, :]
            pr, pi = xs_ref[l, earlier_rows, :], xs_ref[half + l, earlier_rows, :]
            lr0, li0 = es_ref[l, first, :], es_ref[half + l, first, :]
            dar_ref[l] += over_tokens(lr * pr + li * pi) + lr0 * start[l] + li0 * start[half + l]
            dai_ref[l] += over_tokens(li * pr - lr * pi) + li0 * start[l] - lr0 * start[half + l]

        for gb in range(N_GBLK):
            cols = slice(gb * LANES, (gb + 1) * LANES)
            lg = _slab_rows(es_ref, gb, tt).astype(BF16)
            sg = _slab_rows(xs_ref, gb, tt).astype(BF16)
            du_ref[:, cols] = (dua_ref[:, cols] + lax.dot_general(
                lg, bbd_ref[gb], (((1,), (1,)), ((), ())), preferred_element_type=F32)).astype(du_ref.dtype)
            dbbd_ref[gb] += lax.dot_general(u_bf[:, cols], lg, (((0,), (0,)), ((), ())), preferred_element_type=F32)
            dccd_ref[gb] += lax.dot_general(sg, dy_bf[:, cols], (((0,), (0,)), ((), ())), preferred_element_type=F32)

    row = pl.BlockSpec((tt, D_MODEL), rev)
    small = pl.BlockSpec((half, N_GBLK, LANES), lambda i: (0, 0, 0))
    return _pallas(
        body, name="s5_bwd", grid=(nt,),
        in_specs=[row, row, row, pl.BlockSpec((1, N_SLAB, N_GBLK, LANES), lambda i: (nt - 1 - i, 0, 0, 0)),
                  _full_spec(bbd), _full_spec(ccd), _full_spec(a_re), _full_spec(a_im)],
        out_specs=[row, _full_spec(bbd), _full_spec(ccd), small, small],
        out_shape=[jax.ShapeDtypeStruct((t, D_MODEL), BF16), jax.ShapeDtypeStruct(bbd.shape, F32),
                   jax.ShapeDtypeStruct(ccd.shape, F32), jax.ShapeDtypeStruct(a_re.shape, F32),
                   jax.ShapeDtypeStruct(a_im.shape, F32)],
        scratch_shapes=[pltpu.VMEM((N_SLAB, N_GBLK * tt, LANES), F32), pltpu.VMEM((N_SLAB, N_GBLK * tt, LANES), F32),
                        pltpu.VMEM((N_SLAB, N_GBLK, LANES), F32)],
        compiler_params=_cparams("arbitrary"),
    )(u, dy, du_add, hb, bbd, ccd, a_re, a_im)


def _s5_discretize(lam_re, lam_im, log_dt, b_re, b_im, c_re, c_im):
    lr = jnp.minimum(lam_re, -1e-4)
    li = lam_im
    dt = jnp.exp(log_dt)[:, None]
    mag = jnp.exp(lr * dt)
    a_re = mag * jnp.cos(li * dt)
    a_im = mag * jnp.sin(li * dt)
    den = lr * lr + li * li
    nr = a_re - 1.0
    f_re = (nr * lr + a_im * li) / den
    f_im = (a_im * lr - nr * li) / den
    bb_re = f_re[..., None] * b_re - f_im[..., None] * b_im
    bb_im = f_re[..., None] * b_im + f_im[..., None] * b_re
    eye = jnp.eye(GROUPS_PER_BLOCK, dtype=F32)
    bb = jnp.stack([bb_re, bb_im]).reshape(2, N_GBLK, GROUPS_PER_BLOCK, STATE, GROUP_SIZE)
    bbd = jnp.einsum('rgjph,jk->gjhrkp', bb, eye).reshape(N_GBLK, LANES, STATE_COLS)
    cc = jnp.stack([c_re, -c_im]).reshape(2, N_GBLK, GROUPS_PER_BLOCK, GROUP_SIZE, STATE)
    ccd = jnp.einsum('rgjhp,jk->grjpkh', cc, eye).reshape(N_GBLK, STATE_COLS, LANES)

    def dense(a):
        return a.reshape(N_GBLK, N_SLAB // 2, LANES).transpose(1, 0, 2)

    return dense(a_re), dense(a_im), bbd, ccd


def _attn_tiles(tq):
    row = lax.broadcasted_iota(jnp.int32, (tq, tq), 0)
    col = lax.broadcasted_iota(jnp.int32, (tq, tq), 1)
    return col < row, (row > col).astype(BF16), (row >= col).astype(BF16)


def _split_bf16(x):
    hi = x.astype(BF16)
    return hi, (x - hi.astype(F32)).astype(BF16)


def _attn_scores(qs, kj, cs, diag, later, masked):
    zs = [lax.dot_general(qa, kj, (((1,), (1,)), ((), ())), preferred_element_type=F32) * (HEAD_DIM ** -0.5) for qa in qs]
    lks = [-(jnp.maximum(z, 0.0) + jnp.log(1.0 + jnp.exp(-jnp.abs(z)))) for z in zs]
    if masked:
        lks = [jnp.where(diag, lk, 0.0) for lk in lks]
    parts = [_split_bf16(lk) for lk in lks]
    tails = [jnp.dot(hi, later, preferred_element_type=F32) + jnp.dot(lo, later, preferred_element_type=F32)
             for hi, lo in parts]
    ws = [jnp.exp(z + lk + tail + c) for z, lk, tail, c in zip(zs, lks, tails, cs)]
    if masked:
        ws = [jnp.where(diag, w, 0.0) for w in ws]
    return zs, lks, ws


def _head_masks(width=LANES):
    lane = lax.broadcasted_iota(jnp.int32, (1, width), 1)
    return [(lane >= a * HEAD_DIM) & (lane < (a + 1) * HEAD_DIM) for a in range(width // HEAD_DIM)]


def _any_alive(cs):
    alive = jnp.max(cs[0]) > EXP_UNDERFLOW
    for c in cs[1:]:
        alive = alive | (jnp.max(c) > EXP_UNDERFLOW)
    return alive


def _attn_fwd(q, k, v, *, tq=256):
    t = q.shape[0]
    tq = min(tq, t)
    width = ATTN_LANES

    def body(q_ref, k_ref, v_ref, o_ref, ol_ref):
        i = pl.program_id(1)
        diag, later, _ = _attn_tiles(tq)
        masks = _head_masks(width)
        q2 = q_ref[...]
        qs = [jnp.where(hm, q2, jnp.zeros_like(q2)) for hm in masks]

        def block(j, state, masked):
            rows = pl.ds(pl.multiple_of(j * tq, tq), tq)
            kj, vj = k_ref[rows, :], v_ref[rows, :]
            _, lks, ws = _attn_scores(qs, kj, [h[0] for h in state], diag, later, masked)
            parts = [_split_bf16(w) for w in ws]
            return tuple((c + jnp.sum(lk, axis=1, keepdims=True),
                          acc + jnp.dot(w_hi, vj, preferred_element_type=F32),
                          low + jnp.dot(w_lo, vj, preferred_element_type=F32))
                         for (c, acc, low), lk, (w_hi, w_lo) in zip(state, lks, parts))

        zero = jnp.zeros((tq, width), F32)
        state = block(i, tuple((jnp.zeros((tq, 1), F32), zero, zero) for _ in masks), True)

        def cond(st):
            return (st[0] >= 0) & _any_alive([h[0] for h in st[1]])

        def step(st):
            return st[0] - 1, block(st[0], st[1], False)

        _, state = lax.while_loop(cond, step, (i - 1, state))
        o_ref[...] = sum(jnp.where(hm, h[1], 0.0) for hm, h in zip(masks, state))
        ol_ref[...] = sum(jnp.where(hm, h[2], 0.0) for hm, h in zip(masks, state))

    qspec = pl.BlockSpec((tq, width), lambda h, i: (i, h))
    kspec = pl.BlockSpec((t, width), lambda h, i: (0, h))
    full = jax.ShapeDtypeStruct((t, D_MODEL), F32)
    return _pallas(
        body, name="attn_fwd", grid=(D_MODEL // width, t // tq), in_specs=[qspec, kspec, kspec], out_specs=[qspec, qspec],
        out_shape=[full, full], compiler_params=_cparams("parallel", "arbitrary"),
    )(q, k, v)


def _attn_bwd(q, k, v, o, o_low, do, *, tq=256):
    t = q.shape[0]
    tq = min(tq, t)
    width = ATTN_LANES

    def body(q_ref, k_ref, v_ref, o_ref, ol_ref, do_ref, dq_ref, dk_ref, dv_ref):
        i = pl.program_id(1)

        @pl.when(i == 0)
        def _():
            dk_ref[...] = jnp.zeros_like(dk_ref)
            dv_ref[...] = jnp.zeros_like(dv_ref)

        diag, later, later_eq = _attn_tiles(tq)
        masks = _head_masks(width)
        q2 = q_ref[...]
        do2 = do_ref[...]
        dd = do2.astype(BF16).astype(F32) * (o_ref[...] + ol_ref[...])
        qs = [jnp.where(hm, q2, jnp.zeros_like(q2)) for hm in masks]
        dos = [jnp.where(hm, do2, 0.0).astype(BF16) for hm in masks]
        totals = [jnp.sum(jnp.where(hm, dd, 0.0), axis=1, keepdims=True) for hm in masks]

        def block(j, state, masked):
            rows = pl.ds(pl.multiple_of(j * tq, tq), tq)
            kj, vj = k_ref[rows, :], v_ref[rows, :]
            nt = (((1,), (1,)), ((), ()))
            tn = (((0,), (0,)), ((), ()))
            zs, lks, ws = _attn_scores(qs, kj, [h[0] for h in state], diag, later, masked)
            dws = [lax.dot_general(doa, vj, nt, preferred_element_type=F32) for doa in dos]
            gs = [dw * w for dw, w in zip(dws, ws)]
            parts = [_split_bf16(g) for g in gs]
            from_here = [jnp.dot(hi, later_eq, preferred_element_type=F32) + jnp.dot(lo, later_eq, preferred_element_type=F32)
                         for hi, lo in parts]
            dzs = []
            for z, lk, g, fh, total, h in zip(zs, lks, gs, from_here, totals, state):
                beta = jnp.exp(z + lk)
                dz = g * (1.0 - beta) - beta * (total - h[1] - fh)
                if masked:
                    dz = jnp.where(diag, dz, 0.0)
                dzs.append((dz * (HEAD_DIM ** -0.5)).astype(BF16))
            dk = sum(lax.dot_general(dz, qa, tn, preferred_element_type=F32) for dz, qa in zip(dzs, qs))
            dv = sum(lax.dot_general(w.astype(BF16), doa, tn, preferred_element_type=F32) for w, doa in zip(ws, dos))
            new = [(c + jnp.sum(lk, axis=1, keepdims=True), r + jnp.sum(g, axis=1, keepdims=True),
                    acc + jnp.dot(dz, kj, preferred_element_type=F32))
                   for (c, r, acc), lk, g, dz in zip(state, lks, gs, dzs)]
            dk_ref[rows, :] += dk
            dv_ref[rows, :] += dv
            return tuple(new)

        zc = jnp.zeros((tq, 1), F32)
        state = block(i, tuple((zc, zc, jnp.zeros((tq, width), F32)) for _ in masks), True)

        def cond(st):
            return (st[0] >= 0) & _any_alive([h[0] for h in st[1]])

        def step(st):
            return st[0] - 1, block(st[0], st[1], False)

        _, state = lax.while_loop(cond, step, (i - 1, state))
        dq_ref[...] = sum(jnp.where(hm, h[2], 0.0) for hm, h in zip(masks, state))

    qspec = pl.BlockSpec((tq, width), lambda h, i: (i, h))
    kspec = pl.BlockSpec((t, width), lambda h, i: (0, h))
    kin = pl.BlockSpec((t, width), lambda h, i: (0, h), pipeline_mode=pl.Buffered(1))
    full = jax.ShapeDtypeStruct((t, D_MODEL), F32)
    return _pallas(
        body, name="attn_bwd", grid=(D_MODEL // width, t // tq), in_specs=[qspec, kin, kin, qspec, qspec, qspec],
        out_specs=[qspec, kspec, kspec], out_shape=[full, full, full],
        compiler_params=_cparams("parallel", "arbitrary"),
    )(q, k, v, o, o_low, do)


N_CHIP = 4
MESH_ID = pl.DeviceIdType.MESH
ANY_SPEC = pl.BlockSpec(memory_space=pl.ANY)


def _position():
    return lax.axis_index("x"), lax.axis_index("y"), lax.axis_index("c")


def _other_chips(x, y):
    return [(1 - x, y), (x, 1 - y), (1 - x, 1 - y)]


def _gather(arrs, *, name):
    n = len(arrs)

    def body(*refs):
        ins, outs = refs[:n], refs[n:2 * n]
        send_sems, recv_sems, local_sems = refs[2 * n:]
        x, y, c = _position()
        me, sibling = (x, y, c), (x, y, 1 - c)
        chips = _other_chips(x, y)

        def copy(a, k, block, to, src=None):
            slot = outs[a].at[4 * block[0] + 2 * block[1] + block[2]]
            return pltpu.make_async_remote_copy(
                src_ref=slot if src is None else src, dst_ref=slot, send_sem=send_sems.at[a, k],
                recv_sem=recv_sems.at[a, k], device_id=to, device_id_type=MESH_ID)

        local = [pltpu.make_async_copy(ins[a], outs[a].at[4 * x + 2 * y + c], local_sems.at[a]) for a in range(n)]
        first = []
        for a in range(n):
            local[a].start()
            first.append(copy(a, 0, me, sibling, src=ins[a]))
            first += [copy(a, 1 + j, me, (*chip, c), src=ins[a]) for j, chip in enumerate(chips)]
        for cp in first:
            cp.start()
        passed = []
        for a in range(n):
            for j, chip in enumerate(chips):
                copy(a, 1 + j, (*chip, c), me).wait_recv()
                passed.append(copy(a, 4 + j, (*chip, c), sibling))
                passed[-1].start()
        for a in range(n):
            copy(a, 0, sibling, me).wait_recv()
            for j, chip in enumerate(chips):
                copy(a, 4 + j, (*chip, 1 - c), me).wait_recv()
        for cp in first + passed:
            cp.wait_send()
        for cp in local:
            cp.wait()

    return _pallas(
        body, name=name, in_specs=[ANY_SPEC] * n, out_specs=[ANY_SPEC] * n,
        out_shape=[jax.ShapeDtypeStruct((N_DEV,) + a.shape, a.dtype) for a in arrs],
        scratch_shapes=[pltpu.SemaphoreType.DMA((n, N_DEV - 1)), pltpu.SemaphoreType.DMA((n, N_DEV - 1)),
                        pltpu.SemaphoreType.DMA((n,))],
    )(*arrs)


def _pair_exchange(arrs, *, name):
    n = len(arrs)

    def body(*refs):
        ins, got = refs[:n], refs[n:2 * n]
        send_sems, recv_sems = refs[2 * n:]
        x, y, c = _position()
        sends = []
        for a in range(n):
            for ch in range(N_CHIP):
                sends.append(pltpu.make_async_remote_copy(
                    src_ref=ins[a].at[2 * ch + (1 - c)], dst_ref=got[a].at[ch], send_sem=send_sems.at[a, ch],
                    recv_sem=recv_sems.at[a, ch], device_id=(x, y, 1 - c), device_id_type=MESH_ID))
        for cp in sends:
            cp.start()
        for cp in sends:
            cp.wait_send()
            cp.wait_recv()

    return _pallas(
        body, name=name, in_specs=[ANY_SPEC] * n, out_specs=[ANY_SPEC] * n,
        out_shape=[jax.ShapeDtypeStruct((N_CHIP,) + a.shape[1:], a.dtype) for a in arrs],
        scratch_shapes=[pltpu.SemaphoreType.DMA((n, N_CHIP)), pltpu.SemaphoreType.DMA((n, N_CHIP))],
    )(*arrs)


def _chip_exchange(arrs, *, name):
    n = len(arrs)

    def body(*refs):
        ins, outs = refs[:n], refs[n:2 * n]
        send_sems, recv_sems, local_sems = refs[2 * n:]
        x, y, c = _position()
        mine = 2 * x + y
        chips = _other_chips(x, y)
        sends, recvs, locals_ = [], [], []
        for a in range(n):
            locals_.append(pltpu.make_async_copy(ins[a].at[mine], outs[a].at[mine], local_sems.at[a]))
            for j, (px, py) in enumerate(chips):
                sends.append(pltpu.make_async_remote_copy(
                    src_ref=ins[a].at[2 * px + py], dst_ref=outs[a].at[mine], send_sem=send_sems.at[a, j],
                    recv_sem=recv_sems.at[a, j], device_id=(px, py, c), device_id_type=MESH_ID))
                recvs.append(pltpu.make_async_remote_copy(
                    src_ref=ins[a].at[2 * px + py], dst_ref=outs[a].at[2 * px + py], send_sem=send_sems.at[a, j],
                    recv_sem=recv_sems.at[a, j], device_id=(px, py, c), device_id_type=MESH_ID))
        for cp in locals_ + sends:
            cp.start()
        for cp in sends:
            cp.wait_send()
        for cp in recvs:
            cp.wait_recv()
        for cp in locals_:
            cp.wait()

    return _pallas(
        body, name=name, in_specs=[ANY_SPEC] * n, out_specs=[ANY_SPEC] * n,
        out_shape=[jax.ShapeDtypeStruct(a.shape, a.dtype) for a in arrs],
        scratch_shapes=[pltpu.SemaphoreType.DMA((n, N_CHIP - 1)), pltpu.SemaphoreType.DMA((n, N_CHIP - 1)),
                        pltpu.SemaphoreType.DMA((n,))],
    )(*arrs)


def _pair_sum(mine, got, core, *, name):
    s_, r, c = got.shape
    tr = r if r <= 256 else 256

    def body(core_ref, a_ref, b_ref, o_ref):
        o_ref[...] = a_ref[...] + b_ref[...]

    quarter = pl.BlockSpec((s_, tr, c), lambda i, core_ref: (0, i, 0))
    return _pallas(
        body, name=name, out_shape=jax.ShapeDtypeStruct(got.shape, got.dtype),
        grid_spec=pltpu.PrefetchScalarGridSpec(
            num_scalar_prefetch=1, grid=(r // tr,),
            in_specs=[pl.BlockSpec((s_, None, tr, c), lambda i, core_ref: (0, core_ref[0], i, 0)), quarter],
            out_specs=quarter),
        compiler_params=_cparams("parallel"),
    )(core, mine.reshape(s_, 2, r, c), got)


def _reduce_scatter(arrs, names):
    got = _pair_exchange(arrs, name="pair_exchange")
    core = lax.axis_index("c").astype(jnp.int32).reshape(1)
    sums = [_pair_sum(a, g_, core, name="pair_sum_" + n) for a, g_, n in zip(arrs, got, names)]
    return _chip_exchange(sums, name="chip_exchange")


def _adamw(parts, w, m, v, *, name):
    s, r, c = parts.shape
    tr = r if r * c <= 256 * D_MODEL else 256

    def body(p_ref, w_ref, m_ref, v_ref, g_ref, d_ref, nm_ref, nv_ref):
        g = p_ref[0]
        for j in range(1, s):
            g = g + p_ref[j]
        nm = ADAM_B1 * m_ref[...] + (1.0 - ADAM_B1) * g
        nv = ADAM_B2 * v_ref[...] + (1.0 - ADAM_B2) * (g * g)
        m_hat = nm / (1.0 - ADAM_B1 ** ADAM_STEP)
        v_hat = nv / (1.0 - ADAM_B2 ** ADAM_STEP)
        g_ref[...] = g
        d_ref[...] = -ADAM_LR * (m_hat / (jnp.sqrt(v_hat) + ADAM_EPS) + ADAM_WD * w_ref[...])
        nm_ref[...] = nm
        nv_ref[...] = nv

    spec = pl.BlockSpec((tr, c), lambda i: (i, 0))
    one = jax.ShapeDtypeStruct((r, c), F32)
    return _pallas(
        body, name=name, grid=(r // tr,), in_specs=[pl.BlockSpec((s, tr, c), lambda i: (0, i, 0)), spec, spec, spec],
        out_specs=[spec] * 4, out_shape=[one] * 4, compiler_params=_cparams("parallel"),
    )(parts, w, m, v)


def _cols_to_blocks(g, n_blk):
    r = g.shape[0]
    return g.reshape(r, n_blk, g.shape[1] // n_blk).transpose(1, 0, 2)


def _blocks_to_cols(gathered):
    n_blk, r, c = gathered.shape
    return gathered.transpose(1, 0, 2).reshape(r, n_blk * c)


def kernel(x, p, a_norm_pre, a_norm_post, a_w_in, a_lam_re, a_lam_im, a_log_dt, a_b_re, a_b_im, a_c_re, a_c_im, a_d_skip, a_w_glu, a_b_glu, a_w_out, kv_norm, w_kv, b_norm_pre, b_norm_post, b_w_in, b_w_out, ple_w_proj, ple_w_gate, loss_target, m_a_norm_pre, m_a_norm_post, m_a_w_in, m_a_lam_re, m_a_lam_im, m_a_log_dt, m_a_b_re, m_a_b_im, m_a_c_re, m_a_c_im, m_a_d_skip, m_a_w_glu, m_a_b_glu, m_a_w_out, m_kv_norm, m_w_kv, m_b_norm_pre, m_b_norm_post, m_b_w_in, m_b_w_out, m_ple_w_proj, m_ple_w_gate, v_a_norm_pre, v_a_norm_post, v_a_w_in, v_a_lam_re, v_a_lam_im, v_a_log_dt, v_a_b_re, v_a_b_im, v_a_c_re, v_a_c_im, v_a_d_skip, v_a_w_glu, v_a_b_glu, v_a_w_out, v_kv_norm, v_w_kv, v_b_norm_pre, v_b_norm_post, v_b_w_in, v_b_w_out, v_ple_w_proj, v_ple_w_gate):
    weights = dict(a_norm_pre=a_norm_pre, a_norm_post=a_norm_post, a_w_in=a_w_in, a_lam_re=a_lam_re, a_lam_im=a_lam_im, a_log_dt=a_log_dt, a_b_re=a_b_re, a_b_im=a_b_im, a_c_re=a_c_re, a_c_im=a_c_im, a_d_skip=a_d_skip, a_w_glu=a_w_glu, a_b_glu=a_b_glu, a_w_out=a_w_out, kv_norm=kv_norm, w_kv=w_kv, b_norm_pre=b_norm_pre, b_norm_post=b_norm_post, b_w_in=b_w_in, b_w_out=b_w_out, ple_w_proj=ple_w_proj, ple_w_gate=ple_w_gate)
    mom_m = dict(a_norm_pre=m_a_norm_pre, a_norm_post=m_a_norm_post, a_w_in=m_a_w_in, a_lam_re=m_a_lam_re, a_lam_im=m_a_lam_im, a_log_dt=m_a_log_dt, a_b_re=m_a_b_re, a_b_im=m_a_b_im, a_c_re=m_a_c_re, a_c_im=m_a_c_im, a_d_skip=m_a_d_skip, a_w_glu=m_a_w_glu, a_b_glu=m_a_b_glu, a_w_out=m_a_w_out, kv_norm=m_kv_norm, w_kv=m_w_kv, b_norm_pre=m_b_norm_pre, b_norm_post=m_b_norm_post, b_w_in=m_b_w_in, b_w_out=m_b_w_out, ple_w_proj=m_ple_w_proj, ple_w_gate=m_ple_w_gate)
    mom_v = dict(a_norm_pre=v_a_norm_pre, a_norm_post=v_a_norm_post, a_w_in=v_a_w_in, a_lam_re=v_a_lam_re, a_lam_im=v_a_lam_im, a_log_dt=v_a_log_dt, a_b_re=v_a_b_re, a_b_im=v_a_b_im, a_c_re=v_a_c_re, a_c_im=v_a_c_im, a_d_skip=v_a_d_skip, a_w_glu=v_a_w_glu, a_b_glu=v_a_b_glu, a_w_out=v_a_w_out, kv_norm=v_kv_norm, w_kv=v_w_kv, b_norm_pre=v_b_norm_pre, b_norm_post=v_b_norm_post, b_w_in=v_b_w_in, b_w_out=v_b_w_out, ple_w_proj=v_ple_w_proj, ple_w_gate=v_ple_w_gate)
    names = list(weights)

    t = x.shape[1]
    x0 = x[0]
    p0, p1 = p[0, 0], p[1, 0]
    tgt = loss_target[0]

    mats = dict(a_w_in=a_w_in[0], a_w_glu=a_w_glu[0], a_w_out=a_w_out[0], w_kv=w_kv, b_w_in=b_w_in[0],
                b_w_out=b_w_out[0], ple_w_proj=ple_w_proj.reshape(2 * 256, LANES),
                ple_w_gate=ple_w_gate.reshape(2 * LANES, D_MODEL))
    vec_names = ["a_norm_pre", "a_norm_post", "a_d_skip", "a_b_glu"]
    vec_pack = jnp.concatenate([weights[n] for n in vec_names], axis=0)
    gathered = _gather([m_.astype(BF16) for m_ in mats.values()] + [vec_pack], name="gather_weights")
    gw = dict(zip(mats, gathered[:-1]))
    vec_full = gathered[-1].transpose(1, 0, 2).reshape(len(vec_names), D_MODEL)
    g_apre, g_apost, d_skip, b_glu = (vec_full[i:i + 1] for i in range(4))
    w_ain, w_kvb, w_bin = gw["a_w_in"], gw["w_kv"], gw["b_w_in"]
    w_glu = gw["a_w_glu"].reshape(D_MODEL, D_MODEL)
    w_out = gw["a_w_out"].reshape(D_MODEL, D_MODEL)
    w_bout = gw["b_w_out"].reshape(D_MODEL, D_MODEL)
    wp = gw["ple_w_proj"].reshape(N_DEV, 2, 256, LANES)
    w_p0, w_p1 = _blocks_to_cols(wp[:, 0]), _blocks_to_cols(wp[:, 1])
    wg = gw["ple_w_gate"].reshape(N_DEV, 2, LANES, D_MODEL)
    w_g0, w_g1 = wg[:, 0].reshape(D_MODEL, D_MODEL), wg[:, 1].reshape(D_MODEL, D_MODEL)
    g_kv, g_bpre, g_bpost = kv_norm.reshape(1, D_MODEL), b_norm_pre, b_norm_post

    s5_params = (a_lam_re[0], a_lam_im[0], a_log_dt[0], a_b_re[0], a_b_im[0], a_c_re[0], a_c_im[0])
    (a_re, a_im, bbd, ccd), s5_vjp = jax.vjp(_s5_discretize, *s5_params)
    bbd_bf, ccd_bf = bbd.astype(BF16), ccd.astype(BF16)

    f_a2 = lambda ys, u, d: (_gelu(ys + d * u),)
    f_a3 = lambda g, s, ga, b: (g * _sigmoid(s + b) * _silu(ga),)
    f_rms = lambda y, g: (_rms(y, g),)
    f_k = lambda x2, gk, gp: (_rms(x2, gk), _rms(x2, gp))
    f_b3 = lambda o, gb: (o * _silu(gb),)

    def f_a_in(x0_, g_, w):
        h = _rms(x0_, g_).astype(BF16)
        return h, _dot(h, w[:4]), _dot(h, w[4:])

    h0, u, ga = _ew(f_a_in, [x0], [g_apre], [BF16, F32, F32], mats=[w_ain], name="a_norm_proj")
    ys, hb = _s5_fwd(u, bbd_bf, ccd_bf, a_re, a_im)

    def f_a_glu(ys_, u_, d_, w):
        g = f_a2(ys_, u_, d_)[0].astype(BF16)
        return g, _dot(g, w)

    g_act, s_glu = _ew(f_a_glu, [ys, u], [d_skip], [BF16, F32], mats=[w_glu], name="a_gelu_glu")
    def f_a_out(ys_, u_, s_, ga_, x0_, d_, b_, g_, w):
        (y3_,) = f_a3(_gelu(ys_ + d_ * u_), s_, ga_, b_)
        y3_ = y3_.astype(BF16)
        y4_ = _dot(y3_, w)
        return y3_, y4_, x0_ + _rms(y4_, g_)

    y3, y4, x1 = _ew(f_a_out, [ys, u, s_glu, ga, x0], [d_skip, b_glu, g_apost], [BF16, F32, F32], mats=[w_out],
                     name="a_gate_out_norm")

    def f_ple(xin, p_, wg_, wp_):
        pg, pp = _dot(xin, wg_), _dot(p_, wp_)
        return xin + _sigmoid(pg) * pp, pg, pp

    x2, pg0, pp0 = _ew(f_ple, [x1, p0], [], [F32, F32, F32], mats=[w_g0, w_p0], name="ple0")
    def f_b_in(x2_, gk, gp, wkv_, win_):
        kvn_, h1_ = (t_.astype(BF16) for t_ in f_k(x2_, gk, gp))
        return kvn_, h1_, _dot(kvn_, wkv_[:4]), _dot(kvn_, wkv_[4:]), _dot(h1_, win_[:4]), _dot(h1_, win_[4:])

    kvn, h1, k_, v_, q_, gb = _ew(f_b_in, [x2], [g_kv, g_bpre], [BF16, BF16, BF16, BF16, BF16, F32],
                                  mats=[w_kvb, w_bin], name="b_norms_proj")
    o, o_low = _attn_fwd(q_, k_, v_)

    def f_b_out(o_, gb_, x2_, g_, w):
        y5in_ = f_b3(o_, gb_)[0].astype(BF16)
        y5_ = _dot(y5in_, w)
        return y5in_, y5_, x2_ + _rms(y5_, g_)

    y5in, y5, x3 = _ew(f_b_out, [o, gb, x2], [g_bpost], [BF16, F32, F32], mats=[w_bout], name="b_gate_out_norm")

    def f_ple_loss(xin, p_, tg, wg_, wp_):
        pg, pp = _dot(xin, wg_), _dot(p_, wp_)
        err = xin + _sigmoid(pg) * pp - tg
        return err * (1.0 / D_MODEL), pg, pp, (0.5 / D_MODEL) * jnp.sum(err * err).reshape(1, 1)

    dx4, pg1, pp1, loss_part = _ew(f_ple_loss, [x3, p1, tgt], [], [F32, F32, F32], mats=[w_g1, w_p1], n_red=1,
                                   name="ple1_loss")
    loss = lax.psum(loss_part[0, 0], ("x", "y", "c"))

    grads = {}
    def f_ple_bwd(dx, pg, pp, wg_):
        sg = _sigmoid(pg)
        dpg = (dx * pp * sg * (1.0 - sg)).astype(BF16)
        return dx + _dot_t(dpg, wg_), dpg, dx * sg

    def f_norm_bwd_mm(y, dx, g_, w):
        _, vjp = jax.vjp(_rms, y, g_)
        dy, dg = vjp(dx)
        dy = dy.astype(BF16)
        return _dot_t(dy, w), dy, dg

    dx3, dpg1, dpp1 = _ew(f_ple_bwd, [dx4, pg1, pp1], [], [F32, BF16, BF16], mats=[w_g1], name="ple1_bwd")
    d_wg1 = _mm(x3, dpg1, ta=True, name="ple1_gate_dw")
    d_wp1 = _mm(p1, dpp1, ta=True, name="ple1_proj_dw")
    def f_b_out_bwd(y, dx, o_, gb_, g_, w):
        dyin, dy, dg = f_norm_bwd_mm(y, dx, g_, w)
        _, vjp = jax.vjp(lambda a, b: f_b3(a, b)[0], o_, gb_)
        do_, dgb_ = vjp(dyin)
        return dy, do_, dgb_, dg

    dy5, do, dgb, d_gbpost = _ew(f_b_out_bwd, [y5, dx3, o, gb], [g_bpost], [BF16, F32, BF16], mats=[w_bout], n_red=1,
                                 name="b_out_bwd")
    grads["b_w_out"] = _mm(y5in, dy5, ta=True, name="b_out_dw").reshape(N_DEV, LANES, D_MODEL)
    dq, dk, dv = _attn_bwd(q_, k_, v_, o, o_low, do)
    grads["b_w_in"] = _mm_dw_cols(h1, [dq, dgb], 4, name="b_proj_dw")
    grads["w_kv"] = _mm_dw_cols(kvn, [dk, dv], 4, name="kv_dw")
    def f_b_in_bwd(x2_, dx, dk_, dv_, dq_, dgb_, gk, gp, wkv_, win_):
        dkvn = _dot_t(dk_, wkv_[:4]) + _dot_t(dv_, wkv_[4:])
        dh1 = _dot_t(dq_, win_[:4]) + _dot_t(dgb_, win_[4:])
        _, vjp = jax.vjp(f_k, x2_, gk, gp)
        dx2_, dgk, dgp = vjp((dkvn, dh1))
        return dx + dx2_, dgk, dgp

    dx2, d_gkv, d_gbpre = _ew(f_b_in_bwd, [x2, dx3, dk, dv, dq, dgb], [g_kv, g_bpre], [F32],
                              mats=[w_kvb, w_bin], n_red=2, name="b_norms_proj_bwd")
    dx1, dpg0, dpp0 = _ew(f_ple_bwd, [dx2, pg0, pp0], [], [F32, BF16, BF16], mats=[w_g0], name="ple0_bwd")
    d_wg0 = _mm(x1, dpg0, ta=True, name="ple0_gate_dw")
    d_wp0 = _mm(p0, dpp0, ta=True, name="ple0_proj_dw")
    grads["ple_w_gate"] = jnp.stack([d_wg0.reshape(N_DEV, LANES, D_MODEL), d_wg1.reshape(N_DEV, LANES, D_MODEL)],
                                    axis=1).reshape(N_DEV, 2 * LANES, D_MODEL)
    grads["ple_w_proj"] = jnp.stack([_cols_to_blocks(d_wp0, N_DEV), _cols_to_blocks(d_wp1, N_DEV)],
                                    axis=1).reshape(N_DEV, 2 * 256, LANES)
    dy3, dy4, d_gapost = _ew(f_norm_bwd_mm, [y4, dx1], [g_apost], [F32, BF16], mats=[w_out], n_red=1,
                             name="a_norm_post_bwd")
    grads["a_w_out"] = _mm(y3, dy4, ta=True, name="a_out_dw").reshape(N_DEV, LANES, D_MODEL)
    def f_a_gate_bwd(ys_, u_, s_, ga_, ct, d_, b_, w):
        g, vjp_gelu = jax.vjp(lambda a, b, c: f_a2(a, b, c)[0], ys_, u_, d_)
        _, vjp_gate = jax.vjp(lambda a, b, c, e: f_a3(a, b, c, e)[0], g, s_, ga_, b_)
        dg, ds_, dga_, db = vjp_gate(ct)
        ds_ = ds_.astype(BF16)
        dys_, du_, dd = vjp_gelu(dg + _dot_t(ds_, w))
        return ds_, dga_, dys_, du_, db, dd

    ds, dga, dys, du_elem, d_bglu, d_dskip = _ew(f_a_gate_bwd, [ys, u, s_glu, ga, dy3], [d_skip, b_glu],
                                                 [BF16, BF16, F32, F32], mats=[w_glu], n_red=2, name="a_gate_bwd")
    grads["a_w_glu"] = _mm(g_act, ds, ta=True, name="a_glu_dw").reshape(N_DEV, LANES, D_MODEL)
    du, d_bbd, d_ccd, d_are, d_aim = _s5_bwd(u, dys, du_elem, hb, bbd_bf, ccd_bf, a_re, a_im)
    grads["a_w_in"] = _mm_dw_cols(h0, [du, dga], 4, name="a_proj_dw")
    def f_a_in_bwd(x0_, dx, du_, dga_, g_, w):
        dh0 = _dot_t(du_, w[:4]) + _dot_t(dga_, w[4:])
        _, vjp = jax.vjp(_rms, x0_, g_)
        dx0, dg = vjp(dh0)
        return dx + dx0, dg

    grad_x, d_gapre = _ew(f_a_in_bwd, [x0, dx1, du, dga], [g_apre], [F32], mats=[w_ain], n_red=1,
                          name="a_norm_proj_bwd")

    d_s5 = s5_vjp((d_are, d_aim, d_bbd, d_ccd))
    small = dict(zip(["a_lam_re", "a_lam_im", "a_log_dt", "a_b_re", "a_b_im", "a_c_re", "a_c_im"], d_s5))
    small.update(kv_norm=d_gkv, b_norm_pre=d_gbpre, b_norm_post=d_gbpost)
    small_names = list(small)
    small_flat = jnp.concatenate([small[n].reshape(-1) for n in small_names])
    small_pack = jnp.pad(small_flat, (0, -small_flat.size % (8 * LANES))).reshape(-1, LANES)
    vec_grads = jnp.concatenate([d_gapre, d_gapost, d_dskip, d_bglu], axis=0)
    vec_scatter = vec_grads.reshape(len(vec_names), N_DEV, LANES).transpose(1, 0, 2)
    mat_names = list(mats)
    scattered = _reduce_scatter([grads[n] for n in mat_names] + [vec_scatter], mat_names + ["vectors"])
    (small_all,) = _gather([small_pack], name="gather_small_grads")

    out_g, out_d, out_m, out_v = {}, {}, {}, {}

    def update(n, parts):
        shape = weights[n].shape
        rc = parts.shape[1:]
        res = _adamw(parts, weights[n].reshape(rc), mom_m[n].reshape(rc), mom_v[n].reshape(rc), name="adamw_" + n)
        out_g[n], out_d[n], out_m[n], out_v[n] = (r.reshape(shape) for r in res)

    for n, parts in zip(mat_names, scattered[:-1]):
        update(n, parts)
    for i, n in enumerate(vec_names):
        update(n, scattered[-1][:, i:i + 1, :])
    off = 0
    for n in small_names:
        size = small[n].size
        rc = (size // LANES, LANES) if size % LANES == 0 else (1, size)
        update(n, small_all.reshape(N_DEV, -1)[:, off:off + size].reshape((N_DEV,) + rc))
        off += size

    return (loss, grad_x[None], *[out_g[n] for n in names], *[out_d[n] for n in names],
            *[out_m[n] for n in names], *[out_v[n] for n in names])
```

```python
import functools
import math

import jax
import jax.numpy as jnp
from jax import lax
from jax.experimental import pallas as pl
from jax.experimental.pallas import tpu as pltpu

F32 = jnp.float32
BF16 = jnp.bfloat16

N_DEV = 8
D_MODEL = 1024
GROUP_SIZE = 16
N_GROUPS = D_MODEL // GROUP_SIZE
STATE = 64
HEAD_DIM = 64
EPS = 1e-6
LANES = 128
GROUPS_PER_BLOCK = LANES // GROUP_SIZE
N_GBLK = N_GROUPS // GROUPS_PER_BLOCK
STATE_COLS = 2 * GROUPS_PER_BLOCK * STATE
N_SLAB = STATE_COLS // LANES
VMEM_LIMIT = 56 * 1024 * 1024
EXP_UNDERFLOW = -104.0
ATTN_LANES = 256
MM_TILE = 1024
EW_ROWS_BYTES = 24 * 1024 * 1024

ADAM_LR = 0.001
ADAM_B1 = 0.9
ADAM_B2 = 0.999
ADAM_EPS = 1e-08
ADAM_WD = 0.01
ADAM_STEP = 10


def _pallas(body, **kw):
    return pl.pallas_call(body, **kw)


def _cparams(*sem):
    return pltpu.CompilerParams(dimension_semantics=sem, vmem_limit_bytes=VMEM_LIMIT)


def _mm(a, b, *, ta=False, tb=False, add=None, out_dtype=F32, name):
    m, k = (a.shape[1], a.shape[0]) if ta else a.shape
    n = b.shape[0] if tb else b.shape[1]
    tm, tn, tk = min(m, MM_TILE), min(n, MM_TILE), min(k, MM_TILE)
    nk = k // tk
    grid = (m // tm, n // tn, nk)
    a_spec = pl.BlockSpec((tk, tm), lambda i, j, l: (l, i)) if ta else pl.BlockSpec((tm, tk), lambda i, j, l: (i, l))
    b_spec = pl.BlockSpec((tn, tk), lambda i, j, l: (j, l)) if tb else pl.BlockSpec((tk, tn), lambda i, j, l: (l, j))
    o_spec = pl.BlockSpec((tm, tn), lambda i, j, l: (i, j))
    dims = (((0 if ta else 1,), (1 if tb else 0,)), ((), ()))
    has_add = add is not None
    assert nk == 1 or (out_dtype == F32 and not has_add)

    def body(*refs):
        a_ref, b_ref, o_ref = refs[0], refs[1], refs[-1]
        part = lax.dot_general(a_ref[...].astype(BF16), b_ref[...].astype(BF16), dims, preferred_element_type=F32)
        if nk == 1:
            if has_add:
                part = part + refs[2][...].astype(F32)
            o_ref[...] = part.astype(out_dtype)
        else:
            l = pl.program_id(2)

            @pl.when(l == 0)
            def _():
                o_ref[...] = part

            @pl.when(l > 0)
            def _():
                o_ref[...] += part

    ins = [a, b] + ([add] if has_add else [])
    specs = [a_spec, b_spec] + ([o_spec] if has_add else [])
    return _pallas(
        body, name=name, grid=grid, in_specs=specs, out_specs=o_spec,
        out_shape=jax.ShapeDtypeStruct((m, n), out_dtype),
        compiler_params=_cparams("parallel", "parallel", "arbitrary"),
    )(*ins)


def _mm_dw_cols(a, bs, n_blk, *, name):
    t, m = a.shape
    n = bs[0].shape[1]
    c = n // n_blk
    tk = min(t, MM_TILE)
    nb = len(bs)

    def body(*refs):
        a_ref, b_refs, o_ref = refs[0], refs[1:1 + nb], refs[-1]
        l = pl.program_id(0)

        @pl.when(l == 0)
        def _():
            o_ref[...] = jnp.zeros_like(o_ref)

        a_bf = a_ref[...].astype(BF16)
        for bi, b_ref in enumerate(b_refs):
            part = lax.dot_general(a_bf, b_ref[...].astype(BF16), (((0,), (0,)), ((), ())), preferred_element_type=F32)
            for j in range(n_blk):
                o_ref[bi * n_blk + j] += part[:, j * c:(j + 1) * c]

    return _pallas(
        body, name=name, grid=(t // tk,),
        in_specs=[pl.BlockSpec((tk, m), lambda l: (l, 0))] + [pl.BlockSpec((tk, n), lambda l: (l, 0))] * nb,
        out_specs=pl.BlockSpec((nb * n_blk, m, c), lambda l: (0, 0, 0)),
        out_shape=jax.ShapeDtypeStruct((nb * n_blk, m, c), F32),
        compiler_params=_cparams("arbitrary"),
    )(a, *bs)


def _row_spec(arr, tm):
    return pl.BlockSpec((tm, arr.shape[1]), lambda i: (i, 0))


def _full_spec(arr):
    return pl.BlockSpec(arr.shape, lambda i: (0,) * arr.ndim)


def _ew_tile_rows(t, row_bytes):
    tm = 1024
    while tm > 8 and (t % tm or 2 * tm * row_bytes > EW_ROWS_BYTES):
        tm //= 2
    return tm


def _ew(fn, rows, vecs, out_dtypes, *, mats=(), n_red=0, name):
    t = rows[0].shape[0]
    nr, nv, nm = len(rows), len(vecs), len(mats)

    def tile_shapes(tm):
        return jax.eval_shape(fn, *[jax.ShapeDtypeStruct((tm, r.shape[1]), F32) for r in rows],
                              *[jax.ShapeDtypeStruct(v.shape, F32) for v in vecs],
                              *[jax.ShapeDtypeStruct(m_.shape, m_.dtype) for m_ in mats])

    probe = tile_shapes(8)
    row_bytes = sum(r.shape[1] * r.dtype.itemsize for r in rows)
    row_bytes += sum(s.shape[1] * jnp.dtype(dt).itemsize for s, dt in zip(probe, out_dtypes))
    tm = _ew_tile_rows(t, row_bytes)
    shapes = tile_shapes(tm)
    n_out = len(shapes) - n_red

    def body(*refs):
        ins = [r[...].astype(F32) for r in refs[:nr + nv]] + [r[...] for r in refs[nr + nv:nr + nv + nm]]
        outs = fn(*ins)
        o_refs = refs[nr + nv + nm:]
        for o_ref, o in zip(o_refs[:n_out], outs[:n_out]):
            o_ref[...] = o.astype(o_ref.dtype)
        i = pl.program_id(0)
        for o_ref, o in zip(o_refs[n_out:], outs[n_out:]):
            @pl.when(i == 0)
            def _(o_ref=o_ref, o=o):
                o_ref[...] = o

            @pl.when(i > 0)
            def _(o_ref=o_ref, o=o):
                o_ref[...] += o

    out_shape = [jax.ShapeDtypeStruct((t, s.shape[1]), dt) for s, dt in zip(shapes[:n_out], out_dtypes)]
    out_shape += [jax.ShapeDtypeStruct(s.shape, F32) for s in shapes[n_out:]]
    out_specs = [pl.BlockSpec((tm, s.shape[1]), lambda i: (i, 0)) for s in shapes[:n_out]]
    out_specs += [pl.BlockSpec(s.shape, lambda i: (0, 0)) for s in shapes[n_out:]]
    return _pallas(
        body, name=name, grid=(t // tm,),
        in_specs=[_row_spec(r, tm) for r in rows] + [_full_spec(v) for v in list(vecs) + list(mats)],
        out_specs=out_specs, out_shape=out_shape,
        compiler_params=_cparams("arbitrary" if n_red else "parallel"),
    )(*rows, *vecs, *mats)


def _dot(a, w):
    a = a.astype(BF16)
    if w.ndim == 3:
        return jnp.concatenate([jnp.dot(a, w[j], preferred_element_type=F32) for j in range(w.shape[0])], axis=1)
    return jnp.dot(a, w, preferred_element_type=F32)


def _dot_t(a, w):
    a = a.astype(BF16)
    nt = (((1,), (1,)), ((), ()))
    if w.ndim == 3:
        c = w.shape[2]
        return sum(lax.dot_general(a[:, j * c:(j + 1) * c], w[j], nt, preferred_element_type=F32) for j in range(w.shape[0]))
    return lax.dot_general(a, w, nt, preferred_element_type=F32)


def _ew_bwd(fn, rows, vecs, cts, row_wrt, vec_wrt, out_dtypes, *, name):
    nr, nv, nc = len(rows), len(vecs), len(cts)

    def grad_fn(*tiles):
        ins = list(tiles[:nr + nv])
        ct = tiles[nr + nv:]
        wrt = list(row_wrt) + [nr + j for j in vec_wrt]

        def f(*sel):
            full = list(ins)
            for idx, s in zip(wrt, sel):
                full[idx] = s
            return tuple(fn(*full))

        _, vjp = jax.vjp(f, *[ins[idx] for idx in wrt])
        return vjp(tuple(ct))

    return _ew_reordered(grad_fn, rows, vecs, cts, out_dtypes, len(vec_wrt), name)


def _ew_reordered(grad_fn, rows, vecs, cts, out_dtypes, n_red, name):
    nr, nv = len(rows), len(vecs)

    def fn(*tiles):
        r, c, v = tiles[:nr], tiles[nr:nr + len(cts)], tiles[nr + len(cts):]
        return grad_fn(*r, *v, *c)

    return _ew(fn, list(rows) + list(cts), vecs, out_dtypes, n_red=n_red, name=name)


def _rms(x, g):
    return x * lax.rsqrt(jnp.mean(x * x, axis=-1, keepdims=True) + EPS) * g


def _sigmoid(x):
    return 1.0 / (1.0 + jnp.exp(-x))


def _silu(x):
    return x * _sigmoid(x)


def _gelu(x):
    return 0.5 * x * (1.0 + jnp.tanh(math.sqrt(2.0 / math.pi) * (x + 0.044715 * x * x * x)))


def _token_rows(t):
    return pl.ds(pl.multiple_of(t * N_GBLK, N_GBLK), N_GBLK)


def _block_rows(gb, tt):
    return pl.ds(gb, tt, stride=N_GBLK)


def _s5_expand(u_bf, bbd_ref, xs_ref, tt):
    for gb in range(N_GBLK):
        xg = jnp.dot(u_bf[:, gb * LANES:(gb + 1) * LANES], bbd_ref[gb], preferred_element_type=F32)
        for l in range(N_SLAB):
            xs_ref[l, _block_rows(gb, tt), :] = xg[:, l * LANES:(l + 1) * LANES]


def _s5_forward_scan(xs_ref, ar, ai, state, tt):
    half = N_SLAB // 2

    def step(t, st):
        new = [None] * N_SLAB
        for l in range(half):
            sr, si = st[l], st[half + l]
            rows = _token_rows(t)
            nr_ = ar[l] * sr - ai[l] * si + xs_ref[l, rows, :]
            ni_ = ar[l] * si + ai[l] * sr + xs_ref[half + l, rows, :]
            xs_ref[l, rows, :] = nr_
            xs_ref[half + l, rows, :] = ni_
            new[l], new[half + l] = nr_, ni_
        return tuple(new)

    return lax.fori_loop(0, tt, step, tuple(state), unroll=4)


def _slab_rows(ref, gb, tt):
    return jnp.concatenate([ref[l, _block_rows(gb, tt), :] for l in range(N_SLAB)], axis=1)


def _hosted(plan, i, nt, refs, n_in, n_out, n_scratch):
    k_in = len(plan["ins"]) if plan else 0
    k_out = len(plan["out_shape"]) if plan else 0
    own_in, ex_in = refs[:n_in], refs[n_in:n_in + k_in]
    o0 = n_in + k_in
    own_out, ex_out = refs[o0:o0 + n_out], refs[o0 + n_out:o0 + n_out + k_out]
    s0 = o0 + n_out + k_out
    own_scr, ex_sems = refs[s0:s0 + n_scratch], refs[s0 + n_scratch:]

    def run(phase, step):
        if plan:
            @pl.when(i == step)
            def _():
                plan["phases"][phase](ex_in, ex_out, ex_sems)

    return own_in, own_out, own_scr, run


def _s5_fwd(u, bbd, ccd, a_re, a_im, *, tt=256, plan=None):
    t = u.shape[0]
    nt = t // tt
    half = N_SLAB // 2

    def body(*refs):
        i = pl.program_id(0)
        (u_ref, bbd_ref, ccd_ref, ar_ref, ai_ref), (y_ref, hb_ref), (xs_ref, st_ref), run = _hosted(plan, i, nt, refs, 5, 2, 2)
        run(0, 0)
        run(1, nt // 2)

        @pl.when(i == 0)
        def _():
            st_ref[...] = jnp.zeros_like(st_ref)

        hb_ref[0] = st_ref[...]
        _s5_expand(u_ref[...].astype(BF16), bbd_ref, xs_ref, tt)
        ar = [ar_ref[l] for l in range(half)]
        ai = [ai_ref[l] for l in range(half)]
        last = _s5_forward_scan(xs_ref, ar, ai, [st_ref[l] for l in range(N_SLAB)], tt)
        for l in range(N_SLAB):
            st_ref[l] = last[l]
        for gb in range(N_GBLK):
            sg = _slab_rows(xs_ref, gb, tt).astype(BF16)
            y_ref[:, gb * LANES:(gb + 1) * LANES] = jnp.dot(sg, ccd_ref[gb], preferred_element_type=F32)
        run(2, nt - 1)

    ex = plan or dict(ins=[], out_shape=[], scratch=[])
    res = _pallas(
        body, name="s5_fwd", grid=(nt,),
        in_specs=[pl.BlockSpec((tt, D_MODEL), lambda i: (i, 0)), _full_spec(bbd), _full_spec(ccd),
                  _full_spec(a_re), _full_spec(a_im)] + [ANY_SPEC] * len(ex["ins"]),
        out_specs=[pl.BlockSpec((tt, D_MODEL), lambda i: (i, 0)),
                   pl.BlockSpec((1, N_SLAB, N_GBLK, LANES), lambda i: (i, 0, 0, 0))] + [ANY_SPEC] * len(ex["out_shape"]),
        out_shape=[jax.ShapeDtypeStruct((t, D_MODEL), F32), jax.ShapeDtypeStruct((nt, N_SLAB, N_GBLK, LANES), F32)]
        + ex["out_shape"],
        scratch_shapes=[pltpu.VMEM((N_SLAB, N_GBLK * tt, LANES), F32), pltpu.VMEM((N_SLAB, N_GBLK, LANES), F32)]
        + ex["scratch"],
        compiler_params=_cparams("arbitrary"),
    )(u, bbd, ccd, a_re, a_im, *ex["ins"])
    return res[0], res[1], res[2:]


def _s5_bwd(u, dy, du_add, hb, bbd, ccd, a_re, a_im, *, tt=256, plan=None):
    t = u.shape[0]
    nt = t // tt
    half = N_SLAB // 2
    rev = lambda i: (nt - 1 - i, 0)

    def body(*refs):
        i = pl.program_id(0)
        own_in, own_out, own_scr, run = _hosted(plan, i, nt, refs, 8, 5, 3)
        u_ref, dy_ref, dua_ref, hb_ref, bbd_ref, ccd_ref, ar_ref, ai_ref = own_in
        du_ref, dbbd_ref, dccd_ref, dar_ref, dai_ref = own_out
        xs_ref, es_ref, lam_ref = own_scr
        run(0, 0)
        run(1, nt // 2)

        @pl.when(i == 0)
        def _():
            lam_ref[...] = jnp.zeros_like(lam_ref)
            dbbd_ref[...] = jnp.zeros_like(dbbd_ref)
            dccd_ref[...] = jnp.zeros_like(dccd_ref)
            dar_ref[...] = jnp.zeros_like(dar_ref)
            dai_ref[...] = jnp.zeros_like(dai_ref)

        u_bf = u_ref[...].astype(BF16)
        dy_bf = dy_ref[...].astype(BF16)
        _s5_expand(u_bf, bbd_ref, xs_ref, tt)
        ar = [ar_ref[l] for l in range(half)]
        ai = [ai_ref[l] for l in range(half)]
        start = [hb_ref[0, l] for l in range(N_SLAB)]
        _s5_forward_scan(xs_ref, ar, ai, start, tt)
        for gb in range(N_GBLK):
            eg = lax.dot_general(dy_bf[:, gb * LANES:(gb + 1) * LANES], ccd_ref[gb], (((1,), (1,)), ((), ())),
                                 preferred_element_type=F32)
            for l in range(N_SLAB):
                es_ref[l, _block_rows(gb, tt), :] = eg[:, l * LANES:(l + 1) * LANES]

        def lam_step(k, lam):
            rows = _token_rows(tt - 1 - k)
            new_lam = [None] * N_SLAB
            for l in range(half):
                lr = es_ref[l, rows, :] + ar[l] * lam[l] + ai[l] * lam[half + l]
                li = es_ref[half + l, rows, :] - ai[l] * lam[l] + ar[l] * lam[half + l]
                es_ref[l, rows, :] = lr
                es_ref[half + l, rows, :] = li
                new_lam[l], new_lam[half + l] = lr, li
            return tuple(new_lam)

        lam = lax.fori_loop(0, tt, lam_step, tuple(lam_ref[l] for l in range(N_SLAB)), unroll=4)
        for l in range(N_SLAB):
            lam_ref[l] = lam[l]

        def over_tokens(prod):
            return jnp.sum(prod.reshape(tt - 1, N_GBLK, LANES), axis=0)

        later_rows, earlier_rows = pl.ds(N_GBLK, (tt - 1) * N_GBLK), pl.ds(0, (tt - 1) * N_GBLK)
        first = pl.ds(0, N_GBLK)
        for l in range(half):
            lr, li = es_ref[l, later_rows, :], es_ref[half + l, later_rows, :]
            pr, pi = xs_ref[l, earlier_rows, :], xs_ref[half + l, earlier_rows, :]
            lr0, li0 = es_ref[l, first, :], es_ref[half + l, first, :]
            dar_ref[l] += over_tokens(lr * pr + li * pi) + lr0 * start[l] + li0 * start[half + l]
            dai_ref[l] += over_tokens(li * pr - lr * pi) + li0 * start[l] - lr0 * start[half + l]

        for gb in range(N_GBLK):
            cols = slice(gb * LANES, (gb + 1) * LANES)
            lg = _slab_rows(es_ref, gb, tt).astype(BF16)
            sg = _slab_rows(xs_ref, gb, tt).astype(BF16)
            du_ref[:, cols] = (dua_ref[:, cols] + lax.dot_general(
                lg, bbd_ref[gb], (((1,), (1,)), ((), ())), preferred_element_type=F32)).astype(du_ref.dtype)
            dbbd_ref[gb] += lax.dot_general(u_bf[:, cols], lg, (((0,), (0,)), ((), ())), preferred_element_type=F32)
            dccd_ref[gb] += lax.dot_general(sg, dy_bf[:, cols], (((0,), (0,)), ((), ())), preferred_element_type=F32)
        run(2, nt - 1)

    row = pl.BlockSpec((tt, D_MODEL), rev)
    small = pl.BlockSpec((half, N_GBLK, LANES), lambda i: (0, 0, 0))
    ex = plan or dict(ins=[], out_shape=[], scratch=[])
    res = _pallas(
        body, name="s5_bwd", grid=(nt,),
        in_specs=[row, row, row, pl.BlockSpec((1, N_SLAB, N_GBLK, LANES), lambda i: (nt - 1 - i, 0, 0, 0)),
                  _full_spec(bbd), _full_spec(ccd), _full_spec(a_re), _full_spec(a_im)] + [ANY_SPEC] * len(ex["ins"]),
        out_specs=[row, _full_spec(bbd), _full_spec(ccd), small, small] + [ANY_SPEC] * len(ex["out_shape"]),
        out_shape=[jax.ShapeDtypeStruct((t, D_MODEL), BF16), jax.ShapeDtypeStruct(bbd.shape, F32),
                   jax.ShapeDtypeStruct(ccd.shape, F32), jax.ShapeDtypeStruct(a_re.shape, F32),
                   jax.ShapeDtypeStruct(a_im.shape, F32)] + ex["out_shape"],
        scratch_shapes=[pltpu.VMEM((N_SLAB, N_GBLK * tt, LANES), F32), pltpu.VMEM((N_SLAB, N_GBLK * tt, LANES), F32),
                        pltpu.VMEM((N_SLAB, N_GBLK, LANES), F32)] + ex["scratch"],
        compiler_params=_cparams("arbitrary"),
    )(u, dy, du_add, hb, bbd, ccd, a_re, a_im, *ex["ins"])
    return (*res[:5], res[5:])


def _s5_discretize(lam_re, lam_im, log_dt, b_re, b_im, c_re, c_im):
    lr = jnp.minimum(lam_re, -1e-4)
    li = lam_im
    dt = jnp.exp(log_dt)[:, None]
    mag = jnp.exp(lr * dt)
    a_re = mag * jnp.cos(li * dt)
    a_im = mag * jnp.sin(li * dt)
    den = lr * lr + li * li
    nr = a_re - 1.0
    f_re = (nr * lr + a_im * li) / den
    f_im = (a_im * lr - nr * li) / den
    bb_re = f_re[..., None] * b_re - f_im[..., None] * b_im
    bb_im = f_re[..., None] * b_im + f_im[..., None] * b_re
    eye = jnp.eye(GROUPS_PER_BLOCK, dtype=F32)
    bb = jnp.stack([bb_re, bb_im]).reshape(2, N_GBLK, GROUPS_PER_BLOCK, STATE, GROUP_SIZE)
    bbd = jnp.einsum('rgjph,jk->gjhrkp', bb, eye).reshape(N_GBLK, LANES, STATE_COLS)
    cc = jnp.stack([c_re, -c_im]).reshape(2, N_GBLK, GROUPS_PER_BLOCK, GROUP_SIZE, STATE)
    ccd = jnp.einsum('rgjhp,jk->grjpkh', cc, eye).reshape(N_GBLK, STATE_COLS, LANES)

    def dense(a):
        return a.reshape(N_GBLK, N_SLAB // 2, LANES).transpose(1, 0, 2)

    return dense(a_re), dense(a_im), bbd, ccd


def _attn_tiles(tq):
    row = lax.broadcasted_iota(jnp.int32, (tq, tq), 0)
    col = lax.broadcasted_iota(jnp.int32, (tq, tq), 1)
    return col < row, (row > col).astype(BF16), (row >= col).astype(BF16)


def _split_bf16(x):
    hi = x.astype(BF16)
    return hi, (x - hi.astype(F32)).astype(BF16)


def _attn_scores(qs, kj, cs, diag, later, masked):
    zs = [lax.dot_general(qa, kj, (((1,), (1,)), ((), ())), preferred_element_type=F32) * (HEAD_DIM ** -0.5) for qa in qs]
    lks = [-(jnp.maximum(z, 0.0) + jnp.log(1.0 + jnp.exp(-jnp.abs(z)))) for z in zs]
    if masked:
        lks = [jnp.where(diag, lk, 0.0) for lk in lks]
    parts = [_split_bf16(lk) for lk in lks]
    tails = [jnp.dot(hi, later, preferred_element_type=F32) + jnp.dot(lo, later, preferred_element_type=F32)
             for hi, lo in parts]
    ws = [jnp.exp(z + lk + tail + c) for z, lk, tail, c in zip(zs, lks, tails, cs)]
    if masked:
        ws = [jnp.where(diag, w, 0.0) for w in ws]
    return zs, lks, ws


def _head_masks(width=LANES):
    lane = lax.broadcasted_iota(jnp.int32, (1, width), 1)
    return [(lane >= a * HEAD_DIM) & (lane < (a + 1) * HEAD_DIM) for a in range(width // HEAD_DIM)]


def _any_alive(cs):
    alive = jnp.max(cs[0]) > EXP_UNDERFLOW
    for c in cs[1:]:
        alive = alive | (jnp.max(c) > EXP_UNDERFLOW)
    return alive


def _attn_fwd(q, k, v, *, tq=256):
    t = q.shape[0]
    tq = min(tq, t)
    width = ATTN_LANES

    def body(q_ref, k_ref, v_ref, o_ref, ol_ref):
        i = pl.program_id(1)
        diag, later, _ = _attn_tiles(tq)
        masks = _head_masks(width)
        q2 = q_ref[...]
        qs = [jnp.where(hm, q2, jnp.zeros_like(q2)) for hm in masks]

        def block(j, state, masked):
            rows = pl.ds(pl.multiple_of(j * tq, tq), tq)
            kj, vj = k_ref[rows, :], v_ref[rows, :]
            _, lks, ws = _attn_scores(qs, kj, [h[0] for h in state], diag, later, masked)
            parts = [_split_bf16(w) for w in ws]
            return tuple((c + jnp.sum(lk, axis=1, keepdims=True),
                          acc + jnp.dot(w_hi, vj, preferred_element_type=F32),
                          low + jnp.dot(w_lo, vj, preferred_element_type=F32))
                         for (c, acc, low), lk, (w_hi, w_lo) in zip(state, lks, parts))

        zero = jnp.zeros((tq, width), F32)
        state = block(i, tuple((jnp.zeros((tq, 1), F32), zero, zero) for _ in masks), True)

        def cond(st):
            return (st[0] >= 0) & _any_alive([h[0] for h in st[1]])

        def step(st):
            return st[0] - 1, block(st[0], st[1], False)

        _, state = lax.while_loop(cond, step, (i - 1, state))
        o_ref[...] = sum(jnp.where(hm, h[1], 0.0) for hm, h in zip(masks, state))
        ol_ref[...] = sum(jnp.where(hm, h[2], 0.0) for hm, h in zip(masks, state))

    qspec = pl.BlockSpec((tq, width), lambda h, i: (i, h))
    kspec = pl.BlockSpec((t, width), lambda h, i: (0, h))
    full = jax.ShapeDtypeStruct((t, D_MODEL), F32)
    return _pallas(
        body, name="attn_fwd", grid=(D_MODEL // width, t // tq), in_specs=[qspec, kspec, kspec], out_specs=[qspec, qspec],
        out_shape=[full, full], compiler_params=_cparams("parallel", "arbitrary"),
    )(q, k, v)


def _attn_bwd(q, k, v, o, o_low, do, *, tq=256):
    t = q.shape[0]
    tq = min(tq, t)
    width = ATTN_LANES

    def body(q_ref, k_ref, v_ref, o_ref, ol_ref, do_ref, dq_ref, dk_ref, dv_ref):
        i = pl.program_id(1)

        @pl.when(i == 0)
        def _():
            dk_ref[...] = jnp.zeros_like(dk_ref)
            dv_ref[...] = jnp.zeros_like(dv_ref)

        diag, later, later_eq = _attn_tiles(tq)
        masks = _head_masks(width)
        q2 = q_ref[...]
        do2 = do_ref[...]
        dd = do2.astype(BF16).astype(F32) * (o_ref[...] + ol_ref[...])
        qs = [jnp.where(hm, q2, jnp.zeros_like(q2)) for hm in masks]
        dos = [jnp.where(hm, do2, 0.0).astype(BF16) for hm in masks]
        totals = [jnp.sum(jnp.where(hm, dd, 0.0), axis=1, keepdims=True) for hm in masks]

        def block(j, state, masked):
            rows = pl.ds(pl.multiple_of(j * tq, tq), tq)
            kj, vj = k_ref[rows, :], v_ref[rows, :]
            nt = (((1,), (1,)), ((), ()))
            tn = (((0,), (0,)), ((), ()))
            zs, lks, ws = _attn_scores(qs, kj, [h[0] for h in state], diag, later, masked)
            dws = [lax.dot_general(doa, vj, nt, preferred_element_type=F32) for doa in dos]
            gs = [dw * w for dw, w in zip(dws, ws)]
            parts = [_split_bf16(g) for g in gs]
            from_here = [jnp.dot(hi, later_eq, preferred_element_type=F32) + jnp.dot(lo, later_eq, preferred_element_type=F32)
                         for hi, lo in parts]
            dzs = []
            for z, lk, g, fh, total, h in zip(zs, lks, gs, from_here, totals, state):
                beta = jnp.exp(z + lk)
                dz = g * (1.0 - beta) - beta * (total - h[1] - fh)
                if masked:
                    dz = jnp.where(diag, dz, 0.0)
                dzs.append((dz * (HEAD_DIM ** -0.5)).astype(BF16))
            dk = sum(lax.dot_general(dz, qa, tn, preferred_element_type=F32) for dz, qa in zip(dzs, qs))
            dv = sum(lax.dot_general(w.astype(BF16), doa, tn, preferred_element_type=F32) for w, doa in zip(ws, dos))
            new = [(c + jnp.sum(lk, axis=1, keepdims=True), r + jnp.sum(g, axis=1, keepdims=True),
                    acc + jnp.dot(dz, kj, preferred_element_type=F32))
                   for (c, r, acc), lk, g, dz in zip(state, lks, gs, dzs)]
            dk_ref[rows, :] += dk
            dv_ref[rows, :] += dv
            return tuple(new)

        zc = jnp.zeros((tq, 1), F32)
        state = block(i, tuple((zc, zc, jnp.zeros((tq, width), F32)) for _ in masks), True)

        def cond(st):
            return (st[0] >= 0) & _any_alive([h[0] for h in st[1]])

        def step(st):
            return st[0] - 1, block(st[0], st[1], False)

        _, state = lax.while_loop(cond, step, (i - 1, state))
        dq_ref[...] = sum(jnp.where(hm, h[2], 0.0) for hm, h in zip(masks, state))

    qspec = pl.BlockSpec((tq, width), lambda h, i: (i, h))
    kspec = pl.BlockSpec((t, width), lambda h, i: (0, h))
    kin = pl.BlockSpec((t, width), lambda h, i: (0, h), pipeline_mode=pl.Buffered(1))
    full = jax.ShapeDtypeStruct((t, D_MODEL), F32)
    return _pallas(
        body, name="attn_bwd", grid=(D_MODEL // width, t // tq), in_specs=[qspec, kin, kin, qspec, qspec, qspec],
        out_specs=[qspec, kspec, kspec], out_shape=[full, full, full],
        compiler_params=_cparams("parallel", "arbitrary"),
    )(q, k, v, o, o_low, do)


N_CHIP = 4
MESH_ID = pl.DeviceIdType.MESH
ANY_SPEC = pl.BlockSpec(memory_space=pl.ANY)


def _position():
    return lax.axis_index("x"), lax.axis_index("y"), lax.axis_index("c")


def _other_chips(x, y):
    return [(1 - x, y), (x, 1 - y), (1 - x, 1 - y)]


def _gather_plan(arrs):
    n = len(arrs)

    def parts(ins, outs, sems):
        send_sems, recv_sems, local_sems = sems
        x, y, c = _position()
        me, sibling = (x, y, c), (x, y, 1 - c)
        chips = _other_chips(x, y)

        def copy(a, k, block, to, src=None):
            slot = outs[a].at[4 * block[0] + 2 * block[1] + block[2]]
            return pltpu.make_async_remote_copy(
                src_ref=slot if src is None else src, dst_ref=slot, send_sem=send_sems.at[a, k],
                recv_sem=recv_sems.at[a, k], device_id=to, device_id_type=MESH_ID)

        local = [pltpu.make_async_copy(ins[a], outs[a].at[4 * x + 2 * y + c], local_sems.at[a]) for a in range(n)]
        first = []
        for a in range(n):
            first.append(copy(a, 0, me, sibling, src=ins[a]))
            first += [copy(a, 1 + j, me, (*chip, c), src=ins[a]) for j, chip in enumerate(chips)]
        passed = [copy(a, 4 + j, (*chip, c), sibling) for a in range(n) for j, chip in enumerate(chips)]
        return me, sibling, chips, c, copy, local, first, passed

    def start(ins, outs, sems):
        *_, local, first, _ = parts(ins, outs, sems)
        for cp in local + first:
            cp.start()

    def forward(ins, outs, sems):
        me, _, chips, c, copy, _, _, passed = parts(ins, outs, sems)
        for a in range(n):
            for j, chip in enumerate(chips):
                copy(a, 1 + j, (*chip, c), me).wait_recv()
                passed[a * len(chips) + j].start()

    def finish(ins, outs, sems):
        me, sibling, chips, c, copy, local, first, passed = parts(ins, outs, sems)
        for a in range(n):
            copy(a, 0, sibling, me).wait_recv()
            for j, chip in enumerate(chips):
                copy(a, 4 + j, (*chip, 1 - c), me).wait_recv()
        for cp in first + passed:
            cp.wait_send()
        for cp in local:
            cp.wait()

    return dict(ins=list(arrs), out_shape=[jax.ShapeDtypeStruct((N_DEV,) + a.shape, a.dtype) for a in arrs],
                scratch=[pltpu.SemaphoreType.DMA((n, N_DEV - 1)), pltpu.SemaphoreType.DMA((n, N_DEV - 1)),
                         pltpu.SemaphoreType.DMA((n,))],
                phases=(start, forward, finish))


def _run_plan(plan, *, name):
    n_in, n_out = len(plan["ins"]), len(plan["out_shape"])

    def body(*refs):
        ins, outs, sems = refs[:n_in], refs[n_in:n_in + n_out], refs[n_in + n_out:]
        for phase in plan["phases"]:
            phase(ins, outs, sems)

    return _pallas(body, name=name, in_specs=[ANY_SPEC] * n_in, out_specs=[ANY_SPEC] * n_out,
                   out_shape=plan["out_shape"], scratch_shapes=plan["scratch"])(*plan["ins"])


def _gather(arrs, *, name):
    return _run_plan(_gather_plan(arrs), name=name)


def _pair_exchange(arrs, *, name):
    n = len(arrs)

    def body(*refs):
        ins, got = refs[:n], refs[n:2 * n]
        send_sems, recv_sems = refs[2 * n:]
        x, y, c = _position()
        sends = []
        for a in range(n):
            for ch in range(N_CHIP):
                sends.append(pltpu.make_async_remote_copy(
                    src_ref=ins[a].at[2 * ch + (1 - c)], dst_ref=got[a].at[ch], send_sem=send_sems.at[a, ch],
                    recv_sem=recv_sems.at[a, ch], device_id=(x, y, 1 - c), device_id_type=MESH_ID))
        for cp in sends:
            cp.start()
        for cp in sends:
            cp.wait_send()
            cp.wait_recv()

    return _pallas(
        body, name=name, in_specs=[ANY_SPEC] * n, out_specs=[ANY_SPEC] * n,
        out_shape=[jax.ShapeDtypeStruct((N_CHIP,) + a.shape[1:], a.dtype) for a in arrs],
        scratch_shapes=[pltpu.SemaphoreType.DMA((n, N_CHIP)), pltpu.SemaphoreType.DMA((n, N_CHIP))],
    )(*arrs)


def _chip_exchange_plan(arrs):
    n = len(arrs)

    def parts(ins, outs, sems):
        send_sems, recv_sems, local_sems = sems
        x, y, c = _position()
        mine = 2 * x + y
        sends, recvs, locals_ = [], [], []
        for a in range(n):
            locals_.append(pltpu.make_async_copy(ins[a].at[mine], outs[a].at[mine], local_sems.at[a]))
            for j, (px, py) in enumerate(_other_chips(x, y)):
                sends.append(pltpu.make_async_remote_copy(
                    src_ref=ins[a].at[2 * px + py], dst_ref=outs[a].at[mine], send_sem=send_sems.at[a, j],
                    recv_sem=recv_sems.at[a, j], device_id=(px, py, c), device_id_type=MESH_ID))
                recvs.append(pltpu.make_async_remote_copy(
                    src_ref=ins[a].at[2 * px + py], dst_ref=outs[a].at[2 * px + py], send_sem=send_sems.at[a, j],
                    recv_sem=recv_sems.at[a, j], device_id=(px, py, c), device_id_type=MESH_ID))
        return sends, recvs, locals_

    def start(ins, outs, sems):
        sends, _, locals_ = parts(ins, outs, sems)
        for cp in locals_ + sends:
            cp.start()

    def nothing(ins, outs, sems):
        pass

    def finish(ins, outs, sems):
        sends, recvs, locals_ = parts(ins, outs, sems)
        for cp in sends:
            cp.wait_send()
        for cp in recvs:
            cp.wait_recv()
        for cp in locals_:
            cp.wait()

    return dict(ins=list(arrs), out_shape=[jax.ShapeDtypeStruct(a.shape, a.dtype) for a in arrs],
                scratch=[pltpu.SemaphoreType.DMA((n, N_CHIP - 1)), pltpu.SemaphoreType.DMA((n, N_CHIP - 1)),
                         pltpu.SemaphoreType.DMA((n,))],
                phases=(start, nothing, finish))


def _pair_sum(mine, got, core, *, name):
    s_, r, c = got.shape
    tr = r if r <= 256 else 256

    def body(core_ref, a_ref, b_ref, o_ref):
        o_ref[...] = a_ref[...] + b_ref[...]

    quarter = pl.BlockSpec((s_, tr, c), lambda i, core_ref: (0, i, 0))
    return _pallas(
        body, name=name, out_shape=jax.ShapeDtypeStruct(got.shape, got.dtype),
        grid_spec=pltpu.PrefetchScalarGridSpec(
            num_scalar_prefetch=1, grid=(r // tr,),
            in_specs=[pl.BlockSpec((s_, None, tr, c), lambda i, core_ref: (0, core_ref[0], i, 0)), quarter],
            out_specs=quarter),
        compiler_params=_cparams("parallel"),
    )(core, mine.reshape(s_, 2, r, c), got)


def _pair_sums(arrs, names, *, name):
    got = _pair_exchange(arrs, name=name)
    core = lax.axis_index("c").astype(jnp.int32).reshape(1)
    return [_pair_sum(a, g_, core, name="pair_sum_" + n) for a, g_, n in zip(arrs, got, names)]


def _adamw(parts, w, m, v, *, name):
    s, r, c = parts.shape
    tr = r if r * c <= 256 * D_MODEL else 256

    def body(p_ref, w_ref, m_ref, v_ref, g_ref, d_ref, nm_ref, nv_ref):
        g = p_ref[0]
        for j in range(1, s):
            g = g + p_ref[j]
        nm = ADAM_B1 * m_ref[...] + (1.0 - ADAM_B1) * g
        nv = ADAM_B2 * v_ref[...] + (1.0 - ADAM_B2) * (g * g)
        m_hat = nm / (1.0 - ADAM_B1 ** ADAM_STEP)
        v_hat = nv / (1.0 - ADAM_B2 ** ADAM_STEP)
        g_ref[...] = g
        d_ref[...] = -ADAM_LR * (m_hat / (jnp.sqrt(v_hat) + ADAM_EPS) + ADAM_WD * w_ref[...])
        nm_ref[...] = nm
        nv_ref[...] = nv

    spec = pl.BlockSpec((tr, c), lambda i: (i, 0))
    one = jax.ShapeDtypeStruct((r, c), F32)
    return _pallas(
        body, name=name, grid=(r // tr,), in_specs=[pl.BlockSpec((s, tr, c), lambda i: (0, i, 0)), spec, spec, spec],
        out_specs=[spec] * 4, out_shape=[one] * 4, compiler_params=_cparams("parallel"),
    )(parts, w, m, v)


def _cols_to_blocks(g, n_blk):
    r = g.shape[0]
    return g.reshape(r, n_blk, g.shape[1] // n_blk).transpose(1, 0, 2)


def _blocks_to_cols(gathered):
    n_blk, r, c = gathered.shape
    return gathered.transpose(1, 0, 2).reshape(r, n_blk * c)


def kernel(x, p, a_norm_pre, a_norm_post, a_w_in, a_lam_re, a_lam_im, a_log_dt, a_b_re, a_b_im, a_c_re, a_c_im, a_d_skip, a_w_glu, a_b_glu, a_w_out, kv_norm, w_kv, b_norm_pre, b_norm_post, b_w_in, b_w_out, ple_w_proj, ple_w_gate, loss_target, m_a_norm_pre, m_a_norm_post, m_a_w_in, m_a_lam_re, m_a_lam_im, m_a_log_dt, m_a_b_re, m_a_b_im, m_a_c_re, m_a_c_im, m_a_d_skip, m_a_w_glu, m_a_b_glu, m_a_w_out, m_kv_norm, m_w_kv, m_b_norm_pre, m_b_norm_post, m_b_w_in, m_b_w_out, m_ple_w_proj, m_ple_w_gate, v_a_norm_pre, v_a_norm_post, v_a_w_in, v_a_lam_re, v_a_lam_im, v_a_log_dt, v_a_b_re, v_a_b_im, v_a_c_re, v_a_c_im, v_a_d_skip, v_a_w_glu, v_a_b_glu, v_a_w_out, v_kv_norm, v_w_kv, v_b_norm_pre, v_b_norm_post, v_b_w_in, v_b_w_out, v_ple_w_proj, v_ple_w_gate):
    weights = dict(a_norm_pre=a_norm_pre, a_norm_post=a_norm_post, a_w_in=a_w_in, a_lam_re=a_lam_re, a_lam_im=a_lam_im, a_log_dt=a_log_dt, a_b_re=a_b_re, a_b_im=a_b_im, a_c_re=a_c_re, a_c_im=a_c_im, a_d_skip=a_d_skip, a_w_glu=a_w_glu, a_b_glu=a_b_glu, a_w_out=a_w_out, kv_norm=kv_norm, w_kv=w_kv, b_norm_pre=b_norm_pre, b_norm_post=b_norm_post, b_w_in=b_w_in, b_w_out=b_w_out, ple_w_proj=ple_w_proj, ple_w_gate=ple_w_gate)
    mom_m = dict(a_norm_pre=m_a_norm_pre, a_norm_post=m_a_norm_post, a_w_in=m_a_w_in, a_lam_re=m_a_lam_re, a_lam_im=m_a_lam_im, a_log_dt=m_a_log_dt, a_b_re=m_a_b_re, a_b_im=m_a_b_im, a_c_re=m_a_c_re, a_c_im=m_a_c_im, a_d_skip=m_a_d_skip, a_w_glu=m_a_w_glu, a_b_glu=m_a_b_glu, a_w_out=m_a_w_out, kv_norm=m_kv_norm, w_kv=m_w_kv, b_norm_pre=m_b_norm_pre, b_norm_post=m_b_norm_post, b_w_in=m_b_w_in, b_w_out=m_b_w_out, ple_w_proj=m_ple_w_proj, ple_w_gate=m_ple_w_gate)
    mom_v = dict(a_norm_pre=v_a_norm_pre, a_norm_post=v_a_norm_post, a_w_in=v_a_w_in, a_lam_re=v_a_lam_re, a_lam_im=v_a_lam_im, a_log_dt=v_a_log_dt, a_b_re=v_a_b_re, a_b_im=v_a_b_im, a_c_re=v_a_c_re, a_c_im=v_a_c_im, a_d_skip=v_a_d_skip, a_w_glu=v_a_w_glu, a_b_glu=v_a_b_glu, a_w_out=v_a_w_out, kv_norm=v_kv_norm, w_kv=v_w_kv, b_norm_pre=v_b_norm_pre, b_norm_post=v_b_norm_post, b_w_in=v_b_w_in, b_w_out=v_b_w_out, ple_w_proj=v_ple_w_proj, ple_w_gate=v_ple_w_gate)
    names = list(weights)

    t = x.shape[1]
    x0 = x[0]
    p0, p1 = p[0, 0], p[1, 0]
    tgt = loss_target[0]

    mats = dict(a_w_in=a_w_in[0], a_w_glu=a_w_glu[0], a_w_out=a_w_out[0], w_kv=w_kv, b_w_in=b_w_in[0],
                b_w_out=b_w_out[0], ple_w_proj=ple_w_proj.reshape(2 * 256, LANES),
                ple_w_gate=ple_w_gate.reshape(2 * LANES, D_MODEL))
    vec_names = ["a_norm_pre", "a_norm_post", "a_d_skip", "a_b_glu"]
    vec_pack = jnp.concatenate([weights[n] for n in vec_names], axis=0)
    late = ["w_kv", "b_w_in", "b_w_out"]
    early = [n for n in mats if n not in late]
    gathered = _gather([mats[n].astype(BF16) for n in early] + [vec_pack], name="gather_weights")
    gw = dict(zip(early, gathered[:-1]))
    vec_full = gathered[-1].transpose(1, 0, 2).reshape(len(vec_names), D_MODEL)
    g_apre, g_apost, d_skip, b_glu = (vec_full[i:i + 1] for i in range(4))
    w_ain = gw["a_w_in"]
    w_glu = gw["a_w_glu"].reshape(D_MODEL, D_MODEL)
    w_out = gw["a_w_out"].reshape(D_MODEL, D_MODEL)
    wp = gw["ple_w_proj"].reshape(N_DEV, 2, 256, LANES)
    w_p0, w_p1 = _blocks_to_cols(wp[:, 0]), _blocks_to_cols(wp[:, 1])
    wg = gw["ple_w_gate"].reshape(N_DEV, 2, LANES, D_MODEL)
    w_g0, w_g1 = wg[:, 0].reshape(D_MODEL, D_MODEL), wg[:, 1].reshape(D_MODEL, D_MODEL)
    g_kv, g_bpre, g_bpost = kv_norm.reshape(1, D_MODEL), b_norm_pre, b_norm_post

    s5_params = (a_lam_re[0], a_lam_im[0], a_log_dt[0], a_b_re[0], a_b_im[0], a_c_re[0], a_c_im[0])
    (a_re, a_im, bbd, ccd), s5_vjp = jax.vjp(_s5_discretize, *s5_params)
    bbd_bf, ccd_bf = bbd.astype(BF16), ccd.astype(BF16)

    f_a2 = lambda ys, u, d: (_gelu(ys + d * u),)
    f_a3 = lambda g, s, ga, b: (g * _sigmoid(s + b) * _silu(ga),)
    f_rms = lambda y, g: (_rms(y, g),)
    f_k = lambda x2, gk, gp: (_rms(x2, gk), _rms(x2, gp))
    f_b3 = lambda o, gb: (o * _silu(gb),)

    def f_a_in(x0_, g_, w):
        h = _rms(x0_, g_).astype(BF16)
        return h, _dot(h, w[:4]), _dot(h, w[4:])

    h0, u, ga = _ew(f_a_in, [x0], [g_apre], [BF16, F32, F32], mats=[w_ain], name="a_norm_proj")
    ys, hb, late_w = _s5_fwd(u, bbd_bf, ccd_bf, a_re, a_im, plan=_gather_plan([mats[n].astype(BF16) for n in late]))
    w_kvb, w_bin = late_w[0], late_w[1]
    w_bout = late_w[2].reshape(D_MODEL, D_MODEL)

    def f_a_glu(ys_, u_, d_, w):
        g = f_a2(ys_, u_, d_)[0].astype(BF16)
        return g, _dot(g, w)

    g_act, s_glu = _ew(f_a_glu, [ys, u], [d_skip], [BF16, F32], mats=[w_glu], name="a_gelu_glu")
    def f_a_out(ys_, u_, s_, ga_, x0_, d_, b_, g_, w):
        (y3_,) = f_a3(_gelu(ys_ + d_ * u_), s_, ga_, b_)
        y3_ = y3_.astype(BF16)
        y4_ = _dot(y3_, w)
        return y3_, y4_, x0_ + _rms(y4_, g_)

    y3, y4, x1 = _ew(f_a_out, [ys, u, s_glu, ga, x0], [d_skip, b_glu, g_apost], [BF16, F32, F32], mats=[w_out],
                     name="a_gate_out_norm")

    def f_ple(xin, p_, wg_, wp_):
        pg, pp = _dot(xin, wg_), _dot(p_, wp_)
        return xin + _sigmoid(pg) * pp, pg, pp

    x2, pg0, pp0 = _ew(f_ple, [x1, p0], [], [F32, F32, F32], mats=[w_g0, w_p0], name="ple0")
    def f_b_in(x2_, gk, gp, wkv_, win_):
        kvn_, h1_ = (t_.astype(BF16) for t_ in f_k(x2_, gk, gp))
        return kvn_, h1_, _dot(kvn_, wkv_[:4]), _dot(kvn_, wkv_[4:]), _dot(h1_, win_[:4]), _dot(h1_, win_[4:])

    kvn, h1, k_, v_, q_, gb = _ew(f_b_in, [x2], [g_kv, g_bpre], [BF16, BF16, BF16, BF16, BF16, F32],
                                  mats=[w_kvb, w_bin], name="b_norms_proj")
    o, o_low = _attn_fwd(q_, k_, v_)

    def f_b_out(o_, gb_, x2_, g_, w):
        y5in_ = f_b3(o_, gb_)[0].astype(BF16)
        y5_ = _dot(y5in_, w)
        return y5in_, y5_, x2_ + _rms(y5_, g_)

    y5in, y5, x3 = _ew(f_b_out, [o, gb, x2], [g_bpost], [BF16, F32, F32], mats=[w_bout], name="b_gate_out_norm")

    def f_ple_loss(xin, p_, tg, wg_, wp_):
        pg, pp = _dot(xin, wg_), _dot(p_, wp_)
        err = xin + _sigmoid(pg) * pp - tg
        return err * (1.0 / D_MODEL), pg, pp, (0.5 / D_MODEL) * jnp.sum(err * err).reshape(1, 1)

    dx4, pg1, pp1, loss_part = _ew(f_ple_loss, [x3, p1, tgt], [], [F32, F32, F32], mats=[w_g1, w_p1], n_red=1,
                                   name="ple1_loss")
    loss = lax.psum(loss_part[0, 0], ("x", "y", "c"))

    grads = {}
    def f_ple_bwd(dx, pg, pp, wg_):
        sg = _sigmoid(pg)
        dpg = (dx * pp * sg * (1.0 - sg)).astype(BF16)
        return dx + _dot_t(dpg, wg_), dpg, dx * sg

    def f_norm_bwd_mm(y, dx, g_, w):
        _, vjp = jax.vjp(_rms, y, g_)
        dy, dg = vjp(dx)
        dy = dy.astype(BF16)
        return _dot_t(dy, w), dy, dg

    dx3, dpg1, dpp1 = _ew(f_ple_bwd, [dx4, pg1, pp1], [], [F32, BF16, BF16], mats=[w_g1], name="ple1_bwd")
    d_wg1 = _mm(x3, dpg1, ta=True, name="ple1_gate_dw")
    d_wp1 = _mm(p1, dpp1, ta=True, name="ple1_proj_dw")
    def f_b_out_bwd(y, dx, o_, gb_, g_, w):
        dyin, dy, dg = f_norm_bwd_mm(y, dx, g_, w)
        _, vjp = jax.vjp(lambda a, b: f_b3(a, b)[0], o_, gb_)
        do_, dgb_ = vjp(dyin)
        return dy, do_, dgb_, dg

    dy5, do, dgb, d_gbpost = _ew(f_b_out_bwd, [y5, dx3, o, gb], [g_bpost], [BF16, F32, BF16], mats=[w_bout], n_red=1,
                                 name="b_out_bwd")
    grads["b_w_out"] = _mm(y5in, dy5, ta=True, name="b_out_dw").reshape(N_DEV, LANES, D_MODEL)
    dq, dk, dv = _attn_bwd(q_, k_, v_, o, o_low, do)
    grads["b_w_in"] = _mm_dw_cols(h1, [dq, dgb], 4, name="b_proj_dw")
    grads["w_kv"] = _mm_dw_cols(kvn, [dk, dv], 4, name="kv_dw")
    def f_b_in_bwd(x2_, dx, dk_, dv_, dq_, dgb_, gk, gp, wkv_, win_):
        dkvn = _dot_t(dk_, wkv_[:4]) + _dot_t(dv_, wkv_[4:])
        dh1 = _dot_t(dq_, win_[:4]) + _dot_t(dgb_, win_[4:])
        _, vjp = jax.vjp(f_k, x2_, gk, gp)
        dx2_, dgk, dgp = vjp((dkvn, dh1))
        return dx + dx2_, dgk, dgp

    dx2, d_gkv, d_gbpre = _ew(f_b_in_bwd, [x2, dx3, dk, dv, dq, dgb], [g_kv, g_bpre], [F32],
                              mats=[w_kvb, w_bin], n_red=2, name="b_norms_proj_bwd")
    dx1, dpg0, dpp0 = _ew(f_ple_bwd, [dx2, pg0, pp0], [], [F32, BF16, BF16], mats=[w_g0], name="ple0_bwd")
    d_wg0 = _mm(x1, dpg0, ta=True, name="ple0_gate_dw")
    d_wp0 = _mm(p0, dpp0, ta=True, name="ple0_proj_dw")
    grads["ple_w_gate"] = jnp.stack([d_wg0.reshape(N_DEV, LANES, D_MODEL), d_wg1.reshape(N_DEV, LANES, D_MODEL)],
                                    axis=1).reshape(N_DEV, 2 * LANES, D_MODEL)
    grads["ple_w_proj"] = jnp.stack([_cols_to_blocks(d_wp0, N_DEV), _cols_to_blocks(d_wp1, N_DEV)],
                                    axis=1).reshape(N_DEV, 2 * 256, LANES)
    dy3, dy4, d_gapost = _ew(f_norm_bwd_mm, [y4, dx1], [g_apost], [F32, BF16], mats=[w_out], n_red=1,
                             name="a_norm_post_bwd")
    grads["a_w_out"] = _mm(y3, dy4, ta=True, name="a_out_dw").reshape(N_DEV, LANES, D_MODEL)
    def f_a_gate_bwd(ys_, u_, s_, ga_, ct, d_, b_, w):
        g, vjp_gelu = jax.vjp(lambda a, b, c: f_a2(a, b, c)[0], ys_, u_, d_)
        _, vjp_gate = jax.vjp(lambda a, b, c, e: f_a3(a, b, c, e)[0], g, s_, ga_, b_)
        dg, ds_, dga_, db = vjp_gate(ct)
        ds_ = ds_.astype(BF16)
        dys_, du_, dd = vjp_gelu(dg + _dot_t(ds_, w))
        return ds_, dga_, dys_, du_, db, dd

    ds, dga, dys, du_elem, d_bglu, d_dskip = _ew(f_a_gate_bwd, [ys, u, s_glu, ga, dy3], [d_skip, b_glu],
                                                 [BF16, BF16, F32, F32], mats=[w_glu], n_red=2, name="a_gate_bwd")
    grads["a_w_glu"] = _mm(g_act, ds, ta=True, name="a_glu_dw").reshape(N_DEV, LANES, D_MODEL)
    late_sums = _pair_sums([grads[n] for n in late], late, name="pair_exchange_late")
    du, d_bbd, d_ccd, d_are, d_aim, late_parts = _s5_bwd(u, dys, du_elem, hb, bbd_bf, ccd_bf, a_re, a_im,
                                                         plan=_chip_exchange_plan(late_sums))
    grads["a_w_in"] = _mm_dw_cols(h0, [du, dga], 4, name="a_proj_dw")
    def f_a_in_bwd(x0_, dx, du_, dga_, g_, w):
        dh0 = _dot_t(du_, w[:4]) + _dot_t(dga_, w[4:])
        _, vjp = jax.vjp(_rms, x0_, g_)
        dx0, dg = vjp(dh0)
        return dx + dx0, dg

    grad_x, d_gapre = _ew(f_a_in_bwd, [x0, dx1, du, dga], [g_apre], [F32], mats=[w_ain], n_red=1,
                          name="a_norm_proj_bwd")

    d_s5 = s5_vjp((d_are, d_aim, d_bbd, d_ccd))
    small = dict(zip(["a_lam_re", "a_lam_im", "a_log_dt", "a_b_re", "a_b_im", "a_c_re", "a_c_im"], d_s5))
    small.update(kv_norm=d_gkv, b_norm_pre=d_gbpre, b_norm_post=d_gbpost)
    small_names = list(small)
    small_flat = jnp.concatenate([small[n].reshape(-1) for n in small_names])
    small_pack = jnp.pad(small_flat, (0, -small_flat.size % (8 * LANES))).reshape(-1, LANES)
    vec_grads = jnp.concatenate([d_gapre, d_gapost, d_dskip, d_bglu], axis=0)
    vec_scatter = vec_grads.reshape(len(vec_names), N_DEV, LANES).transpose(1, 0, 2)
    early_sums = _pair_sums([grads[n] for n in early] + [vec_scatter], early + ["vectors"], name="pair_exchange")
    scattered = _run_plan(_chip_exchange_plan(early_sums), name="chip_exchange")
    (small_all,) = _gather([small_pack], name="gather_small_grads")

    out_g, out_d, out_m, out_v = {}, {}, {}, {}

    def update(n, parts):
        shape = weights[n].shape
        rc = parts.shape[1:]
        res = _adamw(parts, weights[n].reshape(rc), mom_m[n].reshape(rc), mom_v[n].reshape(rc), name="adamw_" + n)
        out_g[n], out_d[n], out_m[n], out_v[n] = (r.reshape(shape) for r in res)

    for n, parts in list(zip(early, scattered[:-1])) + list(zip(late, late_parts)):
        update(n, parts)
    for i, n in enumerate(vec_names):
        update(n, scattered[-1][:, i:i + 1, :])
    off = 0
    for n in small_names:
        size = small[n].size
        rc = (size // LANES, LANES) if size % LANES == 0 else (1, size)
        update(n, small_all.reshape(N_DEV, -1)[:, off:off + size].reshape((N_DEV,) + rc))
        off += size

    return (loss, grad_x[None], *[out_g[n] for n in names], *[out_d[n] for n in names],
            *[out_m[n] for n in names], *[out_v[n] for n in names])
```

```python
import functools
import math

import jax
import jax.numpy as jnp
from jax import lax
from jax.experimental import pallas as pl
from jax.experimental.pallas import tpu as pltpu

F32 = jnp.float32
BF16 = jnp.bfloat16

N_DEV = 8
D_MODEL = 1024
GROUP_SIZE = 16
N_GROUPS = D_MODEL // GROUP_SIZE
STATE = 64
HEAD_DIM = 64
EPS = 1e-6
LANES = 128
GROUPS_PER_BLOCK = LANES // GROUP_SIZE
N_GBLK = N_GROUPS // GROUPS_PER_BLOCK
STATE_COLS = 2 * GROUPS_PER_BLOCK * STATE
N_SLAB = STATE_COLS // LANES
VMEM_LIMIT = 56 * 1024 * 1024
EXP_UNDERFLOW = -104.0
ATTN_LANES = 256
MM_TILE = 1024
EW_ROWS_BYTES = 24 * 1024 * 1024

ADAM_LR = 0.001
ADAM_B1 = 0.9
ADAM_B2 = 0.999
ADAM_EPS = 1e-08
ADAM_WD = 0.01
ADAM_STEP = 10


def _pallas(body, **kw):
    return pl.pallas_call(body, **kw)


def _cparams(*sem):
    return pltpu.CompilerParams(dimension_semantics=sem, vmem_limit_bytes=VMEM_LIMIT)


def _mm(a, b, *, ta=False, tb=False, add=None, out_dtype=F32, name):
    m, k = (a.shape[1], a.shape[0]) if ta else a.shape
    n = b.shape[0] if tb else b.shape[1]
    tm, tn, tk = min(m, MM_TILE), min(n, MM_TILE), min(k, MM_TILE)
    nk = k // tk
    grid = (m // tm, n // tn, nk)
    a_spec = pl.BlockSpec((tk, tm), lambda i, j, l: (l, i)) if ta else pl.BlockSpec((tm, tk), lambda i, j, l: (i, l))
    b_spec = pl.BlockSpec((tn, tk), lambda i, j, l: (j, l)) if tb else pl.BlockSpec((tk, tn), lambda i, j, l: (l, j))
    o_spec = pl.BlockSpec((tm, tn), lambda i, j, l: (i, j))
    dims = (((0 if ta else 1,), (1 if tb else 0,)), ((), ()))
    has_add = add is not None
    assert nk == 1 or (out_dtype == F32 and not has_add)

    def body(*refs):
        a_ref, b_ref, o_ref = refs[0], refs[1], refs[-1]
        part = lax.dot_general(a_ref[...].astype(BF16), b_ref[...].astype(BF16), dims, preferred_element_type=F32)
        if nk == 1:
            if has_add:
                part = part + refs[2][...].astype(F32)
            o_ref[...] = part.astype(out_dtype)
        else:
            l = pl.program_id(2)

            @pl.when(l == 0)
            def _():
                o_ref[...] = part

            @pl.when(l > 0)
            def _():
                o_ref[...] += part

    ins = [a, b] + ([add] if has_add else [])
    specs = [a_spec, b_spec] + ([o_spec] if has_add else [])
    return _pallas(
        body, name=name, grid=grid, in_specs=specs, out_specs=o_spec,
        out_shape=jax.ShapeDtypeStruct((m, n), out_dtype),
        compiler_params=_cparams("parallel", "parallel", "arbitrary"),
    )(*ins)


def _mm_dw_cols(a, bs, n_blk, *, name):
    t, m = a.shape
    n = bs[0].shape[1]
    c = n // n_blk
    tk = min(t, MM_TILE)
    nb = len(bs)

    def body(*refs):
        a_ref, b_refs, o_ref = refs[0], refs[1:1 + nb], refs[-1]
        l = pl.program_id(0)

        @pl.when(l == 0)
        def _():
            o_ref[...] = jnp.zeros_like(o_ref)

        a_bf = a_ref[...].astype(BF16)
        for bi, b_ref in enumerate(b_refs):
            part = lax.dot_general(a_bf, b_ref[...].astype(BF16), (((0,), (0,)), ((), ())), preferred_element_type=F32)
            for j in range(n_blk):
                o_ref[bi * n_blk + j] += part[:, j * c:(j + 1) * c]

    return _pallas(
        body, name=name, grid=(t // tk,),
        in_specs=[pl.BlockSpec((tk, m), lambda l: (l, 0))] + [pl.BlockSpec((tk, n), lambda l: (l, 0))] * nb,
        out_specs=pl.BlockSpec((nb * n_blk, m, c), lambda l: (0, 0, 0)),
        out_shape=jax.ShapeDtypeStruct((nb * n_blk, m, c), F32),
        compiler_params=_cparams("arbitrary"),
    )(a, *bs)


def _row_spec(arr, tm):
    return pl.BlockSpec((tm, arr.shape[1]), lambda i: (i, 0))


def _full_spec(arr):
    return pl.BlockSpec(arr.shape, lambda i: (0,) * arr.ndim)


def _ew_tile_rows(t, row_bytes):
    tm = 1024
    while tm > 8 and (t % tm or 2 * tm * row_bytes > EW_ROWS_BYTES):
        tm //= 2
    return tm


def _ew(fn, rows, vecs, out_dtypes, *, mats=(), n_red=0, name):
    t = rows[0].shape[0]
    nr, nv, nm = len(rows), len(vecs), len(mats)

    def tile_shapes(tm):
        return jax.eval_shape(fn, *[jax.ShapeDtypeStruct((tm, r.shape[1]), F32) for r in rows],
                              *[jax.ShapeDtypeStruct(v.shape, F32) for v in vecs],
                              *[jax.ShapeDtypeStruct(m_.shape, m_.dtype) for m_ in mats])

    probe = tile_shapes(8)
    row_bytes = sum(r.shape[1] * r.dtype.itemsize for r in rows)
    row_bytes += sum(s.shape[1] * jnp.dtype(dt).itemsize for s, dt in zip(probe, out_dtypes))
    tm = _ew_tile_rows(t, row_bytes)
    shapes = tile_shapes(tm)
    n_out = len(shapes) - n_red

    def body(*refs):
        ins = [r[...].astype(F32) for r in refs[:nr + nv]] + [r[...] for r in refs[nr + nv:nr + nv + nm]]
        outs = fn(*ins)
        o_refs = refs[nr + nv + nm:]
        for o_ref, o in zip(o_refs[:n_out], outs[:n_out]):
            o_ref[...] = o.astype(o_ref.dtype)
        i = pl.program_id(0)
        for o_ref, o in zip(o_refs[n_out:], outs[n_out:]):
            @pl.when(i == 0)
            def _(o_ref=o_ref, o=o):
                o_ref[...] = o

            @pl.when(i > 0)
            def _(o_ref=o_ref, o=o):
                o_ref[...] += o

    out_shape = [jax.ShapeDtypeStruct((t, s.shape[1]), dt) for s, dt in zip(shapes[:n_out], out_dtypes)]
    out_shape += [jax.ShapeDtypeStruct(s.shape, F32) for s in shapes[n_out:]]
    out_specs = [pl.BlockSpec((tm, s.shape[1]), lambda i: (i, 0)) for s in shapes[:n_out]]
    out_specs += [pl.BlockSpec(s.shape, lambda i: (0, 0)) for s in shapes[n_out:]]
    return _pallas(
        body, name=name, grid=(t // tm,),
        in_specs=[_row_spec(r, tm) for r in rows] + [_full_spec(v) for v in list(vecs) + list(mats)],
        out_specs=out_specs, out_shape=out_shape,
        compiler_params=_cparams("arbitrary" if n_red else "parallel"),
    )(*rows, *vecs, *mats)


def _dot(a, w):
    a = a.astype(BF16)
    if w.ndim == 3:
        return jnp.concatenate([jnp.dot(a, w[j], preferred_element_type=F32) for j in range(w.shape[0])], axis=1)
    return jnp.dot(a, w, preferred_element_type=F32)


def _dot_t(a, w):
    a = a.astype(BF16)
    nt = (((1,), (1,)), ((), ()))
    if w.ndim == 3:
        c = w.shape[2]
        return sum(lax.dot_general(a[:, j * c:(j + 1) * c], w[j], nt, preferred_element_type=F32) for j in range(w.shape[0]))
    return lax.dot_general(a, w, nt, preferred_element_type=F32)


def _ew_bwd(fn, rows, vecs, cts, row_wrt, vec_wrt, out_dtypes, *, name):
    nr, nv, nc = len(rows), len(vecs), len(cts)

    def grad_fn(*tiles):
        ins = list(tiles[:nr + nv])
        ct = tiles[nr + nv:]
        wrt = list(row_wrt) + [nr + j for j in vec_wrt]

        def f(*sel):
            full = list(ins)
            for idx, s in zip(wrt, sel):
                full[idx] = s
            return tuple(fn(*full))

        _, vjp = jax.vjp(f, *[ins[idx] for idx in wrt])
        return vjp(tuple(ct))

    return _ew_reordered(grad_fn, rows, vecs, cts, out_dtypes, len(vec_wrt), name)


def _ew_reordered(grad_fn, rows, vecs, cts, out_dtypes, n_red, name):
    nr, nv = len(rows), len(vecs)

    def fn(*tiles):
        r, c, v = tiles[:nr], tiles[nr:nr + len(cts)], tiles[nr + len(cts):]
        return grad_fn(*r, *v, *c)

    return _ew(fn, list(rows) + list(cts), vecs, out_dtypes, n_red=n_red, name=name)


def _rms(x, g):
    return x * lax.rsqrt(jnp.mean(x * x, axis=-1, keepdims=True) + EPS) * g


def _sigmoid(x):
    return 1.0 / (1.0 + jnp.exp(-x))


def _silu(x):
    return x * _sigmoid(x)


def _gelu(x):
    return 0.5 * x * (1.0 + jnp.tanh(math.sqrt(2.0 / math.pi) * (x + 0.044715 * x * x * x)))


def _token_rows(t):
    return pl.ds(pl.multiple_of(t * N_GBLK, N_GBLK), N_GBLK)


def _block_rows(gb, tt):
    return pl.ds(gb, tt, stride=N_GBLK)


def _s5_expand(u_bf, bbd_ref, xs_ref, tt):
    for gb in range(N_GBLK):
        xg = jnp.dot(u_bf[:, gb * LANES:(gb + 1) * LANES], bbd_ref[gb], preferred_element_type=F32)
        for l in range(N_SLAB):
            xs_ref[l, _block_rows(gb, tt), :] = xg[:, l * LANES:(l + 1) * LANES]


def _s5_forward_scan(xs_ref, ar, ai, state, tt):
    half = N_SLAB // 2

    def step(t, st):
        new = [None] * N_SLAB
        for l in range(half):
            sr, si = st[l], st[half + l]
            rows = _token_rows(t)
            nr_ = ar[l] * sr - ai[l] * si + xs_ref[l, rows, :]
            ni_ = ar[l] * si + ai[l] * sr + xs_ref[half + l, rows, :]
            xs_ref[l, rows, :] = nr_
            xs_ref[half + l, rows, :] = ni_
            new[l], new[half + l] = nr_, ni_
        return tuple(new)

    return lax.fori_loop(0, tt, step, tuple(state), unroll=4)


def _slab_rows(ref, gb, tt):
    return jnp.concatenate([ref[l, _block_rows(gb, tt), :] for l in range(N_SLAB)], axis=1)


def _hosted(plan, i, nt, refs, n_in, n_out, n_scratch):
    k_in = len(plan["ins"]) if plan else 0
    k_out = len(plan["out_shape"]) if plan else 0
    own_in, ex_in = refs[:n_in], refs[n_in:n_in + k_in]
    o0 = n_in + k_in
    own_out, ex_out = refs[o0:o0 + n_out], refs[o0 + n_out:o0 + n_out + k_out]
    s0 = o0 + n_out + k_out
    own_scr, ex_sems = refs[s0:s0 + n_scratch], refs[s0 + n_scratch:]

    def run(phase, step):
        if plan:
            @pl.when(i == step)
            def _():
                plan["phases"][phase](ex_in, ex_out, ex_sems)

    return own_in, own_out, own_scr, run


def _s5_fwd(u, bbd, ccd, a_re, a_im, *, tt=256, plan=None):
    t = u.shape[0]
    nt = t // tt
    half = N_SLAB // 2

    def body(*refs):
        i = pl.program_id(0)
        (u_ref, bbd_ref, ccd_ref, ar_ref, ai_ref), (y_ref, hb_ref), (xs_ref, st_ref), run = _hosted(plan, i, nt, refs, 5, 2, 2)
        run(0, 0)
        run(1, nt // 2)

        @pl.when(i == 0)
        def _():
            st_ref[...] = jnp.zeros_like(st_ref)

        hb_ref[0] = st_ref[...]
        _s5_expand(u_ref[...].astype(BF16), bbd_ref, xs_ref, tt)
        ar = [ar_ref[l] for l in range(half)]
        ai = [ai_ref[l] for l in range(half)]
        last = _s5_forward_scan(xs_ref, ar, ai, [st_ref[l] for l in range(N_SLAB)], tt)
        for l in range(N_SLAB):
            st_ref[l] = last[l]
        for gb in range(N_GBLK):
            sg = _slab_rows(xs_ref, gb, tt).astype(BF16)
            y_ref[:, gb * LANES:(gb + 1) * LANES] = jnp.dot(sg, ccd_ref[gb], preferred_element_type=F32)
        run(2, nt - 1)

    ex = plan or dict(ins=[], out_shape=[], scratch=[])
    res = _pallas(
        body, name="s5_fwd", grid=(nt,),
        in_specs=[pl.BlockSpec((tt, D_MODEL), lambda i: (i, 0)), _full_spec(bbd), _full_spec(ccd),
                  _full_spec(a_re), _full_spec(a_im)] + [ANY_SPEC] * len(ex["ins"]),
        out_specs=[pl.BlockSpec((tt, D_MODEL), lambda i: (i, 0)),
                   pl.BlockSpec((1, N_SLAB, N_GBLK, LANES), lambda i: (i, 0, 0, 0))] + [ANY_SPEC] * len(ex["out_shape"]),
        out_shape=[jax.ShapeDtypeStruct((t, D_MODEL), F32), jax.ShapeDtypeStruct((nt, N_SLAB, N_GBLK, LANES), F32)]
        + ex["out_shape"],
        scratch_shapes=[pltpu.VMEM((N_SLAB, N_GBLK * tt, LANES), F32), pltpu.VMEM((N_SLAB, N_GBLK, LANES), F32)]
        + ex["scratch"],
        compiler_params=_cparams("arbitrary"),
    )(u, bbd, ccd, a_re, a_im, *ex["ins"])
    return res[0], res[1], res[2:]


def _s5_bwd(u, dy, du_add, hb, bbd, ccd, a_re, a_im, *, tt=256, plan=None):
    t = u.shape[0]
    nt = t // tt
    half = N_SLAB // 2
    rev = lambda i: (nt - 1 - i, 0)

    def body(*refs):
        i = pl.program_id(0)
        own_in, own_out, own_scr, run = _hosted(plan, i, nt, refs, 8, 5, 3)
        u_ref, dy_ref, dua_ref, hb_ref, bbd_ref, ccd_ref, ar_ref, ai_ref = own_in
        du_ref, dbbd_ref, dccd_ref, dar_ref, dai_ref = own_out
        xs_ref, es_ref, lam_ref = own_scr
        run(0, 0)
        run(1, nt // 2)

        @pl.when(i == 0)
        def _():
            lam_ref[...] = jnp.zeros_like(lam_ref)
            dbbd_ref[...] = jnp.zeros_like(dbbd_ref)
            dccd_ref[...] = jnp.zeros_like(dccd_ref)
            dar_ref[...] = jnp.zeros_like(dar_ref)
            dai_ref[...] = jnp.zeros_like(dai_ref)

        u_bf = u_ref[...].astype(BF16)
        dy_bf = dy_ref[...].astype(BF16)
        _s5_expand(u_bf, bbd_ref, xs_ref, tt)
        ar = [ar_ref[l] for l in range(half)]
        ai = [ai_ref[l] for l in range(half)]
        start = [hb_ref[0, l] for l in range(N_SLAB)]
        _s5_forward_scan(xs_ref, ar, ai, start, tt)
        for gb in range(N_GBLK):
            eg = lax.dot_general(dy_bf[:, gb * LANES:(gb + 1) * LANES], ccd_ref[gb], (((1,), (1,)), ((), ())),
                                 preferred_element_type=F32)
            for l in range(N_SLAB):
                es_ref[l, _block_rows(gb, tt), :] = eg[:, l * LANES:(l + 1) * LANES]

        def lam_step(k, lam):
            rows = _token_rows(tt - 1 - k)
            new_lam = [None] * N_SLAB
            for l in range(half):
                lr = es_ref[l, rows, :] + ar[l] * lam[l] + ai[l] * lam[half + l]
                li = es_ref[half + l, rows, :] - ai[l] * lam[l] + ar[l] * lam[half + l]
                es_ref[l, rows, :] = lr
                es_ref[half + l, rows, :] = li
                new_lam[l], new_lam[half + l] = lr, li
            return tuple(new_lam)

        lam = lax.fori_loop(0, tt, lam_step, tuple(lam_ref[l] for l in range(N_SLAB)), unroll=4)
        for l in range(N_SLAB):
            lam_ref[l] = lam[l]

        def over_tokens(prod):
            return jnp.sum(prod.reshape(tt - 1, N_GBLK, LANES), axis=0)

        later_rows, earlier_rows = pl.ds(N_GBLK, (tt - 1) * N_GBLK), pl.ds(0, (tt - 1) * N_GBLK)
        first = pl.ds(0, N_GBLK)
        for l in range(half):
            lr, li = es_ref[l, later_rows, :], es_ref[half + l, later_rows, :]
            pr, pi = xs_ref[l, earlier_rows, :], xs_ref[half + l, earlier_rows, :]
            lr0, li0 = es_ref[l, first, :], es_ref[half + l, first, :]
            dar_ref[l] += over_tokens(lr * pr + li * pi) + lr0 * start[l] + li0 * start[half + l]
            dai_ref[l] += over_tokens(li * pr - lr * pi) + li0 * start[l] - lr0 * start[half + l]

        for gb in range(N_GBLK):
            cols = slice(gb * LANES, (gb + 1) * LANES)
            lg = _slab_rows(es_ref, gb, tt).astype(BF16)
            sg = _slab_rows(xs_ref, gb, tt).astype(BF16)
            du_ref[:, cols] = (dua_ref[:, cols] + lax.dot_general(
                lg, bbd_ref[gb], (((1,), (1,)), ((), ())), preferred_element_type=F32)).astype(du_ref.dtype)
            dbbd_ref[gb] += lax.dot_general(u_bf[:, cols], lg, (((0,), (0,)), ((), ())), preferred_element_type=F32)
            dccd_ref[gb] += lax.dot_general(sg, dy_bf[:, cols], (((0,), (0,)), ((), ())), preferred_element_type=F32)
        run(2, nt - 1)

    row = pl.BlockSpec((tt, D_MODEL), rev)
    small = pl.BlockSpec((half, N_GBLK, LANES), lambda i: (0, 0, 0))
    ex = plan or dict(ins=[], out_shape=[], scratch=[])
    res = _pallas(
        body, name="s5_bwd", grid=(nt,),
        in_specs=[row, row, row, pl.BlockSpec((1, N_SLAB, N_GBLK, LANES), lambda i: (nt - 1 - i, 0, 0, 0)),
                  _full_spec(bbd), _full_spec(ccd), _full_spec(a_re), _full_spec(a_im)] + [ANY_SPEC] * len(ex["ins"]),
        out_specs=[row, _full_spec(bbd), _full_spec(ccd), small, small] + [ANY_SPEC] * len(ex["out_shape"]),
        out_shape=[jax.ShapeDtypeStruct((t, D_MODEL), BF16), jax.ShapeDtypeStruct(bbd.shape, F32),
                   jax.ShapeDtypeStruct(ccd.shape, F32), jax.ShapeDtypeStruct(a_re.shape, F32),
                   jax.ShapeDtypeStruct(a_im.shape, F32)] + ex["out_shape"],
        scratch_shapes=[pltpu.VMEM((N_SLAB, N_GBLK * tt, LANES), F32), pltpu.VMEM((N_SLAB, N_GBLK * tt, LANES), F32),
                        pltpu.VMEM((N_SLAB, N_GBLK, LANES), F32)] + ex["scratch"],
        compiler_params=_cparams("arbitrary"),
    )(u, dy, du_add, hb, bbd, ccd, a_re, a_im, *ex["ins"])
    return (*res[:5], res[5:])


def _s5_discretize(lam_re, lam_im, log_dt, b_re, b_im, c_re, c_im):
    lr = jnp.minimum(lam_re, -1e-4)
    li = lam_im
    dt = jnp.exp(log_dt)[:, None]
    mag = jnp.exp(lr * dt)
    a_re = mag * jnp.cos(li * dt)
    a_im = mag * jnp.sin(li * dt)
    den = lr * lr + li * li
    nr = a_re - 1.0
    f_re = (nr * lr + a_im * li) / den
    f_im = (a_im * lr - nr * li) / den
    bb_re = f_re[..., None] * b_re - f_im[..., None] * b_im
    bb_im = f_re[..., None] * b_im + f_im[..., None] * b_re
    eye = jnp.eye(GROUPS_PER_BLOCK, dtype=F32)
    bb = jnp.stack([bb_re, bb_im]).reshape(2, N_GBLK, GROUPS_PER_BLOCK, STATE, GROUP_SIZE)
    bbd = jnp.einsum('rgjph,jk->gjhrkp', bb, eye).reshape(N_GBLK, LANES, STATE_COLS)
    cc = jnp.stack([c_re, -c_im]).reshape(2, N_GBLK, GROUPS_PER_BLOCK, GROUP_SIZE, STATE)
    ccd = jnp.einsum('rgjhp,jk->grjpkh', cc, eye).reshape(N_GBLK, STATE_COLS, LANES)

    def dense(a):
        return a.reshape(N_GBLK, N_SLAB // 2, LANES).transpose(1, 0, 2)

    return dense(a_re), dense(a_im), bbd, ccd


def _attn_tiles(tq):
    row = lax.broadcasted_iota(jnp.int32, (tq, tq), 0)
    col = lax.broadcasted_iota(jnp.int32, (tq, tq), 1)
    return col < row, (row > col).astype(BF16), (row >= col).astype(BF16)


def _split_bf16(x):
    hi = x.astype(BF16)
    return hi, (x - hi.astype(F32)).astype(BF16)


def _attn_scores(qs, kj, cs, diag, later, masked):
    zs = [lax.dot_general(qa, kj, (((1,), (1,)), ((), ())), preferred_element_type=F32) * (HEAD_DIM ** -0.5) for qa in qs]
    lks = [-(jnp.maximum(z, 0.0) + jnp.log(1.0 + jnp.exp(-jnp.abs(z)))) for z in zs]
    if masked:
        lks = [jnp.where(diag, lk, 0.0) for lk in lks]
    parts = [_split_bf16(lk) for lk in lks]
    tails = [jnp.dot(hi, later, preferred_element_type=F32) + jnp.dot(lo, later, preferred_element_type=F32)
             for hi, lo in parts]
    ws = [jnp.exp(z + lk + tail + c) for z, lk, tail, c in zip(zs, lks, tails, cs)]
    if masked:
        ws = [jnp.where(diag, w, 0.0) for w in ws]
    return zs, lks, ws


def _head_masks(width=LANES):
    lane = lax.broadcasted_iota(jnp.int32, (1, width), 1)
    return [(lane >= a * HEAD_DIM) & (lane < (a + 1) * HEAD_DIM) for a in range(width // HEAD_DIM)]


def _any_alive(cs):
    alive = jnp.max(cs[0]) > EXP_UNDERFLOW
    for c in cs[1:]:
        alive = alive | (jnp.max(c) > EXP_UNDERFLOW)
    return alive


def _attn_fwd(q, k, v, *, tq=256):
    t = q.shape[0]
    tq = min(tq, t)
    width = ATTN_LANES

    def body(q_ref, k_ref, v_ref, o_ref, ol_ref):
        i = pl.program_id(1)
        diag, later, _ = _attn_tiles(tq)
        masks = _head_masks(width)
        q2 = q_ref[...]
        qs = [jnp.where(hm, q2, jnp.zeros_like(q2)) for hm in masks]

        def block(j, state, masked):
            rows = pl.ds(pl.multiple_of(j * tq, tq), tq)
            kj, vj = k_ref[rows, :], v_ref[rows, :]
            _, lks, ws = _attn_scores(qs, kj, [h[0] for h in state], diag, later, masked)
            parts = [_split_bf16(w) for w in ws]
            return tuple((c + jnp.sum(lk, axis=1, keepdims=True),
                          acc + jnp.dot(w_hi, vj, preferred_element_type=F32),
                          low + jnp.dot(w_lo, vj, preferred_element_type=F32))
                         for (c, acc, low), lk, (w_hi, w_lo) in zip(state, lks, parts))

        zero = jnp.zeros((tq, width), F32)
        state = block(i, tuple((jnp.zeros((tq, 1), F32), zero, zero) for _ in masks), True)

        def cond(st):
            return (st[0] >= 0) & _any_alive([h[0] for h in st[1]])

        def step(st):
            return st[0] - 1, block(st[0], st[1], False)

        _, state = lax.while_loop(cond, step, (i - 1, state))
        o_ref[...] = sum(jnp.where(hm, h[1], 0.0) for hm, h in zip(masks, state))
        ol_ref[...] = sum(jnp.where(hm, h[2], 0.0) for hm, h in zip(masks, state))

    qspec = pl.BlockSpec((tq, width), lambda h, i: (i, h))
    kspec = pl.BlockSpec((t, width), lambda h, i: (0, h))
    full = jax.ShapeDtypeStruct((t, D_MODEL), F32)
    return _pallas(
        body, name="attn_fwd", grid=(D_MODEL // width, t // tq), in_specs=[qspec, kspec, kspec], out_specs=[qspec, qspec],
        out_shape=[full, full], compiler_params=_cparams("parallel", "arbitrary"),
    )(q, k, v)


def _attn_bwd(q, k, v, o, o_low, do, *, tq=256):
    t = q.shape[0]
    tq = min(tq, t)
    width = ATTN_LANES

    def body(q_ref, k_ref, v_ref, o_ref, ol_ref, do_ref, dq_ref, dk_ref, dv_ref):
        i = pl.program_id(1)

        @pl.when(i == 0)
        def _():
            dk_ref[...] = jnp.zeros_like(dk_ref)
            dv_ref[...] = jnp.zeros_like(dv_ref)

        diag, later, later_eq = _attn_tiles(tq)
        masks = _head_masks(width)
        q2 = q_ref[...]
        do2 = do_ref[...]
        dd = do2.astype(BF16).astype(F32) * (o_ref[...] + ol_ref[...])
        qs = [jnp.where(hm, q2, jnp.zeros_like(q2)) for hm in masks]
        dos = [jnp.where(hm, do2, 0.0).astype(BF16) for hm in masks]
        totals = [jnp.sum(jnp.where(hm, dd, 0.0), axis=1, keepdims=True) for hm in masks]

        def block(j, state, masked):
            rows = pl.ds(pl.multiple_of(j * tq, tq), tq)
            kj, vj = k_ref[rows, :], v_ref[rows, :]
            nt = (((1,), (1,)), ((), ()))
            tn = (((0,), (0,)), ((), ()))
            zs, lks, ws = _attn_scores(qs, kj, [h[0] for h in state], diag, later, masked)
            dws = [lax.dot_general(doa, vj, nt, preferred_element_type=F32) for doa in dos]
            gs = [dw * w for dw, w in zip(dws, ws)]
            parts = [_split_bf16(g) for g in gs]
            from_here = [jnp.dot(hi, later_eq, preferred_element_type=F32) + jnp.dot(lo, later_eq, preferred_element_type=F32)
                         for hi, lo in parts]
            dzs = []
            for z, lk, g, fh, total, h in zip(zs, lks, gs, from_here, totals, state):
                beta = jnp.exp(z + lk)
                dz = g * (1.0 - beta) - beta * (total - h[1] - fh)
                if masked:
                    dz = jnp.where(diag, dz, 0.0)
                dzs.append((dz * (HEAD_DIM ** -0.5)).astype(BF16))
            dk = sum(lax.dot_general(dz, qa, tn, preferred_element_type=F32) for dz, qa in zip(dzs, qs))
            dv = sum(lax.dot_general(w.astype(BF16), doa, tn, preferred_element_type=F32) for w, doa in zip(ws, dos))
            new = [(c + jnp.sum(lk, axis=1, keepdims=True), r + jnp.sum(g, axis=1, keepdims=True),
                    acc + jnp.dot(dz, kj, preferred_element_type=F32))
                   for (c, r, acc), lk, g, dz in zip(state, lks, gs, dzs)]
            dk_ref[rows, :] += dk
            dv_ref[rows, :] += dv
            return tuple(new)

        zc = jnp.zeros((tq, 1), F32)
        state = block(i, tuple((zc, zc, jnp.zeros((tq, width), F32)) for _ in masks), True)

        def cond(st):
            return (st[0] >= 0) & _any_alive([h[0] for h in st[1]])

        def step(st):
            return st[0] - 1, block(st[0], st[1], False)

        _, state = lax.while_loop(cond, step, (i - 1, state))
        dq_ref[...] = sum(jnp.where(hm, h[2], 0.0) for hm, h in zip(masks, state))

    qspec = pl.BlockSpec((tq, width), lambda h, i: (i, h))
    kspec = pl.BlockSpec((t, width), lambda h, i: (0, h))
    kin = pl.BlockSpec((t, width), lambda h, i: (0, h), pipeline_mode=pl.Buffered(1))
    full = jax.ShapeDtypeStruct((t, D_MODEL), F32)
    return _pallas(
        body, name="attn_bwd", grid=(D_MODEL // width, t // tq), in_specs=[qspec, kin, kin, qspec, qspec, qspec],
        out_specs=[qspec, kspec, kspec], out_shape=[full, full, full],
        compiler_params=_cparams("parallel", "arbitrary"),
    )(q, k, v, o, o_low, do)


N_CHIP = 4
MESH_ID = pl.DeviceIdType.MESH
ANY_SPEC = pl.BlockSpec(memory_space=pl.ANY)


def _position():
    return lax.axis_index("x"), lax.axis_index("y"), lax.axis_index("c")


def _other_chips(x, y):
    return [(1 - x, y), (x, 1 - y), (1 - x, 1 - y)]


def _gather_plan(arrs):
    n = len(arrs)

    def parts(ins, outs, sems):
        send_sems, recv_sems, local_sems = sems
        x, y, c = _position()
        me, sibling = (x, y, c), (x, y, 1 - c)
        chips = _other_chips(x, y)

        def copy(a, k, block, to, src=None):
            slot = outs[a].at[4 * block[0] + 2 * block[1] + block[2]]
            return pltpu.make_async_remote_copy(
                src_ref=slot if src is None else src, dst_ref=slot, send_sem=send_sems.at[a, k],
                recv_sem=recv_sems.at[a, k], device_id=to, device_id_type=MESH_ID)

        local = [pltpu.make_async_copy(ins[a], outs[a].at[4 * x + 2 * y + c], local_sems.at[a]) for a in range(n)]
        first = []
        for a in range(n):
            first.append(copy(a, 0, me, sibling, src=ins[a]))
            first += [copy(a, 1 + j, me, (*chip, c), src=ins[a]) for j, chip in enumerate(chips)]
        passed = [copy(a, 4 + j, (*chip, c), sibling) for a in range(n) for j, chip in enumerate(chips)]
        return me, sibling, chips, c, copy, local, first, passed

    def start(ins, outs, sems):
        *_, local, first, _ = parts(ins, outs, sems)
        for cp in local + first:
            cp.start()

    def forward(ins, outs, sems):
        me, _, chips, c, copy, _, _, passed = parts(ins, outs, sems)
        for a in range(n):
            for j, chip in enumerate(chips):
                copy(a, 1 + j, (*chip, c), me).wait_recv()
                passed[a * len(chips) + j].start()

    def finish(ins, outs, sems):
        me, sibling, chips, c, copy, local, first, passed = parts(ins, outs, sems)
        for a in range(n):
            copy(a, 0, sibling, me).wait_recv()
            for j, chip in enumerate(chips):
                copy(a, 4 + j, (*chip, 1 - c), me).wait_recv()
        for cp in first + passed:
            cp.wait_send()
        for cp in local:
            cp.wait()

    return dict(ins=list(arrs), out_shape=[jax.ShapeDtypeStruct((N_DEV,) + a.shape, a.dtype) for a in arrs],
                scratch=[pltpu.SemaphoreType.DMA((n, N_DEV - 1)), pltpu.SemaphoreType.DMA((n, N_DEV - 1)),
                         pltpu.SemaphoreType.DMA((n,))],
                phases=(start, forward, finish))


def _run_plan(plan, *, name):
    n_in, n_out = len(plan["ins"]), len(plan["out_shape"])

    def body(*refs):
        ins, outs, sems = refs[:n_in], refs[n_in:n_in + n_out], refs[n_in + n_out:]
        for phase in plan["phases"]:
            phase(ins, outs, sems)

    return _pallas(body, name=name, in_specs=[ANY_SPEC] * n_in, out_specs=[ANY_SPEC] * n_out,
                   out_shape=plan["out_shape"], scratch_shapes=plan["scratch"])(*plan["ins"])


def _gather(arrs, *, name):
    return _run_plan(_gather_plan(arrs), name=name)


def _pair_exchange(arrs, *, name):
    n = len(arrs)

    def body(*refs):
        ins, got = refs[:n], refs[n:2 * n]
        send_sems, recv_sems = refs[2 * n:]
        x, y, c = _position()
        sends = []
        for a in range(n):
            for ch in range(N_CHIP):
                sends.append(pltpu.make_async_remote_copy(
                    src_ref=ins[a].at[2 * ch + (1 - c)], dst_ref=got[a].at[ch], send_sem=send_sems.at[a, ch],
                    recv_sem=recv_sems.at[a, ch], device_id=(x, y, 1 - c), device_id_type=MESH_ID))
        for cp in sends:
            cp.start()
        for cp in sends:
            cp.wait_send()
            cp.wait_recv()

    return _pallas(
        body, name=name, in_specs=[ANY_SPEC] * n, out_specs=[ANY_SPEC] * n,
        out_shape=[jax.ShapeDtypeStruct((N_CHIP,) + a.shape[1:], a.dtype) for a in arrs],
        scratch_shapes=[pltpu.SemaphoreType.DMA((n, N_CHIP)), pltpu.SemaphoreType.DMA((n, N_CHIP))],
    )(*arrs)


def _chip_exchange_plan(arrs):
    n = len(arrs)

    def parts(ins, outs, sems):
        send_sems, recv_sems, local_sems = sems
        x, y, c = _position()
        mine = 2 * x + y
        sends, recvs, locals_ = [], [], []
        for a in range(n):
            locals_.append(pltpu.make_async_copy(ins[a].at[mine], outs[a].at[mine], local_sems.at[a]))
            for j, (px, py) in enumerate(_other_chips(x, y)):
                sends.append(pltpu.make_async_remote_copy(
                    src_ref=ins[a].at[2 * px + py], dst_ref=outs[a].at[mine], send_sem=send_sems.at[a, j],
                    recv_sem=recv_sems.at[a, j], device_id=(px, py, c), device_id_type=MESH_ID))
                recvs.append(pltpu.make_async_remote_copy(
                    src_ref=ins[a].at[2 * px + py], dst_ref=outs[a].at[2 * px + py], send_sem=send_sems.at[a, j],
                    recv_sem=recv_sems.at[a, j], device_id=(px, py, c), device_id_type=MESH_ID))
        return sends, recvs, locals_

    def start(ins, outs, sems):
        sends, _, locals_ = parts(ins, outs, sems)
        for cp in locals_ + sends:
            cp.start()

    def nothing(ins, outs, sems):
        pass

    def finish(ins, outs, sems):
        sends, recvs, locals_ = parts(ins, outs, sems)
        for cp in sends:
            cp.wait_send()
        for cp in recvs:
            cp.wait_recv()
        for cp in locals_:
            cp.wait()

    return dict(ins=list(arrs), out_shape=[jax.ShapeDtypeStruct(a.shape, a.dtype) for a in arrs],
                scratch=[pltpu.SemaphoreType.DMA((n, N_CHIP - 1)), pltpu.SemaphoreType.DMA((n, N_CHIP - 1)),
                         pltpu.SemaphoreType.DMA((n,))],
                phases=(start, nothing, finish))


def _pair_sum(mine, got, core, *, name):
    s_, r, c = got.shape
    tr = r if r <= 256 else 256

    def body(core_ref, a_ref, b_ref, o_ref):
        o_ref[...] = a_ref[...] + b_ref[...]

    quarter = pl.BlockSpec((s_, tr, c), lambda i, core_ref: (0, i, 0))
    return _pallas(
        body, name=name, out_shape=jax.ShapeDtypeStruct(got.shape, got.dtype),
        grid_spec=pltpu.PrefetchScalarGridSpec(
            num_scalar_prefetch=1, grid=(r // tr,),
            in_specs=[pl.BlockSpec((s_, None, tr, c), lambda i, core_ref: (0, core_ref[0], i, 0)), quarter],
            out_specs=quarter),
        compiler_params=_cparams("parallel"),
    )(core, mine.reshape(s_, 2, r, c), got)


def _pair_sums(arrs, names, *, name):
    got = _pair_exchange(arrs, name=name)
    core = lax.axis_index("c").astype(jnp.int32).reshape(1)
    return [_pair_sum(a, g_, core, name="pair_sum_" + n) for a, g_, n in zip(arrs, got, names)]


def _adamw(parts, w, m, v, *, name):
    s, r, c = parts.shape
    tr = r if r * c <= 256 * D_MODEL else 256

    def body(p_ref, w_ref, m_ref, v_ref, g_ref, d_ref, nm_ref, nv_ref):
        g = p_ref[0]
        for j in range(1, s):
            g = g + p_ref[j]
        nm = ADAM_B1 * m_ref[...] + (1.0 - ADAM_B1) * g
        nv = ADAM_B2 * v_ref[...] + (1.0 - ADAM_B2) * (g * g)
        m_hat = nm / (1.0 - ADAM_B1 ** ADAM_STEP)
        v_hat = nv / (1.0 - ADAM_B2 ** ADAM_STEP)
        g_ref[...] = g
        d_ref[...] = -ADAM_LR * (m_hat / (jnp.sqrt(v_hat) + ADAM_EPS) + ADAM_WD * w_ref[...])
        nm_ref[...] = nm
        nv_ref[...] = nv

    spec = pl.BlockSpec((tr, c), lambda i: (i, 0))
    one = jax.ShapeDtypeStruct((r, c), F32)
    return _pallas(
        body, name=name, grid=(r // tr,), in_specs=[pl.BlockSpec((s, tr, c), lambda i: (0, i, 0)), spec, spec, spec],
        out_specs=[spec] * 4, out_shape=[one] * 4, compiler_params=_cparams("parallel"),
    )(parts, w, m, v)


def _cols_to_blocks(g, n_blk):
    r = g.shape[0]
    return g.reshape(r, n_blk, g.shape[1] // n_blk).transpose(1, 0, 2)


def _blocks_to_cols(gathered):
    n_blk, r, c = gathered.shape
    return gathered.transpose(1, 0, 2).reshape(r, n_blk * c)


def kernel(x, p, a_norm_pre, a_norm_post, a_w_in, a_lam_re, a_lam_im, a_log_dt, a_b_re, a_b_im, a_c_re, a_c_im, a_d_skip, a_w_glu, a_b_glu, a_w_out, kv_norm, w_kv, b_norm_pre, b_norm_post, b_w_in, b_w_out, ple_w_proj, ple_w_gate, loss_target, m_a_norm_pre, m_a_norm_post, m_a_w_in, m_a_lam_re, m_a_lam_im, m_a_log_dt, m_a_b_re, m_a_b_im, m_a_c_re, m_a_c_im, m_a_d_skip, m_a_w_glu, m_a_b_glu, m_a_w_out, m_kv_norm, m_w_kv, m_b_norm_pre, m_b_norm_post, m_b_w_in, m_b_w_out, m_ple_w_proj, m_ple_w_gate, v_a_norm_pre, v_a_norm_post, v_a_w_in, v_a_lam_re, v_a_lam_im, v_a_log_dt, v_a_b_re, v_a_b_im, v_a_c_re, v_a_c_im, v_a_d_skip, v_a_w_glu, v_a_b_glu, v_a_w_out, v_kv_norm, v_w_kv, v_b_norm_pre, v_b_norm_post, v_b_w_in, v_b_w_out, v_ple_w_proj, v_ple_w_gate):
    weights = dict(a_norm_pre=a_norm_pre, a_norm_post=a_norm_post, a_w_in=a_w_in, a_lam_re=a_lam_re, a_lam_im=a_lam_im, a_log_dt=a_log_dt, a_b_re=a_b_re, a_b_im=a_b_im, a_c_re=a_c_re, a_c_im=a_c_im, a_d_skip=a_d_skip, a_w_glu=a_w_glu, a_b_glu=a_b_glu, a_w_out=a_w_out, kv_norm=kv_norm, w_kv=w_kv, b_norm_pre=b_norm_pre, b_norm_post=b_norm_post, b_w_in=b_w_in, b_w_out=b_w_out, ple_w_proj=ple_w_proj, ple_w_gate=ple_w_gate)
    mom_m = dict(a_norm_pre=m_a_norm_pre, a_norm_post=m_a_norm_post, a_w_in=m_a_w_in, a_lam_re=m_a_lam_re, a_lam_im=m_a_lam_im, a_log_dt=m_a_log_dt, a_b_re=m_a_b_re, a_b_im=m_a_b_im, a_c_re=m_a_c_re, a_c_im=m_a_c_im, a_d_skip=m_a_d_skip, a_w_glu=m_a_w_glu, a_b_glu=m_a_b_glu, a_w_out=m_a_w_out, kv_norm=m_kv_norm, w_kv=m_w_kv, b_norm_pre=m_b_norm_pre, b_norm_post=m_b_norm_post, b_w_in=m_b_w_in, b_w_out=m_b_w_out, ple_w_proj=m_ple_w_proj, ple_w_gate=m_ple_w_gate)
    mom_v = dict(a_norm_pre=v_a_norm_pre, a_norm_post=v_a_norm_post, a_w_in=v_a_w_in, a_lam_re=v_a_lam_re, a_lam_im=v_a_lam_im, a_log_dt=v_a_log_dt, a_b_re=v_a_b_re, a_b_im=v_a_b_im, a_c_re=v_a_c_re, a_c_im=v_a_c_im, a_d_skip=v_a_d_skip, a_w_glu=v_a_w_glu, a_b_glu=v_a_b_glu, a_w_out=v_a_w_out, kv_norm=v_kv_norm, w_kv=v_w_kv, b_norm_pre=v_b_norm_pre, b_norm_post=v_b_norm_post, b_w_in=v_b_w_in, b_w_out=v_b_w_out, ple_w_proj=v_ple_w_proj, ple_w_gate=v_ple_w_gate)
    names = list(weights)

    t = x.shape[1]
    x0 = x[0]
    p0, p1 = p[0, 0], p[1, 0]
    tgt = loss_target[0]

    mats = dict(a_w_in=a_w_in[0], a_w_glu=a_w_glu[0], a_w_out=a_w_out[0], w_kv=w_kv, b_w_in=b_w_in[0],
                b_w_out=b_w_out[0], ple_w_proj=ple_w_proj.reshape(2 * 256, LANES),
                ple_w_gate=ple_w_gate.reshape(2 * LANES, D_MODEL))
    vec_names = ["a_norm_pre", "a_norm_post", "a_d_skip", "a_b_glu"]
    vec_pack = jnp.concatenate([weights[n] for n in vec_names], axis=0)
    early = ["a_w_in"]
    late = [n for n in mats if n not in early]
    gathered = _gather([mats[n].astype(BF16) for n in early] + [vec_pack], name="gather_weights")
    vec_full = gathered[-1].transpose(1, 0, 2).reshape(len(vec_names), D_MODEL)
    g_apre, g_apost, d_skip, b_glu = (vec_full[i:i + 1] for i in range(4))
    w_ain = gathered[0]
    g_kv, g_bpre, g_bpost = kv_norm.reshape(1, D_MODEL), b_norm_pre, b_norm_post

    s5_params = (a_lam_re[0], a_lam_im[0], a_log_dt[0], a_b_re[0], a_b_im[0], a_c_re[0], a_c_im[0])
    (a_re, a_im, bbd, ccd), s5_vjp = jax.vjp(_s5_discretize, *s5_params)
    bbd_bf, ccd_bf = bbd.astype(BF16), ccd.astype(BF16)

    f_a2 = lambda ys, u, d: (_gelu(ys + d * u),)
    f_a3 = lambda g, s, ga, b: (g * _sigmoid(s + b) * _silu(ga),)
    f_rms = lambda y, g: (_rms(y, g),)
    f_k = lambda x2, gk, gp: (_rms(x2, gk), _rms(x2, gp))
    f_b3 = lambda o, gb: (o * _silu(gb),)

    def f_a_in(x0_, g_, w):
        h = _rms(x0_, g_).astype(BF16)
        return h, _dot(h, w[:4]), _dot(h, w[4:])

    h0, u, ga = _ew(f_a_in, [x0], [g_apre], [BF16, F32, F32], mats=[w_ain], name="a_norm_proj")
    ys, hb, late_w = _s5_fwd(u, bbd_bf, ccd_bf, a_re, a_im, plan=_gather_plan([mats[n].astype(BF16) for n in late]))
    gw = dict(zip(late, late_w))
    w_kvb, w_bin = gw["w_kv"], gw["b_w_in"]
    w_glu = gw["a_w_glu"].reshape(D_MODEL, D_MODEL)
    w_out = gw["a_w_out"].reshape(D_MODEL, D_MODEL)
    w_bout = gw["b_w_out"].reshape(D_MODEL, D_MODEL)
    wp = gw["ple_w_proj"].reshape(N_DEV, 2, 256, LANES)
    w_p0, w_p1 = _blocks_to_cols(wp[:, 0]), _blocks_to_cols(wp[:, 1])
    wg = gw["ple_w_gate"].reshape(N_DEV, 2, LANES, D_MODEL)
    w_g0, w_g1 = wg[:, 0].reshape(D_MODEL, D_MODEL), wg[:, 1].reshape(D_MODEL, D_MODEL)

    def f_a_glu(ys_, u_, d_, w):
        g = f_a2(ys_, u_, d_)[0].astype(BF16)
        return g, _dot(g, w)

    g_act, s_glu = _ew(f_a_glu, [ys, u], [d_skip], [BF16, F32], mats=[w_glu], name="a_gelu_glu")
    def f_a_out(ys_, u_, s_, ga_, x0_, d_, b_, g_, w):
        (y3_,) = f_a3(_gelu(ys_ + d_ * u_), s_, ga_, b_)
        y3_ = y3_.astype(BF16)
        y4_ = _dot(y3_, w)
        return y3_, y4_, x0_ + _rms(y4_, g_)

    y3, y4, x1 = _ew(f_a_out, [ys, u, s_glu, ga, x0], [d_skip, b_glu, g_apost], [BF16, F32, F32], mats=[w_out],
                     name="a_gate_out_norm")

    def f_ple(xin, p_, wg_, wp_):
        pg, pp = _dot(xin, wg_), _dot(p_, wp_)
        return xin + _sigmoid(pg) * pp, pg, pp

    x2, pg0, pp0 = _ew(f_ple, [x1, p0], [], [F32, F32, F32], mats=[w_g0, w_p0], name="ple0")
    def f_b_in(x2_, gk, gp, wkv_, win_):
        kvn_, h1_ = (t_.astype(BF16) for t_ in f_k(x2_, gk, gp))
        return kvn_, h1_, _dot(kvn_, wkv_[:4]), _dot(kvn_, wkv_[4:]), _dot(h1_, win_[:4]), _dot(h1_, win_[4:])

    kvn, h1, k_, v_, q_, gb = _ew(f_b_in, [x2], [g_kv, g_bpre], [BF16, BF16, BF16, BF16, BF16, F32],
                                  mats=[w_kvb, w_bin], name="b_norms_proj")
    o, o_low = _attn_fwd(q_, k_, v_)

    def f_b_out(o_, gb_, x2_, g_, w):
        y5in_ = f_b3(o_, gb_)[0].astype(BF16)
        y5_ = _dot(y5in_, w)
        return y5in_, y5_, x2_ + _rms(y5_, g_)

    y5in, y5, x3 = _ew(f_b_out, [o, gb, x2], [g_bpost], [BF16, F32, F32], mats=[w_bout], name="b_gate_out_norm")

    def f_ple_loss(xin, p_, tg, wg_, wp_):
        pg, pp = _dot(xin, wg_), _dot(p_, wp_)
        err = xin + _sigmoid(pg) * pp - tg
        return err * (1.0 / D_MODEL), pg, pp, (0.5 / D_MODEL) * jnp.sum(err * err).reshape(1, 1)

    dx4, pg1, pp1, loss_part = _ew(f_ple_loss, [x3, p1, tgt], [], [F32, F32, F32], mats=[w_g1, w_p1], n_red=1,
                                   name="ple1_loss")
    loss = lax.psum(loss_part[0, 0], ("x", "y", "c"))

    grads = {}
    def f_ple_bwd(dx, pg, pp, wg_):
        sg = _sigmoid(pg)
        dpg = (dx * pp * sg * (1.0 - sg)).astype(BF16)
        return dx + _dot_t(dpg, wg_), dpg, dx * sg

    def f_norm_bwd_mm(y, dx, g_, w):
        _, vjp = jax.vjp(_rms, y, g_)
        dy, dg = vjp(dx)
        dy = dy.astype(BF16)
        return _dot_t(dy, w), dy, dg

    dx3, dpg1, dpp1 = _ew(f_ple_bwd, [dx4, pg1, pp1], [], [F32, BF16, BF16], mats=[w_g1], name="ple1_bwd")
    d_wg1 = _mm(x3, dpg1, ta=True, name="ple1_gate_dw")
    d_wp1 = _mm(p1, dpp1, ta=True, name="ple1_proj_dw")
    def f_b_out_bwd(y, dx, o_, gb_, g_, w):
        dyin, dy, dg = f_norm_bwd_mm(y, dx, g_, w)
        _, vjp = jax.vjp(lambda a, b: f_b3(a, b)[0], o_, gb_)
        do_, dgb_ = vjp(dyin)
        return dy, do_, dgb_, dg

    dy5, do, dgb, d_gbpost = _ew(f_b_out_bwd, [y5, dx3, o, gb], [g_bpost], [BF16, F32, BF16], mats=[w_bout], n_red=1,
                                 name="b_out_bwd")
    grads["b_w_out"] = _mm(y5in, dy5, ta=True, name="b_out_dw").reshape(N_DEV, LANES, D_MODEL)
    dq, dk, dv = _attn_bwd(q_, k_, v_, o, o_low, do)
    grads["b_w_in"] = _mm_dw_cols(h1, [dq, dgb], 4, name="b_proj_dw")
    grads["w_kv"] = _mm_dw_cols(kvn, [dk, dv], 4, name="kv_dw")
    def f_b_in_bwd(x2_, dx, dk_, dv_, dq_, dgb_, gk, gp, wkv_, win_):
        dkvn = _dot_t(dk_, wkv_[:4]) + _dot_t(dv_, wkv_[4:])
        dh1 = _dot_t(dq_, win_[:4]) + _dot_t(dgb_, win_[4:])
        _, vjp = jax.vjp(f_k, x2_, gk, gp)
        dx2_, dgk, dgp = vjp((dkvn, dh1))
        return dx + dx2_, dgk, dgp

    dx2, d_gkv, d_gbpre = _ew(f_b_in_bwd, [x2, dx3, dk, dv, dq, dgb], [g_kv, g_bpre], [F32],
                              mats=[w_kvb, w_bin], n_red=2, name="b_norms_proj_bwd")
    dx1, dpg0, dpp0 = _ew(f_ple_bwd, [dx2, pg0, pp0], [], [F32, BF16, BF16], mats=[w_g0], name="ple0_bwd")
    d_wg0 = _mm(x1, dpg0, ta=True, name="ple0_gate_dw")
    d_wp0 = _mm(p0, dpp0, ta=True, name="ple0_proj_dw")
    grads["ple_w_gate"] = jnp.stack([d_wg0.reshape(N_DEV, LANES, D_MODEL), d_wg1.reshape(N_DEV, LANES, D_MODEL)],
                                    axis=1).reshape(N_DEV, 2 * LANES, D_MODEL)
    grads["ple_w_proj"] = jnp.stack([_cols_to_blocks(d_wp0, N_DEV), _cols_to_blocks(d_wp1, N_DEV)],
                                    axis=1).reshape(N_DEV, 2 * 256, LANES)
    dy3, dy4, d_gapost = _ew(f_norm_bwd_mm, [y4, dx1], [g_apost], [F32, BF16], mats=[w_out], n_red=1,
                             name="a_norm_post_bwd")
    grads["a_w_out"] = _mm(y3, dy4, ta=True, name="a_out_dw").reshape(N_DEV, LANES, D_MODEL)
    def f_a_gate_bwd(ys_, u_, s_, ga_, ct, d_, b_, w):
        g, vjp_gelu = jax.vjp(lambda a, b, c: f_a2(a, b, c)[0], ys_, u_, d_)
        _, vjp_gate = jax.vjp(lambda a, b, c, e: f_a3(a, b, c, e)[0], g, s_, ga_, b_)
        dg, ds_, dga_, db = vjp_gate(ct)
        ds_ = ds_.astype(BF16)
        dys_, du_, dd = vjp_gelu(dg + _dot_t(ds_, w))
        return ds_, dga_, dys_, du_, db, dd

    ds, dga, dys, du_elem, d_bglu, d_dskip = _ew(f_a_gate_bwd, [ys, u, s_glu, ga, dy3], [d_skip, b_glu],
                                                 [BF16, BF16, F32, F32], mats=[w_glu], n_red=2, name="a_gate_bwd")
    grads["a_w_glu"] = _mm(g_act, ds, ta=True, name="a_glu_dw").reshape(N_DEV, LANES, D_MODEL)
    late_sums = _pair_sums([grads[n] for n in late], late, name="pair_exchange_late")
    du, d_bbd, d_ccd, d_are, d_aim, late_parts = _s5_bwd(u, dys, du_elem, hb, bbd_bf, ccd_bf, a_re, a_im,
                                                         plan=_chip_exchange_plan(late_sums))
    grads["a_w_in"] = _mm_dw_cols(h0, [du, dga], 4, name="a_proj_dw")
    def f_a_in_bwd(x0_, dx, du_, dga_, g_, w):
        dh0 = _dot_t(du_, w[:4]) + _dot_t(dga_, w[4:])
        _, vjp = jax.vjp(_rms, x0_, g_)
        dx0, dg = vjp(dh0)
        return dx + dx0, dg

    grad_x, d_gapre = _ew(f_a_in_bwd, [x0, dx1, du, dga], [g_apre], [F32], mats=[w_ain], n_red=1,
                          name="a_norm_proj_bwd")

    d_s5 = s5_vjp((d_are, d_aim, d_bbd, d_ccd))
    small = dict(zip(["a_lam_re", "a_lam_im", "a_log_dt", "a_b_re", "a_b_im", "a_c_re", "a_c_im"], d_s5))
    small.update(kv_norm=d_gkv, b_norm_pre=d_gbpre, b_norm_post=d_gbpost)
    small_names = list(small)
    small_flat = jnp.concatenate([small[n].reshape(-1) for n in small_names])
    small_pack = jnp.pad(small_flat, (0, -small_flat.size % (8 * LANES))).reshape(-1, LANES)
    vec_grads = jnp.concatenate([d_gapre, d_gapost, d_dskip, d_bglu], axis=0)
    vec_scatter = vec_grads.reshape(len(vec_names), N_DEV, LANES).transpose(1, 0, 2)
    early_sums = _pair_sums([grads[n] for n in early] + [vec_scatter], early + ["vectors"], name="pair_exchange")
    scattered = _run_plan(_chip_exchange_plan(early_sums), name="chip_exchange")
    (small_all,) = _gather([small_pack], name="gather_small_grads")

    out_g, out_d, out_m, out_v = {}, {}, {}, {}

    def update(n, parts):
        shape = weights[n].shape
        rc = parts.shape[1:]
        res = _adamw(parts, weights[n].reshape(rc), mom_m[n].reshape(rc), mom_v[n].reshape(rc), name="adamw_" + n)
        out_g[n], out_d[n], out_m[n], out_v[n] = (r.reshape(shape) for r in res)

    for n, parts in list(zip(early, scattered[:-1])) + list(zip(late, late_parts)):
        update(n, parts)
    for i, n in enumerate(vec_names):
        update(n, scattered[-1][:, i:i + 1, :])
    off = 0
    for n in small_names:
        size = small[n].size
        rc = (size // LANES, LANES) if size % LANES == 0 else (1, size)
        update(n, small_all.reshape(N_DEV, -1)[:, off:off + size].reshape((N_DEV,) + rc))
        off += size

    return (loss, grad_x[None], *[out_g[n] for n in names], *[out_d[n] for n in names],
            *[out_m[n] for n in names], *[out_v[n] for n in names])
```

```python
import functools
import math

import jax
import jax.numpy as jnp
from jax import lax
from jax.experimental import pallas as pl
from jax.experimental.pallas import tpu as pltpu

F32 = jnp.float32
BF16 = jnp.bfloat16

N_DEV = 8
D_MODEL = 1024
GROUP_SIZE = 16
N_GROUPS = D_MODEL // GROUP_SIZE
STATE = 64
HEAD_DIM = 64
EPS = 1e-6
LANES = 128
GROUPS_PER_BLOCK = LANES // GROUP_SIZE
N_GBLK = N_GROUPS // GROUPS_PER_BLOCK
STATE_COLS = 2 * GROUPS_PER_BLOCK * STATE
N_SLAB = STATE_COLS // LANES
VMEM_LIMIT = 56 * 1024 * 1024
EXP_UNDERFLOW = -104.0
ATTN_LANES = 256
MM_TILE = 1024
EW_ROWS_BYTES = 24 * 1024 * 1024

ADAM_LR = 0.001
ADAM_B1 = 0.9
ADAM_B2 = 0.999
ADAM_EPS = 1e-08
ADAM_WD = 0.01
ADAM_STEP = 10


def _pallas(body, **kw):
    return pl.pallas_call(body, **kw)


def _cparams(*sem):
    return pltpu.CompilerParams(dimension_semantics=sem, vmem_limit_bytes=VMEM_LIMIT)


def _mm(a, b, *, ta=False, tb=False, add=None, out_dtype=F32, name):
    m, k = (a.shape[1], a.shape[0]) if ta else a.shape
    n = b.shape[0] if tb else b.shape[1]
    tm, tn, tk = min(m, MM_TILE), min(n, MM_TILE), min(k, MM_TILE)
    nk = k // tk
    grid = (m // tm, n // tn, nk)
    a_spec = pl.BlockSpec((tk, tm), lambda i, j, l: (l, i)) if ta else pl.BlockSpec((tm, tk), lambda i, j, l: (i, l))
    b_spec = pl.BlockSpec((tn, tk), lambda i, j, l: (j, l)) if tb else pl.BlockSpec((tk, tn), lambda i, j, l: (l, j))
    o_spec = pl.BlockSpec((tm, tn), lambda i, j, l: (i, j))
    dims = (((0 if ta else 1,), (1 if tb else 0,)), ((), ()))
    has_add = add is not None
    assert nk == 1 or (out_dtype == F32 and not has_add)

    def body(*refs):
        a_ref, b_ref, o_ref = refs[0], refs[1], refs[-1]
        part = lax.dot_general(a_ref[...].astype(BF16), b_ref[...].astype(BF16), dims, preferred_element_type=F32)
        if nk == 1:
            if has_add:
                part = part + refs[2][...].astype(F32)
            o_ref[...] = part.astype(out_dtype)
        else:
            l = pl.program_id(2)

            @pl.when(l == 0)
            def _():
                o_ref[...] = part

            @pl.when(l > 0)
            def _():
                o_ref[...] += part

    ins = [a, b] + ([add] if has_add else [])
    specs = [a_spec, b_spec] + ([o_spec] if has_add else [])
    return _pallas(
        body, name=name, grid=grid, in_specs=specs, out_specs=o_spec,
        out_shape=jax.ShapeDtypeStruct((m, n), out_dtype),
        compiler_params=_cparams("parallel", "parallel", "arbitrary"),
    )(*ins)


def _mm_dw_cols(a, bs, n_blk, *, name, plan=None):
    t, m = a.shape
    n = bs[0].shape[1]
    c = n // n_blk
    tk = min(t, MM_TILE)
    nk = t // tk
    nb = len(bs)

    def body(*refs):
        l = pl.program_id(0)
        (a_ref, *b_refs), (o_ref,), _, run = _hosted(plan, l, nk, refs, 1 + nb, 1, 0)
        run(0, 0)
        run(1, nk // 2)

        @pl.when(l == 0)
        def _():
            o_ref[...] = jnp.zeros_like(o_ref)

        a_bf = a_ref[...].astype(BF16)
        for bi, b_ref in enumerate(b_refs):
            part = lax.dot_general(a_bf, b_ref[...].astype(BF16), (((0,), (0,)), ((), ())), preferred_element_type=F32)
            for j in range(n_blk):
                o_ref[bi * n_blk + j] += part[:, j * c:(j + 1) * c]
        run(2, nk - 1)

    ex = plan or dict(ins=[], out_shape=[], scratch=[])
    res = _pallas(
        body, name=name, grid=(nk,),
        in_specs=[pl.BlockSpec((tk, m), lambda l: (l, 0))] + [pl.BlockSpec((tk, n), lambda l: (l, 0))] * nb
        + [ANY_SPEC] * len(ex["ins"]),
        out_specs=[pl.BlockSpec((nb * n_blk, m, c), lambda l: (0, 0, 0))] + [ANY_SPEC] * len(ex["out_shape"]),
        out_shape=[jax.ShapeDtypeStruct((nb * n_blk, m, c), F32)] + ex["out_shape"],
        scratch_shapes=ex["scratch"],
        compiler_params=_cparams("arbitrary"),
    )(a, *bs, *ex["ins"])
    return (res[0], res[1:]) if plan else res[0]


def _row_spec(arr, tm):
    return pl.BlockSpec((tm, arr.shape[1]), lambda i: (i, 0))


def _full_spec(arr):
    return pl.BlockSpec(arr.shape, lambda i: (0,) * arr.ndim)


def _ew_tile_rows(t, row_bytes):
    tm = 1024
    while tm > 8 and (t % tm or 2 * tm * row_bytes > EW_ROWS_BYTES):
        tm //= 2
    return tm


def _ew(fn, rows, vecs, out_dtypes, *, mats=(), n_red=0, name):
    t = rows[0].shape[0]
    nr, nv, nm = len(rows), len(vecs), len(mats)

    def tile_shapes(tm):
        return jax.eval_shape(fn, *[jax.ShapeDtypeStruct((tm, r.shape[1]), F32) for r in rows],
                              *[jax.ShapeDtypeStruct(v.shape, F32) for v in vecs],
                              *[jax.ShapeDtypeStruct(m_.shape, m_.dtype) for m_ in mats])

    probe = tile_shapes(8)
    row_bytes = sum(r.shape[1] * r.dtype.itemsize for r in rows)
    row_bytes += sum(s.shape[1] * jnp.dtype(dt).itemsize for s, dt in zip(probe, out_dtypes))
    tm = _ew_tile_rows(t, row_bytes)
    shapes = tile_shapes(tm)
    n_out = len(shapes) - n_red

    def body(*refs):
        ins = [r[...].astype(F32) for r in refs[:nr + nv]] + [r[...] for r in refs[nr + nv:nr + nv + nm]]
        outs = fn(*ins)
        o_refs = refs[nr + nv + nm:]
        for o_ref, o in zip(o_refs[:n_out], outs[:n_out]):
            o_ref[...] = o.astype(o_ref.dtype)
        i = pl.program_id(0)
        for o_ref, o in zip(o_refs[n_out:], outs[n_out:]):
            @pl.when(i == 0)
            def _(o_ref=o_ref, o=o):
                o_ref[...] = o

            @pl.when(i > 0)
            def _(o_ref=o_ref, o=o):
                o_ref[...] += o

    out_shape = [jax.ShapeDtypeStruct((t, s.shape[1]), dt) for s, dt in zip(shapes[:n_out], out_dtypes)]
    out_shape += [jax.ShapeDtypeStruct(s.shape, F32) for s in shapes[n_out:]]
    out_specs = [pl.BlockSpec((tm, s.shape[1]), lambda i: (i, 0)) for s in shapes[:n_out]]
    out_specs += [pl.BlockSpec(s.shape, lambda i: (0, 0)) for s in shapes[n_out:]]
    return _pallas(
        body, name=name, grid=(t // tm,),
        in_specs=[_row_spec(r, tm) for r in rows] + [_full_spec(v) for v in list(vecs) + list(mats)],
        out_specs=out_specs, out_shape=out_shape,
        compiler_params=_cparams("arbitrary" if n_red else "parallel"),
    )(*rows, *vecs, *mats)


def _dot(a, w):
    a = a.astype(BF16)
    if w.ndim == 3:
        return jnp.concatenate([jnp.dot(a, w[j], preferred_element_type=F32) for j in range(w.shape[0])], axis=1)
    return jnp.dot(a, w, preferred_element_type=F32)


def _dot_t(a, w):
    a = a.astype(BF16)
    nt = (((1,), (1,)), ((), ()))
    if w.ndim == 3:
        c = w.shape[2]
        return sum(lax.dot_general(a[:, j * c:(j + 1) * c], w[j], nt, preferred_element_type=F32) for j in range(w.shape[0]))
    return lax.dot_general(a, w, nt, preferred_element_type=F32)


def _ew_bwd(fn, rows, vecs, cts, row_wrt, vec_wrt, out_dtypes, *, name):
    nr, nv, nc = len(rows), len(vecs), len(cts)

    def grad_fn(*tiles):
        ins = list(tiles[:nr + nv])
        ct = tiles[nr + nv:]
        wrt = list(row_wrt) + [nr + j for j in vec_wrt]

        def f(*sel):
            full = list(ins)
            for idx, s in zip(wrt, sel):
                full[idx] = s
            return tuple(fn(*full))

        _, vjp = jax.vjp(f, *[ins[idx] for idx in wrt])
        return vjp(tuple(ct))

    return _ew_reordered(grad_fn, rows, vecs, cts, out_dtypes, len(vec_wrt), name)


def _ew_reordered(grad_fn, rows, vecs, cts, out_dtypes, n_red, name):
    nr, nv = len(rows), len(vecs)

    def fn(*tiles):
        r, c, v = tiles[:nr], tiles[nr:nr + len(cts)], tiles[nr + len(cts):]
        return grad_fn(*r, *v, *c)

    return _ew(fn, list(rows) + list(cts), vecs, out_dtypes, n_red=n_red, name=name)


def _rms(x, g):
    return x * lax.rsqrt(jnp.mean(x * x, axis=-1, keepdims=True) + EPS) * g


def _sigmoid(x):
    return 1.0 / (1.0 + jnp.exp(-x))


def _silu(x):
    return x * _sigmoid(x)


def _gelu(x):
    return 0.5 * x * (1.0 + jnp.tanh(math.sqrt(2.0 / math.pi) * (x + 0.044715 * x * x * x)))


def _token_rows(t):
    return pl.ds(pl.multiple_of(t * N_GBLK, N_GBLK), N_GBLK)


def _block_rows(gb, tt):
    return pl.ds(gb, tt, stride=N_GBLK)


def _s5_expand(u_bf, bbd_ref, xs_ref, tt):
    for gb in range(N_GBLK):
        xg = jnp.dot(u_bf[:, gb * LANES:(gb + 1) * LANES], bbd_ref[gb], preferred_element_type=F32)
        for l in range(N_SLAB):
            xs_ref[l, _block_rows(gb, tt), :] = xg[:, l * LANES:(l + 1) * LANES]


def _s5_forward_scan(xs_ref, ar, ai, state, tt):
    half = N_SLAB // 2

    def step(t, st):
        new = [None] * N_SLAB
        for l in range(half):
            sr, si = st[l], st[half + l]
            rows = _token_rows(t)
            nr_ = ar[l] * sr - ai[l] * si + xs_ref[l, rows, :]
            ni_ = ar[l] * si + ai[l] * sr + xs_ref[half + l, rows, :]
            xs_ref[l, rows, :] = nr_
            xs_ref[half + l, rows, :] = ni_
            new[l], new[half + l] = nr_, ni_
        return tuple(new)

    return lax.fori_loop(0, tt, step, tuple(state), unroll=4)


def _slab_rows(ref, gb, tt):
    return jnp.concatenate([ref[l, _block_rows(gb, tt), :] for l in range(N_SLAB)], axis=1)


def _hosted(plan, i, nt, refs, n_in, n_out, n_scratch):
    k_in = len(plan["ins"]) if plan else 0
    k_out = len(plan["out_shape"]) if plan else 0
    own_in, ex_in = refs[:n_in], refs[n_in:n_in + k_in]
    o0 = n_in + k_in
    own_out, ex_out = refs[o0:o0 + n_out], refs[o0 + n_out:o0 + n_out + k_out]
    s0 = o0 + n_out + k_out
    own_scr, ex_sems = refs[s0:s0 + n_scratch], refs[s0 + n_scratch:]

    def run(phase, step):
        if plan:
            @pl.when(i == step)
            def _():
                plan["phases"][phase](ex_in, ex_out, ex_sems)

    return own_in, own_out, own_scr, run


def _s5_fwd(u, bbd, ccd, a_re, a_im, *, tt=256, plan=None):
    t = u.shape[0]
    nt = t // tt
    half = N_SLAB // 2

    def body(*refs):
        i = pl.program_id(0)
        (u_ref, bbd_ref, ccd_ref, ar_ref, ai_ref), (y_ref, hb_ref), (xs_ref, st_ref), run = _hosted(plan, i, nt, refs, 5, 2, 2)
        run(0, 0)
        run(1, nt // 2)

        @pl.when(i == 0)
        def _():
            st_ref[...] = jnp.zeros_like(st_ref)

        hb_ref[0] = st_ref[...]
        _s5_expand(u_ref[...].astype(BF16), bbd_ref, xs_ref, tt)
        ar = [ar_ref[l] for l in range(half)]
        ai = [ai_ref[l] for l in range(half)]
        last = _s5_forward_scan(xs_ref, ar, ai, [st_ref[l] for l in range(N_SLAB)], tt)
        for l in range(N_SLAB):
            st_ref[l] = last[l]
        for gb in range(N_GBLK):
            sg = _slab_rows(xs_ref, gb, tt).astype(BF16)
            y_ref[:, gb * LANES:(gb + 1) * LANES] = jnp.dot(sg, ccd_ref[gb], preferred_element_type=F32)
        run(2, nt - 1)

    ex = plan or dict(ins=[], out_shape=[], scratch=[])
    res = _pallas(
        body, name="s5_fwd", grid=(nt,),
        in_specs=[pl.BlockSpec((tt, D_MODEL), lambda i: (i, 0)), _full_spec(bbd), _full_spec(ccd),
                  _full_spec(a_re), _full_spec(a_im)] + [ANY_SPEC] * len(ex["ins"]),
        out_specs=[pl.BlockSpec((tt, D_MODEL), lambda i: (i, 0)),
                   pl.BlockSpec((1, N_SLAB, N_GBLK, LANES), lambda i: (i, 0, 0, 0))] + [ANY_SPEC] * len(ex["out_shape"]),
        out_shape=[jax.ShapeDtypeStruct((t, D_MODEL), F32), jax.ShapeDtypeStruct((nt, N_SLAB, N_GBLK, LANES), F32)]
        + ex["out_shape"],
        scratch_shapes=[pltpu.VMEM((N_SLAB, N_GBLK * tt, LANES), F32), pltpu.VMEM((N_SLAB, N_GBLK, LANES), F32)]
        + ex["scratch"],
        compiler_params=_cparams("arbitrary"),
    )(u, bbd, ccd, a_re, a_im, *ex["ins"])
    return res[0], res[1], res[2:]


def _s5_bwd(u, dy, du_add, hb, bbd, ccd, a_re, a_im, *, tt=256, plan=None):
    t = u.shape[0]
    nt = t // tt
    half = N_SLAB // 2
    rev = lambda i: (nt - 1 - i, 0)

    def body(*refs):
        i = pl.program_id(0)
        own_in, own_out, own_scr, run = _hosted(plan, i, nt, refs, 8, 5, 3)
        u_ref, dy_ref, dua_ref, hb_ref, bbd_ref, ccd_ref, ar_ref, ai_ref = own_in
        du_ref, dbbd_ref, dccd_ref, dar_ref, dai_ref = own_out
        xs_ref, es_ref, lam_ref = own_scr
        run(0, 0)
        run(1, nt // 2)

        @pl.when(i == 0)
        def _():
            lam_ref[...] = jnp.zeros_like(lam_ref)
            dbbd_ref[...] = jnp.zeros_like(dbbd_ref)
            dccd_ref[...] = jnp.zeros_like(dccd_ref)
            dar_ref[...] = jnp.zeros_like(dar_ref)
            dai_ref[...] = jnp.zeros_like(dai_ref)

        u_bf = u_ref[...].astype(BF16)
        dy_bf = dy_ref[...].astype(BF16)
        _s5_expand(u_bf, bbd_ref, xs_ref, tt)
        ar = [ar_ref[l] for l in range(half)]
        ai = [ai_ref[l] for l in range(half)]
        start = [hb_ref[0, l] for l in range(N_SLAB)]
        _s5_forward_scan(xs_ref, ar, ai, start, tt)
        for gb in range(N_GBLK):
            eg = lax.dot_general(dy_bf[:, gb * LANES:(gb + 1) * LANES], ccd_ref[gb], (((1,), (1,)), ((), ())),
                                 preferred_element_type=F32)
            for l in range(N_SLAB):
                es_ref[l, _block_rows(gb, tt), :] = eg[:, l * LANES:(l + 1) * LANES]

        def lam_step(k, lam):
            rows = _token_rows(tt - 1 - k)
            new_lam = [None] * N_SLAB
            for l in range(half):
                lr = es_ref[l, rows, :] + ar[l] * lam[l] + ai[l] * lam[half + l]
                li = es_ref[half + l, rows, :] - ai[l] * lam[l] + ar[l] * lam[half + l]
                es_ref[l, rows, :] = lr
                es_ref[half + l, rows, :] = li
                new_lam[l], new_lam[half + l] = lr, li
            return tuple(new_lam)

        lam = lax.fori_loop(0, tt, lam_step, tuple(lam_ref[l] for l in range(N_SLAB)), unroll=4)
        for l in range(N_SLAB):
            lam_ref[l] = lam[l]

        def over_tokens(prod):
            return jnp.sum(prod.reshape(tt - 1, N_GBLK, LANES), axis=0)

        later_rows, earlier_rows = pl.ds(N_GBLK, (tt - 1) * N_GBLK), pl.ds(0, (tt - 1) * N_GBLK)
        first = pl.ds(0, N_GBLK)
        for l in range(half):
            lr, li = es_ref[l, later_rows, :], es_ref[half + l, later_rows, :]
            pr, pi = xs_ref[l, earlier_rows, :], xs_ref[half + l, earlier_rows, :]
            lr0, li0 = es_ref[l, first, :], es_ref[half + l, first, :]
            dar_ref[l] += over_tokens(lr * pr + li * pi) + lr0 * start[l] + li0 * start[half + l]
            dai_ref[l] += over_tokens(li * pr - lr * pi) + li0 * start[l] - lr0 * start[half + l]

        for gb in range(N_GBLK):
            cols = slice(gb * LANES, (gb + 1) * LANES)
            lg = _slab_rows(es_ref, gb, tt).astype(BF16)
            sg = _slab_rows(xs_ref, gb, tt).astype(BF16)
            du_ref[:, cols] = (dua_ref[:, cols] + lax.dot_general(
                lg, bbd_ref[gb], (((1,), (1,)), ((), ())), preferred_element_type=F32)).astype(du_ref.dtype)
            dbbd_ref[gb] += lax.dot_general(u_bf[:, cols], lg, (((0,), (0,)), ((), ())), preferred_element_type=F32)
            dccd_ref[gb] += lax.dot_general(sg, dy_bf[:, cols], (((0,), (0,)), ((), ())), preferred_element_type=F32)
        run(2, nt - 1)

    row = pl.BlockSpec((tt, D_MODEL), rev)
    small = pl.BlockSpec((half, N_GBLK, LANES), lambda i: (0, 0, 0))
    ex = plan or dict(ins=[], out_shape=[], scratch=[])
    res = _pallas(
        body, name="s5_bwd", grid=(nt,),
        in_specs=[row, row, row, pl.BlockSpec((1, N_SLAB, N_GBLK, LANES), lambda i: (nt - 1 - i, 0, 0, 0)),
                  _full_spec(bbd), _full_spec(ccd), _full_spec(a_re), _full_spec(a_im)] + [ANY_SPEC] * len(ex["ins"]),
        out_specs=[row, _full_spec(bbd), _full_spec(ccd), small, small] + [ANY_SPEC] * len(ex["out_shape"]),
        out_shape=[jax.ShapeDtypeStruct((t, D_MODEL), BF16), jax.ShapeDtypeStruct(bbd.shape, F32),
                   jax.ShapeDtypeStruct(ccd.shape, F32), jax.ShapeDtypeStruct(a_re.shape, F32),
                   jax.ShapeDtypeStruct(a_im.shape, F32)] + ex["out_shape"],
        scratch_shapes=[pltpu.VMEM((N_SLAB, N_GBLK * tt, LANES), F32), pltpu.VMEM((N_SLAB, N_GBLK * tt, LANES), F32),
                        pltpu.VMEM((N_SLAB, N_GBLK, LANES), F32)] + ex["scratch"],
        compiler_params=_cparams("arbitrary"),
    )(u, dy, du_add, hb, bbd, ccd, a_re, a_im, *ex["ins"])
    return (*res[:5], res[5:])


def _s5_discretize(lam_re, lam_im, log_dt, b_re, b_im, c_re, c_im):
    lr = jnp.minimum(lam_re, -1e-4)
    li = lam_im
    dt = jnp.exp(log_dt)[:, None]
    mag = jnp.exp(lr * dt)
    a_re = mag * jnp.cos(li * dt)
    a_im = mag * jnp.sin(li * dt)
    den = lr * lr + li * li
    nr = a_re - 1.0
    f_re = (nr * lr + a_im * li) / den
    f_im = (a_im * lr - nr * li) / den
    bb_re = f_re[..., None] * b_re - f_im[..., None] * b_im
    bb_im = f_re[..., None] * b_im + f_im[..., None] * b_re
    eye = jnp.eye(GROUPS_PER_BLOCK, dtype=F32)
    bb = jnp.stack([bb_re, bb_im]).reshape(2, N_GBLK, GROUPS_PER_BLOCK, STATE, GROUP_SIZE)
    bbd = jnp.einsum('rgjph,jk->gjhrkp', bb, eye).reshape(N_GBLK, LANES, STATE_COLS)
    cc = jnp.stack([c_re, -c_im]).reshape(2, N_GBLK, GROUPS_PER_BLOCK, GROUP_SIZE, STATE)
    ccd = jnp.einsum('rgjhp,jk->grjpkh', cc, eye).reshape(N_GBLK, STATE_COLS, LANES)

    def dense(a):
        return a.reshape(N_GBLK, N_SLAB // 2, LANES).transpose(1, 0, 2)

    return dense(a_re), dense(a_im), bbd, ccd


def _attn_tiles(tq):
    row = lax.broadcasted_iota(jnp.int32, (tq, tq), 0)
    col = lax.broadcasted_iota(jnp.int32, (tq, tq), 1)
    return col < row, (row > col).astype(BF16), (row >= col).astype(BF16)


def _split_bf16(x):
    hi = x.astype(BF16)
    return hi, (x - hi.astype(F32)).astype(BF16)


def _attn_scores(qs, kj, cs, diag, later, masked):
    zs = [lax.dot_general(qa, kj, (((1,), (1,)), ((), ())), preferred_element_type=F32) * (HEAD_DIM ** -0.5) for qa in qs]
    lks = [-(jnp.maximum(z, 0.0) + jnp.log(1.0 + jnp.exp(-jnp.abs(z)))) for z in zs]
    if masked:
        lks = [jnp.where(diag, lk, 0.0) for lk in lks]
    parts = [_split_bf16(lk) for lk in lks]
    tails = [jnp.dot(hi, later, preferred_element_type=F32) + jnp.dot(lo, later, preferred_element_type=F32)
             for hi, lo in parts]
    ws = [jnp.exp(z + lk + tail + c) for z, lk, tail, c in zip(zs, lks, tails, cs)]
    if masked:
        ws = [jnp.where(diag, w, 0.0) for w in ws]
    return zs, lks, ws


def _head_masks(width=LANES):
    lane = lax.broadcasted_iota(jnp.int32, (1, width), 1)
    return [(lane >= a * HEAD_DIM) & (lane < (a + 1) * HEAD_DIM) for a in range(width // HEAD_DIM)]


def _any_alive(cs):
    alive = jnp.max(cs[0]) > EXP_UNDERFLOW
    for c in cs[1:]:
        alive = alive | (jnp.max(c) > EXP_UNDERFLOW)
    return alive


def _attn_fwd(q, k, v, *, tq=256):
    t = q.shape[0]
    tq = min(tq, t)
    width = ATTN_LANES

    def body(q_ref, k_ref, v_ref, o_ref, ol_ref):
        i = pl.program_id(1)
        diag, later, _ = _attn_tiles(tq)
        masks = _head_masks(width)
        q2 = q_ref[...]
        qs = [jnp.where(hm, q2, jnp.zeros_like(q2)) for hm in masks]

        def block(j, state, masked):
            rows = pl.ds(pl.multiple_of(j * tq, tq), tq)
            kj, vj = k_ref[rows, :], v_ref[rows, :]
            _, lks, ws = _attn_scores(qs, kj, [h[0] for h in state], diag, later, masked)
            parts = [_split_bf16(w) for w in ws]
            return tuple((c + jnp.sum(lk, axis=1, keepdims=True),
                          acc + jnp.dot(w_hi, vj, preferred_element_type=F32),
                          low + jnp.dot(w_lo, vj, preferred_element_type=F32))
                         for (c, acc, low), lk, (w_hi, w_lo) in zip(state, lks, parts))

        zero = jnp.zeros((tq, width), F32)
        state = block(i, tuple((jnp.zeros((tq, 1), F32), zero, zero) for _ in masks), True)

        def cond(st):
            return (st[0] >= 0) & _any_alive([h[0] for h in st[1]])

        def step(st):
            return st[0] - 1, block(st[0], st[1], False)

        _, state = lax.while_loop(cond, step, (i - 1, state))
        o_ref[...] = sum(jnp.where(hm, h[1], 0.0) for hm, h in zip(masks, state))
        ol_ref[...] = sum(jnp.where(hm, h[2], 0.0) for hm, h in zip(masks, state))

    qspec = pl.BlockSpec((tq, width), lambda h, i: (i, h))
    kspec = pl.BlockSpec((t, width), lambda h, i: (0, h))
    full = jax.ShapeDtypeStruct((t, D_MODEL), F32)
    return _pallas(
        body, name="attn_fwd", grid=(D_MODEL // width, t // tq), in_specs=[qspec, kspec, kspec], out_specs=[qspec, qspec],
        out_shape=[full, full], compiler_params=_cparams("parallel", "arbitrary"),
    )(q, k, v)


def _attn_bwd(q, k, v, o, o_low, do, *, tq=256):
    t = q.shape[0]
    tq = min(tq, t)
    width = ATTN_LANES

    def body(q_ref, k_ref, v_ref, o_ref, ol_ref, do_ref, dq_ref, dk_ref, dv_ref):
        i = pl.program_id(1)

        @pl.when(i == 0)
        def _():
            dk_ref[...] = jnp.zeros_like(dk_ref)
            dv_ref[...] = jnp.zeros_like(dv_ref)

        diag, later, later_eq = _attn_tiles(tq)
        masks = _head_masks(width)
        q2 = q_ref[...]
        do2 = do_ref[...]
        dd = do2.astype(BF16).astype(F32) * (o_ref[...] + ol_ref[...])
        qs = [jnp.where(hm, q2, jnp.zeros_like(q2)) for hm in masks]
        dos = [jnp.where(hm, do2, 0.0).astype(BF16) for hm in masks]
        totals = [jnp.sum(jnp.where(hm, dd, 0.0), axis=1, keepdims=True) for hm in masks]

        def block(j, state, masked):
            rows = pl.ds(pl.multiple_of(j * tq, tq), tq)
            kj, vj = k_ref[rows, :], v_ref[rows, :]
            nt = (((1,), (1,)), ((), ()))
            tn = (((0,), (0,)), ((), ()))
            zs, lks, ws = _attn_scores(qs, kj, [h[0] for h in state], diag, later, masked)
            dws = [lax.dot_general(doa, vj, nt, preferred_element_type=F32) for doa in dos]
            gs = [dw * w for dw, w in zip(dws, ws)]
            parts = [_split_bf16(g) for g in gs]
            from_here = [jnp.dot(hi, later_eq, preferred_element_type=F32) + jnp.dot(lo, later_eq, preferred_element_type=F32)
                         for hi, lo in parts]
            dzs = []
            for z, lk, g, fh, total, h in zip(zs, lks, gs, from_here, totals, state):
                beta = jnp.exp(z + lk)
                dz = g * (1.0 - beta) - beta * (total - h[1] - fh)
                if masked:
                    dz = jnp.where(diag, dz, 0.0)
                dzs.append((dz * (HEAD_DIM ** -0.5)).astype(BF16))
            dk = sum(lax.dot_general(dz, qa, tn, preferred_element_type=F32) for dz, qa in zip(dzs, qs))
            dv = sum(lax.dot_general(w.astype(BF16), doa, tn, preferred_element_type=F32) for w, doa in zip(ws, dos))
            new = [(c + jnp.sum(lk, axis=1, keepdims=True), r + jnp.sum(g, axis=1, keepdims=True),
                    acc + jnp.dot(dz, kj, preferred_element_type=F32))
                   for (c, r, acc), lk, g, dz in zip(state, lks, gs, dzs)]
            dk_ref[rows, :] += dk
            dv_ref[rows, :] += dv
            return tuple(new)

        zc = jnp.zeros((tq, 1), F32)
        state = block(i, tuple((zc, zc, jnp.zeros((tq, width), F32)) for _ in masks), True)

        def cond(st):
            return (st[0] >= 0) & _any_alive([h[0] for h in st[1]])

        def step(st):
            return st[0] - 1, block(st[0], st[1], False)

        _, state = lax.while_loop(cond, step, (i - 1, state))
        dq_ref[...] = sum(jnp.where(hm, h[2], 0.0) for hm, h in zip(masks, state))

    qspec = pl.BlockSpec((tq, width), lambda h, i: (i, h))
    kspec = pl.BlockSpec((t, width), lambda h, i: (0, h))
    kin = pl.BlockSpec((t, width), lambda h, i: (0, h), pipeline_mode=pl.Buffered(1))
    full = jax.ShapeDtypeStruct((t, D_MODEL), F32)
    return _pallas(
        body, name="attn_bwd", grid=(D_MODEL // width, t // tq), in_specs=[qspec, kin, kin, qspec, qspec, qspec],
        out_specs=[qspec, kspec, kspec], out_shape=[full, full, full],
        compiler_params=_cparams("parallel", "arbitrary"),
    )(q, k, v, o, o_low, do)


N_CHIP = 4
MESH_ID = pl.DeviceIdType.MESH
ANY_SPEC = pl.BlockSpec(memory_space=pl.ANY)


def _position():
    return lax.axis_index("x"), lax.axis_index("y"), lax.axis_index("c")


def _other_chips(x, y):
    return [(1 - x, y), (x, 1 - y), (1 - x, 1 - y)]


def _gather_plan(arrs):
    n = len(arrs)

    def parts(ins, outs, sems):
        send_sems, recv_sems, local_sems = sems
        x, y, c = _position()
        me, sibling = (x, y, c), (x, y, 1 - c)
        chips = _other_chips(x, y)

        def copy(a, k, block, to, src=None):
            slot = outs[a].at[4 * block[0] + 2 * block[1] + block[2]]
            return pltpu.make_async_remote_copy(
                src_ref=slot if src is None else src, dst_ref=slot, send_sem=send_sems.at[a, k],
                recv_sem=recv_sems.at[a, k], device_id=to, device_id_type=MESH_ID)

        local = [pltpu.make_async_copy(ins[a], outs[a].at[4 * x + 2 * y + c], local_sems.at[a]) for a in range(n)]
        first = []
        for a in range(n):
            first.append(copy(a, 0, me, sibling, src=ins[a]))
            first += [copy(a, 1 + j, me, (*chip, c), src=ins[a]) for j, chip in enumerate(chips)]
        passed = [copy(a, 4 + j, (*chip, c), sibling) for a in range(n) for j, chip in enumerate(chips)]
        return me, sibling, chips, c, copy, local, first, passed

    def start(ins, outs, sems):
        *_, local, first, _ = parts(ins, outs, sems)
        for cp in local + first:
            cp.start()

    def forward(ins, outs, sems):
        me, _, chips, c, copy, _, _, passed = parts(ins, outs, sems)
        for a in range(n):
            for j, chip in enumerate(chips):
                copy(a, 1 + j, (*chip, c), me).wait_recv()
                passed[a * len(chips) + j].start()

    def finish(ins, outs, sems):
        me, sibling, chips, c, copy, local, first, passed = parts(ins, outs, sems)
        for a in range(n):
            copy(a, 0, sibling, me).wait_recv()
            for j, chip in enumerate(chips):
                copy(a, 4 + j, (*chip, 1 - c), me).wait_recv()
        for cp in first + passed:
            cp.wait_send()
        for cp in local:
            cp.wait()

    return dict(ins=list(arrs), out_shape=[jax.ShapeDtypeStruct((N_DEV,) + a.shape, a.dtype) for a in arrs],
                scratch=[pltpu.SemaphoreType.DMA((n, N_DEV - 1)), pltpu.SemaphoreType.DMA((n, N_DEV - 1)),
                         pltpu.SemaphoreType.DMA((n,))],
                phases=(start, forward, finish))


def _run_plan(plan, *, name):
    n_in, n_out = len(plan["ins"]), len(plan["out_shape"])

    def body(*refs):
        ins, outs, sems = refs[:n_in], refs[n_in:n_in + n_out], refs[n_in + n_out:]
        for phase in plan["phases"]:
            phase(ins, outs, sems)

    return _pallas(body, name=name, in_specs=[ANY_SPEC] * n_in, out_specs=[ANY_SPEC] * n_out,
                   out_shape=plan["out_shape"], scratch_shapes=plan["scratch"])(*plan["ins"])


def _gather(arrs, *, name):
    return _run_plan(_gather_plan(arrs), name=name)


def _pair_exchange(arrs, *, name):
    n = len(arrs)

    def body(*refs):
        ins, got = refs[:n], refs[n:2 * n]
        send_sems, recv_sems = refs[2 * n:]
        x, y, c = _position()
        sends = []
        for a in range(n):
            for ch in range(N_CHIP):
                sends.append(pltpu.make_async_remote_copy(
                    src_ref=ins[a].at[2 * ch + (1 - c)], dst_ref=got[a].at[ch], send_sem=send_sems.at[a, ch],
                    recv_sem=recv_sems.at[a, ch], device_id=(x, y, 1 - c), device_id_type=MESH_ID))
        for cp in sends:
            cp.start()
        for cp in sends:
            cp.wait_send()
            cp.wait_recv()

    return _pallas(
        body, name=name, in_specs=[ANY_SPEC] * n, out_specs=[ANY_SPEC] * n,
        out_shape=[jax.ShapeDtypeStruct((N_CHIP,) + a.shape[1:], a.dtype) for a in arrs],
        scratch_shapes=[pltpu.SemaphoreType.DMA((n, N_CHIP)), pltpu.SemaphoreType.DMA((n, N_CHIP))],
    )(*arrs)


def _chip_exchange_plan(arrs):
    n = len(arrs)

    def parts(ins, outs, sems):
        send_sems, recv_sems, local_sems = sems
        x, y, c = _position()
        mine = 2 * x + y
        sends, recvs, locals_ = [], [], []
        for a in range(n):
            locals_.append(pltpu.make_async_copy(ins[a].at[mine], outs[a].at[mine], local_sems.at[a]))
            for j, (px, py) in enumerate(_other_chips(x, y)):
                sends.append(pltpu.make_async_remote_copy(
                    src_ref=ins[a].at[2 * px + py], dst_ref=outs[a].at[mine], send_sem=send_sems.at[a, j],
                    recv_sem=recv_sems.at[a, j], device_id=(px, py, c), device_id_type=MESH_ID))
                recvs.append(pltpu.make_async_remote_copy(
                    src_ref=ins[a].at[2 * px + py], dst_ref=outs[a].at[2 * px + py], send_sem=send_sems.at[a, j],
                    recv_sem=recv_sems.at[a, j], device_id=(px, py, c), device_id_type=MESH_ID))
        return sends, recvs, locals_

    def start(ins, outs, sems):
        sends, _, locals_ = parts(ins, outs, sems)
        for cp in locals_ + sends:
            cp.start()

    def nothing(ins, outs, sems):
        pass

    def finish(ins, outs, sems):
        sends, recvs, locals_ = parts(ins, outs, sems)
        for cp in sends:
            cp.wait_send()
        for cp in recvs:
            cp.wait_recv()
        for cp in locals_:
            cp.wait()

    return dict(ins=list(arrs), out_shape=[jax.ShapeDtypeStruct(a.shape, a.dtype) for a in arrs],
                scratch=[pltpu.SemaphoreType.DMA((n, N_CHIP - 1)), pltpu.SemaphoreType.DMA((n, N_CHIP - 1)),
                         pltpu.SemaphoreType.DMA((n,))],
                phases=(start, nothing, finish))


def _pair_sum(mine, got, core, *, name):
    s_, r, c = got.shape
    tr = r if r <= 256 else 256

    def body(core_ref, a_ref, b_ref, o_ref):
        o_ref[...] = a_ref[...] + b_ref[...]

    quarter = pl.BlockSpec((s_, tr, c), lambda i, core_ref: (0, i, 0))
    return _pallas(
        body, name=name, out_shape=jax.ShapeDtypeStruct(got.shape, got.dtype),
        grid_spec=pltpu.PrefetchScalarGridSpec(
            num_scalar_prefetch=1, grid=(r // tr,),
            in_specs=[pl.BlockSpec((s_, None, tr, c), lambda i, core_ref: (0, core_ref[0], i, 0)), quarter],
            out_specs=quarter),
        compiler_params=_cparams("parallel"),
    )(core, mine.reshape(s_, 2, r, c), got)


def _pair_sums(arrs, names, *, name):
    got = _pair_exchange(arrs, name=name)
    core = lax.axis_index("c").astype(jnp.int32).reshape(1)
    return [_pair_sum(a, g_, core, name="pair_sum_" + n) for a, g_, n in zip(arrs, got, names)]


def _adamw(parts, w, m, v, *, name):
    s, r, c = parts.shape
    tr = r if r * c <= 256 * D_MODEL else 256

    def body(p_ref, w_ref, m_ref, v_ref, g_ref, d_ref, nm_ref, nv_ref):
        g = p_ref[0]
        for j in range(1, s):
            g = g + p_ref[j]
        nm = ADAM_B1 * m_ref[...] + (1.0 - ADAM_B1) * g
        nv = ADAM_B2 * v_ref[...] + (1.0 - ADAM_B2) * (g * g)
        m_hat = nm / (1.0 - ADAM_B1 ** ADAM_STEP)
        v_hat = nv / (1.0 - ADAM_B2 ** ADAM_STEP)
        g_ref[...] = g
        d_ref[...] = -ADAM_LR * (m_hat / (jnp.sqrt(v_hat) + ADAM_EPS) + ADAM_WD * w_ref[...])
        nm_ref[...] = nm
        nv_ref[...] = nv

    spec = pl.BlockSpec((tr, c), lambda i: (i, 0))
    one = jax.ShapeDtypeStruct((r, c), F32)
    return _pallas(
        body, name=name, grid=(r // tr,), in_specs=[pl.BlockSpec((s, tr, c), lambda i: (0, i, 0)), spec, spec, spec],
        out_specs=[spec] * 4, out_shape=[one] * 4, compiler_params=_cparams("parallel"),
    )(parts, w, m, v)


def _cols_to_blocks(g, n_blk):
    r = g.shape[0]
    return g.reshape(r, n_blk, g.shape[1] // n_blk).transpose(1, 0, 2)


def _blocks_to_cols(gathered):
    n_blk, r, c = gathered.shape
    return gathered.transpose(1, 0, 2).reshape(r, n_blk * c)


def kernel(x, p, a_norm_pre, a_norm_post, a_w_in, a_lam_re, a_lam_im, a_log_dt, a_b_re, a_b_im, a_c_re, a_c_im, a_d_skip, a_w_glu, a_b_glu, a_w_out, kv_norm, w_kv, b_norm_pre, b_norm_post, b_w_in, b_w_out, ple_w_proj, ple_w_gate, loss_target, m_a_norm_pre, m_a_norm_post, m_a_w_in, m_a_lam_re, m_a_lam_im, m_a_log_dt, m_a_b_re, m_a_b_im, m_a_c_re, m_a_c_im, m_a_d_skip, m_a_w_glu, m_a_b_glu, m_a_w_out, m_kv_norm, m_w_kv, m_b_norm_pre, m_b_norm_post, m_b_w_in, m_b_w_out, m_ple_w_proj, m_ple_w_gate, v_a_norm_pre, v_a_norm_post, v_a_w_in, v_a_lam_re, v_a_lam_im, v_a_log_dt, v_a_b_re, v_a_b_im, v_a_c_re, v_a_c_im, v_a_d_skip, v_a_w_glu, v_a_b_glu, v_a_w_out, v_kv_norm, v_w_kv, v_b_norm_pre, v_b_norm_post, v_b_w_in, v_b_w_out, v_ple_w_proj, v_ple_w_gate):
    weights = dict(a_norm_pre=a_norm_pre, a_norm_post=a_norm_post, a_w_in=a_w_in, a_lam_re=a_lam_re, a_lam_im=a_lam_im, a_log_dt=a_log_dt, a_b_re=a_b_re, a_b_im=a_b_im, a_c_re=a_c_re, a_c_im=a_c_im, a_d_skip=a_d_skip, a_w_glu=a_w_glu, a_b_glu=a_b_glu, a_w_out=a_w_out, kv_norm=kv_norm, w_kv=w_kv, b_norm_pre=b_norm_pre, b_norm_post=b_norm_post, b_w_in=b_w_in, b_w_out=b_w_out, ple_w_proj=ple_w_proj, ple_w_gate=ple_w_gate)
    mom_m = dict(a_norm_pre=m_a_norm_pre, a_norm_post=m_a_norm_post, a_w_in=m_a_w_in, a_lam_re=m_a_lam_re, a_lam_im=m_a_lam_im, a_log_dt=m_a_log_dt, a_b_re=m_a_b_re, a_b_im=m_a_b_im, a_c_re=m_a_c_re, a_c_im=m_a_c_im, a_d_skip=m_a_d_skip, a_w_glu=m_a_w_glu, a_b_glu=m_a_b_glu, a_w_out=m_a_w_out, kv_norm=m_kv_norm, w_kv=m_w_kv, b_norm_pre=m_b_norm_pre, b_norm_post=m_b_norm_post, b_w_in=m_b_w_in, b_w_out=m_b_w_out, ple_w_proj=m_ple_w_proj, ple_w_gate=m_ple_w_gate)
    mom_v = dict(a_norm_pre=v_a_norm_pre, a_norm_post=v_a_norm_post, a_w_in=v_a_w_in, a_lam_re=v_a_lam_re, a_lam_im=v_a_lam_im, a_log_dt=v_a_log_dt, a_b_re=v_a_b_re, a_b_im=v_a_b_im, a_c_re=v_a_c_re, a_c_im=v_a_c_im, a_d_skip=v_a_d_skip, a_w_glu=v_a_w_glu, a_b_glu=v_a_b_glu, a_w_out=v_a_w_out, kv_norm=v_kv_norm, w_kv=v_w_kv, b_norm_pre=v_b_norm_pre, b_norm_post=v_b_norm_post, b_w_in=v_b_w_in, b_w_out=v_b_w_out, ple_w_proj=v_ple_w_proj, ple_w_gate=v_ple_w_gate)
    names = list(weights)

    t = x.shape[1]
    x0 = x[0]
    p0, p1 = p[0, 0], p[1, 0]
    tgt = loss_target[0]

    mats = dict(a_w_in=a_w_in[0], a_w_glu=a_w_glu[0], a_w_out=a_w_out[0], w_kv=w_kv, b_w_in=b_w_in[0],
                b_w_out=b_w_out[0], ple_w_proj=ple_w_proj.reshape(2 * 256, LANES),
                ple_w_gate=ple_w_gate.reshape(2 * LANES, D_MODEL))
    vec_names = ["a_norm_pre", "a_norm_post", "a_d_skip", "a_b_glu"]
    vec_pack = jnp.concatenate([weights[n] for n in vec_names], axis=0)
    early = ["a_w_in"]
    late = [n for n in mats if n not in early]
    gathered = _gather([mats[n].astype(BF16) for n in early] + [vec_pack], name="gather_weights")
    vec_full = gathered[-1].transpose(1, 0, 2).reshape(len(vec_names), D_MODEL)
    g_apre, g_apost, d_skip, b_glu = (vec_full[i:i + 1] for i in range(4))
    w_ain = gathered[0]
    g_kv, g_bpre, g_bpost = kv_norm.reshape(1, D_MODEL), b_norm_pre, b_norm_post

    s5_params = (a_lam_re[0], a_lam_im[0], a_log_dt[0], a_b_re[0], a_b_im[0], a_c_re[0], a_c_im[0])
    (a_re, a_im, bbd, ccd), s5_vjp = jax.vjp(_s5_discretize, *s5_params)
    bbd_bf, ccd_bf = bbd.astype(BF16), ccd.astype(BF16)

    f_a2 = lambda ys, u, d: (_gelu(ys + d * u),)
    f_a3 = lambda g, s, ga, b: (g * _sigmoid(s + b) * _silu(ga),)
    f_rms = lambda y, g: (_rms(y, g),)
    f_k = lambda x2, gk, gp: (_rms(x2, gk), _rms(x2, gp))
    f_b3 = lambda o, gb: (o * _silu(gb),)

    def f_a_in(x0_, g_, w):
        h = _rms(x0_, g_).astype(BF16)
        return h, _dot(h, w[:4]), _dot(h, w[4:])

    h0, u, ga = _ew(f_a_in, [x0], [g_apre], [BF16, F32, F32], mats=[w_ain], name="a_norm_proj")
    ys, hb, late_w = _s5_fwd(u, bbd_bf, ccd_bf, a_re, a_im, plan=_gather_plan([mats[n].astype(BF16) for n in late]))
    gw = dict(zip(late, late_w))
    w_kvb, w_bin = gw["w_kv"], gw["b_w_in"]
    w_glu = gw["a_w_glu"].reshape(D_MODEL, D_MODEL)
    w_out = gw["a_w_out"].reshape(D_MODEL, D_MODEL)
    w_bout = gw["b_w_out"].reshape(D_MODEL, D_MODEL)
    wp = gw["ple_w_proj"].reshape(N_DEV, 2, 256, LANES)
    w_p0, w_p1 = _blocks_to_cols(wp[:, 0]), _blocks_to_cols(wp[:, 1])
    wg = gw["ple_w_gate"].reshape(N_DEV, 2, LANES, D_MODEL)
    w_g0, w_g1 = wg[:, 0].reshape(D_MODEL, D_MODEL), wg[:, 1].reshape(D_MODEL, D_MODEL)

    def f_a_glu(ys_, u_, d_, w):
        g = f_a2(ys_, u_, d_)[0].astype(BF16)
        return g, _dot(g, w)

    g_act, s_glu = _ew(f_a_glu, [ys, u], [d_skip], [BF16, F32], mats=[w_glu], name="a_gelu_glu")
    def f_a_out(ys_, u_, s_, ga_, x0_, d_, b_, g_, w):
        (y3_,) = f_a3(_gelu(ys_ + d_ * u_), s_, ga_, b_)
        y3_ = y3_.astype(BF16)
        y4_ = _dot(y3_, w)
        return y3_, y4_, x0_ + _rms(y4_, g_)

    y3, y4, x1 = _ew(f_a_out, [ys, u, s_glu, ga, x0], [d_skip, b_glu, g_apost], [BF16, F32, F32], mats=[w_out],
                     name="a_gate_out_norm")

    def f_ple(xin, p_, wg_, wp_):
        pg, pp = _dot(xin, wg_), _dot(p_, wp_)
        return xin + _sigmoid(pg) * pp, pg, pp

    x2, pg0, pp0 = _ew(f_ple, [x1, p0], [], [F32, F32, F32], mats=[w_g0, w_p0], name="ple0")
    def f_b_in(x2_, gk, gp, wkv_, win_):
        kvn_, h1_ = (t_.astype(BF16) for t_ in f_k(x2_, gk, gp))
        return kvn_, h1_, _dot(kvn_, wkv_[:4]), _dot(kvn_, wkv_[4:]), _dot(h1_, win_[:4]), _dot(h1_, win_[4:])

    kvn, h1, k_, v_, q_, gb = _ew(f_b_in, [x2], [g_kv, g_bpre], [BF16, BF16, BF16, BF16, BF16, F32],
                                  mats=[w_kvb, w_bin], name="b_norms_proj")
    o, o_low = _attn_fwd(q_, k_, v_)

    def f_b_out(o_, gb_, x2_, g_, w):
        y5in_ = f_b3(o_, gb_)[0].astype(BF16)
        y5_ = _dot(y5in_, w)
        return y5in_, y5_, x2_ + _rms(y5_, g_)

    y5in, y5, x3 = _ew(f_b_out, [o, gb, x2], [g_bpost], [BF16, F32, F32], mats=[w_bout], name="b_gate_out_norm")

    def f_ple_loss(xin, p_, tg, wg_, wp_):
        pg, pp = _dot(xin, wg_), _dot(p_, wp_)
        err = xin + _sigmoid(pg) * pp - tg
        return err * (1.0 / D_MODEL), pg, pp, (0.5 / D_MODEL) * jnp.sum(err * err).reshape(1, 1)

    dx4, pg1, pp1, loss_part = _ew(f_ple_loss, [x3, p1, tgt], [], [F32, F32, F32], mats=[w_g1, w_p1], n_red=1,
                                   name="ple1_loss")
    loss = lax.psum(loss_part[0, 0], ("x", "y", "c"))

    grads = {}
    def f_ple_bwd(dx, pg, pp, wg_):
        sg = _sigmoid(pg)
        dpg = (dx * pp * sg * (1.0 - sg)).astype(BF16)
        return dx + _dot_t(dpg, wg_), dpg, dx * sg

    def f_norm_bwd_mm(y, dx, g_, w):
        _, vjp = jax.vjp(_rms, y, g_)
        dy, dg = vjp(dx)
        dy = dy.astype(BF16)
        return _dot_t(dy, w), dy, dg

    dx3, dpg1, dpp1 = _ew(f_ple_bwd, [dx4, pg1, pp1], [], [F32, BF16, BF16], mats=[w_g1], name="ple1_bwd")
    d_wg1 = _mm(x3, dpg1, ta=True, name="ple1_gate_dw")
    d_wp1 = _mm(p1, dpp1, ta=True, name="ple1_proj_dw")
    def f_b_out_bwd(y, dx, o_, gb_, g_, w):
        dyin, dy, dg = f_norm_bwd_mm(y, dx, g_, w)
        _, vjp = jax.vjp(lambda a, b: f_b3(a, b)[0], o_, gb_)
        do_, dgb_ = vjp(dyin)
        return dy, do_, dgb_, dg

    dy5, do, dgb, d_gbpost = _ew(f_b_out_bwd, [y5, dx3, o, gb], [g_bpost], [BF16, F32, BF16], mats=[w_bout], n_red=1,
                                 name="b_out_bwd")
    grads["b_w_out"] = _mm(y5in, dy5, ta=True, name="b_out_dw").reshape(N_DEV, LANES, D_MODEL)
    dq, dk, dv = _attn_bwd(q_, k_, v_, o, o_low, do)
    grads["b_w_in"] = _mm_dw_cols(h1, [dq, dgb], 4, name="b_proj_dw")
    grads["w_kv"] = _mm_dw_cols(kvn, [dk, dv], 4, name="kv_dw")
    def f_b_in_bwd(x2_, dx, dk_, dv_, dq_, dgb_, gk, gp, wkv_, win_):
        dkvn = _dot_t(dk_, wkv_[:4]) + _dot_t(dv_, wkv_[4:])
        dh1 = _dot_t(dq_, win_[:4]) + _dot_t(dgb_, win_[4:])
        _, vjp = jax.vjp(f_k, x2_, gk, gp)
        dx2_, dgk, dgp = vjp((dkvn, dh1))
        return dx + dx2_, dgk, dgp

    dx2, d_gkv, d_gbpre = _ew(f_b_in_bwd, [x2, dx3, dk, dv, dq, dgb], [g_kv, g_bpre], [F32],
                              mats=[w_kvb, w_bin], n_red=2, name="b_norms_proj_bwd")
    dx1, dpg0, dpp0 = _ew(f_ple_bwd, [dx2, pg0, pp0], [], [F32, BF16, BF16], mats=[w_g0], name="ple0_bwd")
    d_wg0 = _mm(x1, dpg0, ta=True, name="ple0_gate_dw")
    d_wp0 = _mm(p0, dpp0, ta=True, name="ple0_proj_dw")
    grads["ple_w_gate"] = jnp.stack([d_wg0.reshape(N_DEV, LANES, D_MODEL), d_wg1.reshape(N_DEV, LANES, D_MODEL)],
                                    axis=1).reshape(N_DEV, 2 * LANES, D_MODEL)
    grads["ple_w_proj"] = jnp.stack([_cols_to_blocks(d_wp0, N_DEV), _cols_to_blocks(d_wp1, N_DEV)],
                                    axis=1).reshape(N_DEV, 2 * 256, LANES)
    dy3, dy4, d_gapost = _ew(f_norm_bwd_mm, [y4, dx1], [g_apost], [F32, BF16], mats=[w_out], n_red=1,
                             name="a_norm_post_bwd")
    grads["a_w_out"] = _mm(y3, dy4, ta=True, name="a_out_dw").reshape(N_DEV, LANES, D_MODEL)
    def f_a_gate_bwd(ys_, u_, s_, ga_, ct, d_, b_, w):
        g, vjp_gelu = jax.vjp(lambda a, b, c: f_a2(a, b, c)[0], ys_, u_, d_)
        _, vjp_gate = jax.vjp(lambda a, b, c, e: f_a3(a, b, c, e)[0], g, s_, ga_, b_)
        dg, ds_, dga_, db = vjp_gate(ct)
        ds_ = ds_.astype(BF16)
        dys_, du_, dd = vjp_gelu(dg + _dot_t(ds_, w))
        return ds_, dga_, dys_, du_, db, dd

    ds, dga, dys, du_elem, d_bglu, d_dskip = _ew(f_a_gate_bwd, [ys, u, s_glu, ga, dy3], [d_skip, b_glu],
                                                 [BF16, BF16, F32, F32], mats=[w_glu], n_red=2, name="a_gate_bwd")
    grads["a_w_glu"] = _mm(g_act, ds, ta=True, name="a_glu_dw").reshape(N_DEV, LANES, D_MODEL)
    late_sums = _pair_sums([grads[n] for n in late], late, name="pair_exchange_late")
    du, d_bbd, d_ccd, d_are, d_aim, late_parts = _s5_bwd(u, dys, du_elem, hb, bbd_bf, ccd_bf, a_re, a_im,
                                                         plan=_chip_exchange_plan(late_sums))
    d_s5 = s5_vjp((d_are, d_aim, d_bbd, d_ccd))
    small = dict(zip(["a_lam_re", "a_lam_im", "a_log_dt", "a_b_re", "a_b_im", "a_c_re", "a_c_im"], d_s5))
    small.update(kv_norm=d_gkv, b_norm_pre=d_gbpre, b_norm_post=d_gbpost)
    small_names = list(small)
    small_flat = jnp.concatenate([small[n].reshape(-1) for n in small_names])
    small_pack = jnp.pad(small_flat, (0, -small_flat.size % (8 * LANES))).reshape(-1, LANES)
    grads["a_w_in"], (small_all,) = _mm_dw_cols(h0, [du, dga], 4, name="a_proj_dw", plan=_gather_plan([small_pack]))
    def f_a_in_bwd(x0_, dx, du_, dga_, g_, w):
        dh0 = _dot_t(du_, w[:4]) + _dot_t(dga_, w[4:])
        _, vjp = jax.vjp(_rms, x0_, g_)
        dx0, dg = vjp(dh0)
        return dx + dx0, dg

    grad_x, d_gapre = _ew(f_a_in_bwd, [x0, dx1, du, dga], [g_apre], [F32], mats=[w_ain], n_red=1,
                          name="a_norm_proj_bwd")

    vec_grads = jnp.concatenate([d_gapre, d_gapost, d_dskip, d_bglu], axis=0)
    vec_scatter = vec_grads.reshape(len(vec_names), N_DEV, LANES).transpose(1, 0, 2)
    early_sums = _pair_sums([grads[n] for n in early] + [vec_scatter], early + ["vectors"], name="pair_exchange")
    scattered = _run_plan(_chip_exchange_plan(early_sums), name="chip_exchange")

    out_g, out_d, out_m, out_v = {}, {}, {}, {}

    def update(n, parts):
        shape = weights[n].shape
        rc = parts.shape[1:]
        res = _adamw(parts, weights[n].reshape(rc), mom_m[n].reshape(rc), mom_v[n].reshape(rc), name="adamw_" + n)
        out_g[n], out_d[n], out_m[n], out_v[n] = (r.reshape(shape) for r in res)

    for n, parts in list(zip(early, scattered[:-1])) + list(zip(late, late_parts)):
        update(n, parts)
    for i, n in enumerate(vec_names):
        update(n, scattered[-1][:, i:i + 1, :])
    off = 0
    for n in small_names:
        size = small[n].size
        rc = (size // LANES, LANES) if size % LANES == 0 else (1, size)
        update(n, small_all.reshape(N_DEV, -1)[:, off:off + size].reshape((N_DEV,) + rc))
        off += size

    return (loss, grad_x[None], *[out_g[n] for n in names], *[out_d[n] for n in names],
            *[out_m[n] for n in names], *[out_v[n] for n in names])
```
